```python
import math
import jax, jax.numpy as jnp
from jax import lax
import numpy as np

D_MODEL = 4096
BATCH = 1
SEQ = 16384
DEPTH = 2

GRID_W = 64
CTX_LEN = 256
HEAD_DIM = 128
CONV_WIDTH = D_MODEL // 2
N_HEADS = (D_MODEL // 2) // HEAD_DIM
N_KV_HEADS = N_HEADS // 4
GQA_GROUP = N_HEADS // N_KV_HEADS
Q_WIDTH = N_HEADS * HEAD_DIM
KV_WIDTH = N_KV_HEADS * HEAD_DIM
SPLIT_POINTS = (CONV_WIDTH, 2 * CONV_WIDTH, 3 * CONV_WIDTH, 3 * CONV_WIDTH + Q_WIDTH,
                3 * CONV_WIDTH + Q_WIDTH + KV_WIDTH)
KV_START = 3 * CONV_WIDTH + Q_WIDTH
IN_WIDTH = 3 * CONV_WIDTH + Q_WIDTH + 2 * KV_WIDTH
MIX_WIDTH = CONV_WIDTH + Q_WIDTH
WINDOW = 128
BLOCK = 128
ATTN_SCALE = HEAD_DIM ** -0.5
ROPE_BASE = 10000.0
ROPE_PAIRS = HEAD_DIM // 4
HYENA_ORDER = 2
FILTER_EMB = 33
FILTER_BANDS = (FILTER_EMB - 1) // 2
FILTER_HIDDEN = 64
MIN_DECAY = math.log(1e-2) / 0.3
MAX_DECAY = math.log(1e-2) / 1.5
D_FF = 11008
N_EXPERTS = 8
TOP_K = 2
D_FF_EXPERT = 3072
NORM_EPS = 1e-6

kernel_name = "hybrid_dit_shortconv_swa_hyena_moe"

F32 = jnp.float32


def rmsnorm(x, g):
    x32 = x.astype(F32)
    y = x32 * lax.rsqrt(jnp.mean(x32 * x32, axis=-1, keepdims=True) + NORM_EPS)
    return (y * g.astype(F32)).astype(x.dtype)


def modulate(h, shift, scale):
    return h * (1.0 + scale) + shift


def adaln(act, w, b, n):
    m = act @ w + b
    return jnp.split(m[..., None, :], n, axis=-1)


def sublayer(x, f, g_pre, g_post, shift, scale, gate):
    return x + gate * rmsnorm(f(modulate(rmsnorm(x, g_pre), shift, scale)), g_post)


def short_conv3(u, w):
    up = jnp.pad(u, ((0, 0), (1, 1), (0, 0)))
    return up[:, :-2] * w[0] + up[:, 1:-1] * w[1] + up[:, 2:] * w[2]


def gated_short_conv(b_gate, c_gate, u, w):
    return b_gate * short_conv3(c_gate * u, w)


def axial_rope_tables(rows):
    row = jnp.broadcast_to(jnp.arange(rows, dtype=F32)[:, None], (rows, GRID_W)).reshape(-1)
    col = jnp.broadcast_to(jnp.arange(GRID_W, dtype=F32)[None, :], (rows, GRID_W)).reshape(-1)
    inv = ROPE_BASE ** (-jnp.arange(ROPE_PAIRS, dtype=F32) / ROPE_PAIRS)
    ang = jnp.stack([row[:, None] * inv, col[:, None] * inv], axis=1)
    return jnp.cos(ang), jnp.sin(ang)


def apply_axial_rope(x, cos, sin):
    B, S, H, _ = x.shape
    xr = x.astype(F32).reshape(B, S, H, 2, 2, ROPE_PAIRS)
    a, b = xr[..., 0, :], xr[..., 1, :]
    cs, sn = cos[None, :, None], sin[None, :, None]
    out = jnp.stack([a * cs - b * sn, b * cs + a * sn], axis=-2)
    return out.reshape(B, S, H, HEAD_DIM).astype(x.dtype)


def sink_softmax(parts, sink):
    m = sink
    for s in parts:
        m = jnp.maximum(m, jnp.max(s, axis=-1, keepdims=True))
    es = [jnp.exp(s - m) for s in parts]
    denom = jnp.exp(sink - m)
    for e in es:
        denom = denom + jnp.sum(e, axis=-1, keepdims=True)
    return [e / denom for e in es]


def key_band(t, nb):
    B = t.shape[0]
    tp = jnp.pad(t, ((0, 0), (BLOCK, BLOCK), (0, 0), (0, 0))).reshape(B, nb + 2, BLOCK, N_KV_HEADS, HEAD_DIM)
    return jnp.concatenate([tp[:, :-2], tp[:, 1:-1], tp[:, 2:]], axis=2)


def band_mask(nb):
    q = jnp.arange(BLOCK)[:, None]
    j = jnp.arange(3 * BLOCK)[None, :]
    in_win = jnp.abs(q + BLOCK - j) <= WINDOW
    key_pos = jnp.arange(nb)[:, None, None] * BLOCK + j[None] - BLOCK
    return in_win[None] & (key_pos >= 0) & (key_pos < nb * BLOCK)


def window_attention(q, k, v, k_ctx, v_ctx, sink):
    B, S = q.shape[:2]
    nb = S // BLOCK
    qb = q.reshape(B, nb, BLOCK, N_KV_HEADS, GQA_GROUP, HEAD_DIM)
    kb, vb = key_band(k, nb), key_band(v, nb)
    s_loc = jnp.einsum('bnqkgd,bnjkd->bnkgqj', qb, kb, preferred_element_type=F32) * ATTN_SCALE
    s_loc = jnp.where(band_mask(nb)[None, :, None, None], s_loc, -jnp.inf)
    s_ctx = jnp.einsum('bnqkgd,bckd->bnkgqc', qb, k_ctx, preferred_element_type=F32) * ATTN_SCALE
    sink_b = sink.astype(F32).reshape(N_KV_HEADS, GQA_GROUP)[None, None, :, :, None, None]
    p_loc, p_ctx = sink_softmax([s_loc, s_ctx], sink_b)
    o = (jnp.einsum('bnkgqj,bnjkd->bnqkgd', p_loc, vb.astype(F32))
         + jnp.einsum('bnkgqc,bckd->bnqkgd', p_ctx, v_ctx.astype(F32)))
    return o.reshape(B, S, Q_WIDTH).astype(q.dtype)


def context_attention(qc, kc, vc, sink):
    B, C = qc.shape[:2]
    qg = qc.reshape(B, C, N_KV_HEADS, GQA_GROUP, HEAD_DIM)
    s = jnp.einsum('bqkgd,bckd->bkgqc', qg, kc, preferred_element_type=F32) * ATTN_SCALE
    sink_b = sink.astype(F32).reshape(N_KV_HEADS, GQA_GROUP)[None, :, :, None, None]
    (p,) = sink_softmax([s], sink_b)
    o = jnp.einsum('bkgqc,bckd->bqkgd', p, vc.astype(F32))
    return o.reshape(B, C, Q_WIDTH).astype(qc.dtype)


def even_mixer(h, hc, w_in, conv_w, sink, w_out, cos, sin, with_ctx_out):
    B, S, _ = h.shape
    Cn = hc.shape[1]
    a_b, a_c, a_h, q, k, v = jnp.split(h @ w_in, SPLIT_POINTS, axis=-1)
    y_conv = gated_short_conv(a_b, a_c, a_h, conv_w)
    q = apply_axial_rope(q.reshape(B, S, N_HEADS, HEAD_DIM), cos, sin)
    k = apply_axial_rope(k.reshape(B, S, N_KV_HEADS, HEAD_DIM), cos, sin)
    v = v.reshape(B, S, N_KV_HEADS, HEAD_DIM)
    if with_ctx_out:
        cb, cc, ch, qc, kc, vc = jnp.split(hc @ w_in, SPLIT_POINTS, axis=-1)
    else:
        kc, vc = jnp.split(hc @ w_in[:, KV_START:], 2, axis=-1)
    kc = kc.reshape(B, Cn, N_KV_HEADS, HEAD_DIM)
    vc = vc.reshape(B, Cn, N_KV_HEADS, HEAD_DIM)
    y_attn = window_attention(q, k, v, kc, vc, sink)
    out = jnp.concatenate([y_conv, y_attn], axis=-1) @ w_out
    if not with_ctx_out:
        return out, None
    yc_attn = context_attention(qc.reshape(B, Cn, N_HEADS, HEAD_DIM), kc, vc, sink)
    outc = jnp.concatenate([gated_short_conv(cb, cc, ch, conv_w), yc_attn], axis=-1) @ w_out
    return out, outc


def hyena_filter_basis(L, w1, b1, freq, w2, b2, w3, b3):
    t = jnp.linspace(0.0, 1.0, L, dtype=F32)[:, None]
    w = (2.0 * math.pi / L) * jnp.arange(L, dtype=F32)[:, None]
    bands = jnp.linspace(1e-4, FILTER_BANDS - 1, FILTER_BANDS, dtype=F32)[None]
    z = jnp.concatenate([t, jnp.cos(bands * w), -jnp.sin(bands * w)], axis=-1)
    fr = freq.astype(F32)
    hid = jnp.sin(fr * (z @ w1.astype(F32) + b1.astype(F32)))
    hid = jnp.sin(fr * (hid @ w2.astype(F32) + b2.astype(F32)))
    hid = jnp.sin(fr * (hid @ w3.astype(F32) + b3.astype(F32)))
    deltas = jnp.abs(jnp.linspace(MIN_DECAY, MAX_DECAY, D_MODEL, dtype=F32))
    decay = jnp.exp(-t * deltas[None])
    return hid, decay


def fft_long_conv(u, h_fwd, h_bwd):
    L = u.shape[1]
    n = 2 * L
    k = jnp.concatenate([h_fwd[:1] + h_bwd[:1], h_fwd[1:], jnp.zeros_like(h_fwd[:1]), h_bwd[:0:-1]], axis=0)
    y = jnp.fft.irfft(jnp.fft.rfft(u, n=n, axis=1) * jnp.fft.rfft(k, n=n, axis=0)[None], n=n, axis=1)
    return y[:, :L]


def hyena_mixer(h, w_in, b_in, conv_w, conv_b, f_w1, f_b1, f_freq, f_w2, f_b2, f_w3, f_b3,
                f_w_out, f_bias, w_out, b_out):
    L = h.shape[1]
    u = short_conv3(h @ w_in + b_in, conv_w) + conv_b
    x1, x2, v = jnp.split(u, 3, axis=-1)
    hid, decay = hyena_filter_basis(L, f_w1, f_b1, f_freq, f_w2, f_b2, f_w3, f_b3)
    w_o = f_w_out.astype(F32).reshape(FILTER_HIDDEN, HYENA_ORDER, 2, D_MODEL)
    z = v.astype(F32)
    for o, gate in enumerate((x1, x2)):
        filt = jnp.einsum('lf,fsd->lsd', hid, w_o[:, o]) * decay[:, None, :]
        filt = filt / (jnp.sum(jnp.abs(filt), axis=(0, 1), keepdims=True) + 1e-6)
        z = gate.astype(F32) * (fft_long_conv(z, filt[:, 0], filt[:, 1]) + f_bias[o].astype(F32) * z)
    return z.astype(h.dtype) @ w_out + b_out


def swiglu(h, w_gu, w_down):
    gate, up = jnp.split(h @ w_gu, 2, axis=-1)
    return (jax.nn.silu(gate) * up) @ w_down


def moe_swiglu(h, router_w, router_b, w_gu, w_down):
    logits = (h @ router_w).astype(F32) + router_b.astype(F32)
    top_val, top_idx = lax.top_k(logits, TOP_K)
    top_w = jax.nn.softmax(top_val, axis=-1)
    gates = jnp.sum(jax.nn.one_hot(top_idx, N_EXPERTS, dtype=F32) * top_w[..., None], axis=-2)
    out = jnp.zeros_like(h)
    for e in range(N_EXPERTS):
        out = out + gates[..., e:e + 1].astype(h.dtype) * swiglu(h, w_gu[e], w_down[e])
    return out


def setup_inputs(seed: int = 0) -> dict:
    key = jax.random.key(seed)
    ks = iter(jax.random.split(key, 48))
    n_even = (DEPTH + 1) // 2
    n_odd = DEPTH // 2

    def nrm(shape, fan_in, scale=1.0):
        return jax.random.normal(next(ks), shape, F32) * (scale * fan_in ** -0.5)

    def small(shape, scale):
        return jax.random.normal(next(ks), shape, F32) * scale

    return {
        "x": small((BATCH, SEQ, D_MODEL), 1.0),
        "c": small((BATCH, D_MODEL), 1.0),
        "ctx": small((BATCH, CTX_LEN, D_MODEL), 1.0),
        "c_ctx": small((D_MODEL,), 1.0),
        "ada_w": nrm((DEPTH, D_MODEL, 6 * D_MODEL), D_MODEL, 0.5),
        "ada_b": small((DEPTH, 6 * D_MODEL), 0.02),
        "norm_g": 1.0 + small((DEPTH, 4, D_MODEL), 0.05),
        "mix_w_in": nrm((n_even, D_MODEL, IN_WIDTH), D_MODEL),
        "mix_conv_w": nrm((n_even, 3, CONV_WIDTH), 3),
        "mix_sink": small((n_even, N_HEADS), 0.5),
        "mix_w_out": nrm((n_even, MIX_WIDTH, D_MODEL), MIX_WIDTH),
        "ffn_w_gu": nrm((n_even, D_MODEL, 2 * D_FF), D_MODEL),
        "ffn_w_down": nrm((n_even, D_FF, D_MODEL), D_FF),
        "hy_w_in": nrm((n_odd, D_MODEL, 3 * D_MODEL), D_MODEL),
        "hy_b_in": small((n_odd, 3 * D_MODEL), 0.02),
        "hy_conv_w": nrm((n_odd, 3, 3 * D_MODEL), 3),
        "hy_conv_b": small((n_odd, 3 * D_MODEL), 0.02),
        "hf_w1": nrm((n_odd, FILTER_EMB, FILTER_HIDDEN), FILTER_EMB),
        "hf_b1": small((n_odd, FILTER_HIDDEN), 0.1),
        "hf_freq": 1.0 + small((n_odd, FILTER_HIDDEN), 0.1),
        "hf_w2": nrm((n_odd, FILTER_HIDDEN, FILTER_HIDDEN), FILTER_HIDDEN),
        "hf_b2": small((n_odd, FILTER_HIDDEN), 0.1),
        "hf_w3": nrm((n_odd, FILTER_HIDDEN, FILTER_HIDDEN), FILTER_HIDDEN),
        "hf_b3": small((n_odd, FILTER_HIDDEN), 0.1),
        "hf_w_out": nrm((n_odd, FILTER_HIDDEN, HYENA_ORDER * 2 * D_MODEL), FILTER_HIDDEN),
        "hf_bias": small((n_odd, HYENA_ORDER, D_MODEL), 0.5),
        "hy_w_out": nrm((n_odd, D_MODEL, D_MODEL), D_MODEL),
        "hy_b_out": small((n_odd, D_MODEL), 0.02),
        "router_w": nrm((n_odd, D_MODEL, N_EXPERTS), D_MODEL),
        "router_b": small((n_odd, N_EXPERTS), 0.01),
        "moe_w_gu": nrm((n_odd, N_EXPERTS, D_MODEL, 2 * D_FF_EXPERT), D_MODEL),
        "moe_w_down": nrm((n_odd, N_EXPERTS, D_FF_EXPERT, D_MODEL), D_FF_EXPERT),
    }


def reference(x, c, ctx, c_ctx, ada_w, ada_b, norm_g,
              mix_w_in, mix_conv_w, mix_sink, mix_w_out, ffn_w_gu, ffn_w_down,
              hy_w_in, hy_b_in, hy_conv_w, hy_conv_b, hf_w1, hf_b1, hf_freq, hf_w2, hf_b2,
              hf_w3, hf_b3, hf_w_out, hf_bias, hy_w_out, hy_b_out,
              router_w, router_b, moe_w_gu, moe_w_down):
    n_tok = x.shape[1]
    rows = n_tok // GRID_W
    cos, sin = axial_rope_tables(rows)
    c_act = jax.nn.silu(c)
    cc_act = jax.nn.silu(c_ctx)
    xc = ctx
    for i in range(DEPTH):
        j = i // 2
        feed_ctx = any(later % 2 == 0 for later in range(i + 1, DEPTH))
        g_pre_m, g_post_m, g_pre_f, g_post_f = norm_g[i]
        sh_m, sc_m, gt_m, sh_f, sc_f, gt_f = adaln(c_act, ada_w[i], ada_b[i], 6)
        if feed_ctx:
            cmods = adaln(cc_act, ada_w[i], ada_b[i], 6)
        elif i % 2 == 0:
            cmods = adaln(cc_act, ada_w[i][:, :2 * D_MODEL], ada_b[i][:2 * D_MODEL], 2)
        if i % 2 == 0:
            h = modulate(rmsnorm(x, g_pre_m), sh_m, sc_m)
            hc = modulate(rmsnorm(xc, g_pre_m), cmods[0], cmods[1])
            out, outc = even_mixer(h, hc, mix_w_in[j], mix_conv_w[j], mix_sink[j], mix_w_out[j],
                                   cos, sin, feed_ctx)
            x = x + gt_m * rmsnorm(out, g_post_m)
            if feed_ctx:
                xc = xc + cmods[2] * rmsnorm(outc, g_post_m)
            ffn = lambda t: swiglu(t, ffn_w_gu[j], ffn_w_down[j])
        else:
            mixer = lambda t: hyena_mixer(t, hy_w_in[j], hy_b_in[j], hy_conv_w[j], hy_conv_b[j],
                                          hf_w1[j], hf_b1[j], hf_freq[j], hf_w2[j], hf_b2[j],
                                          hf_w3[j], hf_b3[j], hf_w_out[j], hf_bias[j],
                                          hy_w_out[j], hy_b_out[j])
            x = sublayer(x, mixer, g_pre_m, g_post_m, sh_m, sc_m, gt_m)
            if feed_ctx:
                xc = sublayer(xc, mixer, g_pre_m, g_post_m, cmods[0], cmods[1], cmods[2])
            ffn = lambda t: moe_swiglu(t, router_w[j], router_b[j], moe_w_gu[j], moe_w_down[j])
        x = sublayer(x, ffn, g_pre_f, g_post_f, sh_f, sc_f, gt_f)
        if feed_ctx:
            xc = sublayer(xc, ffn, g_pre_f, g_post_f, cmods[3], cmods[4], cmods[5])
    return x
```

```python
import functools
import math

import jax
import jax.numpy as jnp
from jax import lax
from jax.experimental import pallas as pl
from jax.experimental.pallas import tpu as pltpu

F32 = jnp.float32
BF16 = jnp.bfloat16

D_MODEL = 4096
GRID_W = 64
HEAD_DIM = 128
CONV_WIDTH = D_MODEL // 2
N_HEADS = (D_MODEL // 2) // HEAD_DIM
N_KV_HEADS = N_HEADS // 4
GQA_GROUP = N_HEADS // N_KV_HEADS
Q_WIDTH = N_HEADS * HEAD_DIM
KV_WIDTH = N_KV_HEADS * HEAD_DIM
Q_START = 3 * CONV_WIDTH
KV_START = Q_START + Q_WIDTH
IN_WIDTH = KV_START + 2 * KV_WIDTH
BLOCK = 128
ATTN_SCALE = HEAD_DIM ** -0.5
ROPE_BASE = 10000.0
ROPE_PAIRS = HEAD_DIM // 4
HYENA_ORDER = 2
FILTER_EMB = 33
FILTER_BANDS = (FILTER_EMB - 1) // 2
FILTER_HIDDEN = 64
MIN_DECAY = math.log(1e-2) / 0.3
MAX_DECAY = math.log(1e-2) / 1.5
N_EXPERTS = 8
NORM_EPS = 1e-6

V7X_VMEM_BYTES = 64 * 1024 * 1024
LANES = 128
F32_SUBLANES = 8
BF16_SUBLANES = 16


def _params(semantics, vmem_bytes):
    limit = min(int(vmem_bytes), V7X_VMEM_BYTES - 4 * 1024 * 1024)
    return pltpu.CompilerParams(dimension_semantics=semantics, vmem_limit_bytes=limit)


def _sigmoid(v):
    return 1.0 / (1.0 + jnp.exp(-v))


def _adaln_kernel(a_ref, w_ref, b_ref, o_ref, acc_ref):
    k = pl.program_id(2)

    @pl.when(k == 0)
    def _():
        acc_ref[...] = jnp.zeros_like(acc_ref)

    a = a_ref[...]
    a = a * _sigmoid(a)
    acc_ref[...] += jnp.dot(a.astype(BF16), w_ref[0].astype(BF16), preferred_element_type=F32)

    @pl.when(k == pl.num_programs(2) - 1)
    def _():
        o_ref[0] = acc_ref[...] + b_ref[0]


def adaln_all(acts, ada_w, ada_b):
    depth, d, n = ada_w.shape
    tn, tk = 2048, 1024
    return pl.pallas_call(
        _adaln_kernel,
        grid=(depth, n // tn, d // tk),
        in_specs=[
            pl.BlockSpec((F32_SUBLANES, tk), lambda l, j, k: (0, k)),
            pl.BlockSpec((1, tk, tn), lambda l, j, k: (l, k, j)),
            pl.BlockSpec((1, 1, tn), lambda l, j, k: (l, 0, j)),
        ],
        out_specs=pl.BlockSpec((1, F32_SUBLANES, tn), lambda l, j, k: (l, 0, j)),
        out_shape=jax.ShapeDtypeStruct((depth, F32_SUBLANES, n), F32),
        scratch_shapes=[pltpu.VMEM((F32_SUBLANES, tn), F32)],
        compiler_params=_params(("parallel", "parallel", "arbitrary"), 2 * tk * tn * 4 + tk * tn * 2 + (4 << 20)),
        name="adaln",
    )(acts, ada_w, ada_b.reshape(depth, 1, n))


def _norm_mod(x, g, sh, sc):
    y = x * lax.rsqrt(jnp.mean(x * x, axis=-1, keepdims=True) + NORM_EPS)
    return (y * g) * (1.0 + sc) + sh


def _norm_mod_kernel(x_ref, g_ref, sh_ref, sc_ref, o_ref):
    o_ref[...] = _norm_mod(x_ref[...], g_ref[...], sh_ref[...], sc_ref[...]).astype(o_ref.dtype)


def norm_mod(x, g, shift, scale, tm=512):
    m, d = x.shape
    tm = min(tm, m)
    row = pl.BlockSpec((1, d), lambda i: (0, 0))
    return pl.pallas_call(
        _norm_mod_kernel,
        grid=(m // tm,),
        in_specs=[pl.BlockSpec((tm, d), lambda i: (i, 0)), row, row, row],
        out_specs=pl.BlockSpec((tm, d), lambda i: (i, 0)),
        out_shape=jax.ShapeDtypeStruct((m, d), BF16),
        compiler_params=_params(("parallel",), 2 * tm * d * 4 + 2 * tm * d * 2 + 3 * tm * d * 4 + (2 << 20)),
        name="norm_mod",
    )(x, g, shift, scale)


def _post_kernel(x_ref, y_ref, g_ref, gt_ref, o_ref):
    y = y_ref[...]
    yn = y * lax.rsqrt(jnp.mean(y * y, axis=-1, keepdims=True) + NORM_EPS)
    o_ref[...] = x_ref[...] + gt_ref[...] * (yn * g_ref[...])


def post_residual(x, y, g, gate, tm=512):
    m, d = x.shape
    row = pl.BlockSpec((1, d), lambda i: (0, 0))
    blk = pl.BlockSpec((tm, d), lambda i: (i, 0))
    return pl.pallas_call(
        _post_kernel,
        grid=(m // tm,),
        in_specs=[blk, blk, row, row],
        out_specs=blk,
        out_shape=jax.ShapeDtypeStruct((m, d), F32),
        compiler_params=_params(("parallel",), 6 * tm * d * 4 + 3 * tm * d * 4 + (2 << 20)),
        name="post_residual",
    )(x, y, g, gate)


def _mm_kernel(a_ref, w_ref, *rest, has_bias):
    o_ref = rest[-1]
    acc = jnp.dot(a_ref[...], w_ref[...], preferred_element_type=F32)
    if has_bias:
        acc = acc + rest[0][...]
    o_ref[...] = acc.astype(o_ref.dtype)


def matmul(a, w, bias=None, out_dtype=F32, tm=1024, tn=512):
    m, k = a.shape
    n = w.shape[1]
    tm, tn = min(tm, m), min(tn, n)
    in_specs = [pl.BlockSpec((tm, k), lambda i, j: (i, 0)), pl.BlockSpec((k, tn), lambda i, j: (0, j))]
    args = [a, w]
    if bias is not None:
        in_specs.append(pl.BlockSpec((1, tn), lambda i, j: (0, j)))
        args.append(bias)
    osz = jnp.dtype(out_dtype).itemsize
    vmem = 2 * tm * k * 2 + 2 * k * tn * 2 + 2 * tm * tn * osz + 2 * tm * tn * 4 + (2 << 20)
    return pl.pallas_call(
        functools.partial(_mm_kernel, has_bias=bias is not None),
        grid=(m // tm, n // tn),
        in_specs=in_specs,
        out_specs=pl.BlockSpec((tm, tn), lambda i, j: (i, j)),
        out_shape=jax.ShapeDtypeStruct((m, n), out_dtype),
        compiler_params=_params(("parallel", "arbitrary"), vmem),
        name="matmul",
    )(*args)


def _swiglu_up_kernel(a_ref, wg_ref, wu_ref, o_ref):
    a = a_ref[...]
    g = jnp.dot(a, wg_ref[0], preferred_element_type=F32)
    u = jnp.dot(a, wu_ref[0], preferred_element_type=F32)
    o_ref[0] = (g * _sigmoid(g) * u).astype(o_ref.dtype)


def swiglu_up(a, w_gu, tm, tn):
    m, k = a.shape
    e, _, f2 = w_gu.shape
    f = f2 // 2
    nj = f // tn
    vmem = 2 * tm * k * 2 + 4 * k * tn * 2 + 2 * tm * tn * 2 + 4 * tm * tn * 4 + (2 << 20)
    return pl.pallas_call(
        _swiglu_up_kernel,
        grid=(e, m // tm, nj),
        in_specs=[
            pl.BlockSpec((tm, k), lambda x, i, j: (i, 0)),
            pl.BlockSpec((1, k, tn), lambda x, i, j: (x, 0, j)),
            pl.BlockSpec((1, k, tn), lambda x, i, j: (x, 0, j + nj)),
        ],
        out_specs=pl.BlockSpec((1, tm, tn), lambda x, i, j: (x, i, j)),
        out_shape=jax.ShapeDtypeStruct((e, m, f), BF16),
        compiler_params=_params(("parallel", "parallel", "arbitrary"), vmem),
        name="swiglu_up",
    )(a, w_gu, w_gu)


def _down_kernel(a_ref, w_ref, g_ref, o_ref, acc_ref):
    e = pl.program_id(2)

    @pl.when(e == 0)
    def _():
        acc_ref[...] = jnp.zeros_like(acc_ref)

    y = jnp.dot(a_ref[0], w_ref[0], preferred_element_type=F32)
    g = g_ref[...]
    lane = lax.broadcasted_iota(jnp.int32, g.shape, 1)
    ge = jnp.sum(jnp.where(lane == e, g, 0.0), axis=1, keepdims=True)
    acc_ref[...] += ge * y

    @pl.when(e == pl.num_programs(2) - 1)
    def _():
        o_ref[...] = acc_ref[...]


def gated_down(act, w_down, gates, tm, tn):
    e, m, f = act.shape
    n = w_down.shape[2]
    vmem = 2 * tm * f * 2 + 2 * f * tn * 2 + 2 * tm * tn * 4 + 3 * tm * tn * 4 + 2 * tm * LANES * 4 + (2 << 20)
    return pl.pallas_call(
        _down_kernel,
        grid=(m // tm, n // tn, e),
        in_specs=[
            pl.BlockSpec((1, tm, f), lambda i, j, x: (x, i, 0)),
            pl.BlockSpec((1, f, tn), lambda i, j, x: (x, 0, j)),
            pl.BlockSpec((tm, LANES), lambda i, j, x: (i, 0)),
        ],
        out_specs=pl.BlockSpec((tm, tn), lambda i, j, x: (i, j)),
        out_shape=jax.ShapeDtypeStruct((m, n), F32),
        scratch_shapes=[pltpu.VMEM((tm, tn), F32)],
        compiler_params=_params(("parallel", "parallel", "arbitrary"), vmem),
        name="gated_down",
    )(act, w_down, gates)


def _router_kernel(x_ref, g_ref, sh_ref, sc_ref, w_ref, b_ref, o_ref):
    h = _norm_mod(x_ref[...], g_ref[...], sh_ref[...], sc_ref[...])
    logits = jnp.dot(h, w_ref[...], preferred_element_type=F32, precision=lax.Precision.HIGHEST) + b_ref[...]
    lane = lax.broadcasted_iota(jnp.int32, logits.shape, 1)
    neg = jnp.float32(-jnp.inf)
    logits = jnp.where(lane < N_EXPERTS, logits, neg)
    v1 = jnp.max(logits, axis=1, keepdims=True)
    i1 = jnp.min(jnp.where(logits == v1, lane, LANES), axis=1, keepdims=True)
    rest = jnp.where(lane == i1, neg, logits)
    v2 = jnp.max(rest, axis=1, keepdims=True)
    i2 = jnp.min(jnp.where(rest == v2, lane, LANES), axis=1, keepdims=True)
    e2 = jnp.exp(v2 - v1)
    w1 = 1.0 / (1.0 + e2)
    w2 = e2 / (1.0 + e2)
    o_ref[...] = jnp.where(lane == i1, w1, 0.0) + jnp.where(lane == i2, w2, 0.0)


def router_gates(x, g, shift, scale, router_w, router_b, tm=256):
    m, d = x.shape
    wp = jnp.zeros((d, LANES), F32).at[:, :N_EXPERTS].set(router_w)
    bp = jnp.zeros((1, LANES), F32).at[0, :N_EXPERTS].set(router_b)
    row = pl.BlockSpec((1, d), lambda i: (0, 0))
    return pl.pallas_call(
        _router_kernel,
        grid=(m // tm,),
        in_specs=[pl.BlockSpec((tm, d), lambda i: (i, 0)), row, row, row,
                  pl.BlockSpec((d, LANES), lambda i: (0, 0)), pl.BlockSpec((1, LANES), lambda i: (0, 0))],
        out_specs=pl.BlockSpec((tm, LANES), lambda i: (i, 0)),
        out_shape=jax.ShapeDtypeStruct((m, LANES), F32),
        compiler_params=_params(("parallel",), 2 * tm * d * 4 + 6 * tm * d * 4 + 2 * d * LANES * 4 + (4 << 20)),
        name="router",
    )(x, g, shift, scale, wp, bp)


def _rope(x, cs, sn):
    lane = lax.broadcasted_iota(jnp.int32, x.shape, 1)
    first = (lane % (2 * ROPE_PAIRS)) < ROPE_PAIRS
    partner = jnp.where(first, pltpu.roll(x, HEAD_DIM - ROPE_PAIRS, 1), pltpu.roll(x, ROPE_PAIRS, 1))
    return x * cs + partner * sn


def _shift_rows(u, halo_prev, halo_next):
    n = u.shape[0]
    row = lax.broadcasted_iota(jnp.int32, u.shape, 0)
    prev = jnp.where(row == 0, halo_prev, pltpu.roll(u, 1, 0))
    nxt = jnp.where(row == n - 1, halo_next, pltpu.roll(u, n - 1, 0))
    return prev, nxt


def _mixer_kernel(sink_ref, p_ref, kvp_ref, kvn_ref, cp_ref, hp_ref, cn_ref, hn_ref, ckv_ref,
                  cs_ref, sn_ref, csp_ref, snp_ref, csn_ref, snn_ref, cw_ref, o_ref):
    i = pl.program_id(0)
    nb = pl.num_programs(0)
    cs, sn = cs_ref[...], sn_ref[...]

    a_b = p_ref[:, 0:CONV_WIDTH].astype(F32)
    cu = p_ref[:, CONV_WIDTH:2 * CONV_WIDTH].astype(F32) * p_ref[:, 2 * CONV_WIDTH:3 * CONV_WIDTH].astype(F32)
    last = BF16_SUBLANES - 1
    halo_p = cp_ref[last:last + 1, :].astype(F32) * hp_ref[last:last + 1, :].astype(F32)
    halo_n = cn_ref[0:1, :].astype(F32) * hn_ref[0:1, :].astype(F32)
    halo_p = jnp.where(i > 0, halo_p, 0.0)
    halo_n = jnp.where(i < nb - 1, halo_n, 0.0)
    cu_prev, cu_next = _shift_rows(cu, halo_p, halo_n)
    y_conv = a_b * (cu_prev * cw_ref[0:1, :] + cu * cw_ref[1:2, :] + cu_next * cw_ref[2:3, :])
    o_ref[:, 0:CONV_WIDTH] = y_conv.astype(o_ref.dtype)

    rows = GQA_GROUP * BLOCK
    qi = lax.broadcasted_iota(jnp.int32, (rows, 3 * BLOCK), 0) % BLOCK
    kj = lax.broadcasted_iota(jnp.int32, (rows, 3 * BLOCK), 1)
    key_pos = i * BLOCK + kj - BLOCK
    mask = (jnp.abs(qi + BLOCK - kj) <= BLOCK) & (key_pos >= 0) & (key_pos < nb * BLOCK)
    nt = (((1,), (1,)), ((), ()))

    for kh in range(N_KV_HEADS):
        ko = KV_START + kh * HEAD_DIM
        vo = KV_START + KV_WIDTH + kh * HEAD_DIM
        kb = kh * HEAD_DIM
        vb = KV_WIDTH + kh * HEAD_DIM
        k_band = jnp.concatenate([
            _rope(kvp_ref[:, kb:kb + HEAD_DIM].astype(F32), csp_ref[...], snp_ref[...]),
            _rope(p_ref[:, ko:ko + HEAD_DIM].astype(F32), cs, sn),
            _rope(kvn_ref[:, kb:kb + HEAD_DIM].astype(F32), csn_ref[...], snn_ref[...]),
        ], axis=0).astype(BF16)
        v_band = jnp.concatenate([kvp_ref[:, vb:vb + HEAD_DIM], p_ref[:, vo:vo + HEAD_DIM],
                                  kvn_ref[:, vb:vb + HEAD_DIM]], axis=0)
        k_ctx = ckv_ref[:, kb:kb + HEAD_DIM]
        v_ctx = ckv_ref[:, vb:vb + HEAD_DIM]
        qs, sinks = [], []
        for g in range(GQA_GROUP):
            h = kh * GQA_GROUP + g
            qo = Q_START + h * HEAD_DIM
            qs.append(_rope(p_ref[:, qo:qo + HEAD_DIM].astype(F32), cs, sn).astype(BF16))
            sinks.append(jnp.full((BLOCK, 1), sink_ref[h], F32))
        q = jnp.concatenate(qs, axis=0)
        sink = jnp.concatenate(sinks, axis=0)
        s_loc = lax.dot_general(q, k_band, nt, preferred_element_type=F32) * ATTN_SCALE
        s_loc = jnp.where(mask, s_loc, -jnp.inf)
        s_ctx = lax.dot_general(q, k_ctx, nt, preferred_element_type=F32) * ATTN_SCALE
        m = jnp.maximum(sink, jnp.maximum(jnp.max(s_loc, axis=1, keepdims=True),
                                          jnp.max(s_ctx, axis=1, keepdims=True)))
        e_loc = jnp.exp(s_loc - m)
        e_ctx = jnp.exp(s_ctx - m)
        denom = jnp.exp(sink - m) + jnp.sum(e_loc, axis=1, keepdims=True) + jnp.sum(e_ctx, axis=1, keepdims=True)
        o = (jnp.dot(e_loc.astype(BF16), v_band, preferred_element_type=F32)
             + jnp.dot(e_ctx.astype(BF16), v_ctx, preferred_element_type=F32)) / denom
        for g in range(GQA_GROUP):
            h = kh * GQA_GROUP + g
            oo = CONV_WIDTH + h * HEAD_DIM
            o_ref[:, oo:oo + HEAD_DIM] = o[g * BLOCK:(g + 1) * BLOCK, :].astype(o_ref.dtype)


def even_mixer_core(p, ckv, cs, sn, conv_w, sink):
    s = p.shape[0]
    nb = s // BLOCK
    hb = BLOCK // BF16_SUBLANES
    kvc = KV_START // (2 * KV_WIDTH)
    prev = lambda i: jnp.maximum(i - 1, 0)
    nxt = lambda i: jnp.minimum(i + 1, nb - 1)
    tab = lambda f: pl.BlockSpec((BLOCK, HEAD_DIM), lambda i: (f(i), 0))
    same = lambda i: i
    in_specs = [
        pl.BlockSpec(memory_space=pltpu.SMEM),
        pl.BlockSpec((BLOCK, IN_WIDTH), lambda i: (i, 0)),
        pl.BlockSpec((BLOCK, 2 * KV_WIDTH), lambda i: (prev(i), kvc)),
        pl.BlockSpec((BLOCK, 2 * KV_WIDTH), lambda i: (nxt(i), kvc)),
        pl.BlockSpec((BF16_SUBLANES, CONV_WIDTH), lambda i: (jnp.maximum(i * hb - 1, 0), 1)),
        pl.BlockSpec((BF16_SUBLANES, CONV_WIDTH), lambda i: (jnp.maximum(i * hb - 1, 0), 2)),
        pl.BlockSpec((BF16_SUBLANES, CONV_WIDTH), lambda i: (jnp.minimum((i + 1) * hb, nb * hb - 1), 1)),
        pl.BlockSpec((BF16_SUBLANES, CONV_WIDTH), lambda i: (jnp.minimum((i + 1) * hb, nb * hb - 1), 2)),
        pl.BlockSpec(ckv.shape, lambda i: (0, 0)),
        tab(same), tab(same), tab(prev), tab(prev), tab(nxt), tab(nxt),
        pl.BlockSpec(conv_w.shape, lambda i: (0, 0)),
    ]
    return pl.pallas_call(
        _mixer_kernel,
        grid=(nb,),
        in_specs=in_specs,
        out_specs=pl.BlockSpec((BLOCK, CONV_WIDTH + Q_WIDTH), lambda i: (i, 0)),
        out_shape=jax.ShapeDtypeStruct((s, CONV_WIDTH + Q_WIDTH), BF16),
        compiler_params=_params(("parallel",), 32 << 20),
        name="even_mixer_core",
    )(sink, p, p, p, p, p, p, p, ckv, cs, sn, cs, sn, cs, sn, conv_w)


def rope_tables(s):
    t = jnp.arange(s, dtype=jnp.int32)
    row = (t // GRID_W).astype(F32)
    col = (t % GRID_W).astype(F32)
    inv = ROPE_BASE ** (-jnp.arange(ROPE_PAIRS, dtype=F32) / ROPE_PAIRS)
    ang_r = row[:, None] * inv[None, :]
    ang_c = col[:, None] * inv[None, :]
    cs = jnp.concatenate([jnp.cos(ang_r)] * 2 + [jnp.cos(ang_c)] * 2, axis=1)
    sn = jnp.concatenate([-jnp.sin(ang_r), jnp.sin(ang_r), -jnp.sin(ang_c), jnp.sin(ang_c)], axis=1)
    return cs, sn


def _sconv_kernel(u_ref, up_ref, un_ref, w_ref, b_ref, o_ref):
    i = pl.program_id(0)
    u = u_ref[...]
    halo_p = jnp.where(i > 0, up_ref[F32_SUBLANES - 1:F32_SUBLANES, :], 0.0)
    halo_n = jnp.where(i < pl.num_programs(0) - 1, un_ref[0:1, :], 0.0)
    prev, nxt = _shift_rows(u, halo_p, halo_n)
    o_ref[...] = prev * w_ref[0:1, :] + u * w_ref[1:2, :] + nxt * w_ref[2:3, :] + b_ref[...]


def short_conv(u, w, b, tm=256, tn=2048):
    s, c = u.shape
    hb = tm // F32_SUBLANES
    nh = s // F32_SUBLANES
    return pl.pallas_call(
        _sconv_kernel,
        grid=(s // tm, c // tn),
        in_specs=[
            pl.BlockSpec((tm, tn), lambda i, j: (i, j)),
            pl.BlockSpec((F32_SUBLANES, tn), lambda i, j: (jnp.maximum(i * hb - 1, 0), j)),
            pl.BlockSpec((F32_SUBLANES, tn), lambda i, j: (jnp.minimum((i + 1) * hb, nh - 1), j)),
            pl.BlockSpec((3, tn), lambda i, j: (0, j)),
            pl.BlockSpec((1, tn), lambda i, j: (0, j)),
        ],
        out_specs=pl.BlockSpec((tm, tn), lambda i, j: (i, j)),
        out_shape=jax.ShapeDtypeStruct((s, c), F32),
        compiler_params=_params(("parallel", "parallel"), 10 * tm * tn * 4 + (2 << 20)),
        name="short_conv",
    )(u, u, u, w, b)


def _hid_kernel(z_ref, w1_ref, b1_ref, w2_ref, b2_ref, w3_ref, b3_ref, fr_ref, o_ref):
    hp = lax.Precision.HIGHEST
    fr = fr_ref[...]
    h = jnp.sin(fr * (jnp.dot(z_ref[...], w1_ref[...], preferred_element_type=F32, precision=hp) + b1_ref[...]))
    h = jnp.sin(fr * (jnp.dot(h, w2_ref[...], preferred_element_type=F32, precision=hp) + b2_ref[...]))
    o_ref[...] = jnp.sin(fr * (jnp.dot(h, w3_ref[...], preferred_element_type=F32, precision=hp) + b3_ref[...]))


def _pad2(a, r, c):
    return jnp.zeros((r, c), F32).at[:a.shape[0], :a.shape[1]].set(a.astype(F32))


def filter_hidden(length, w1, b1, freq, w2, b2, w3, b3, tl=2048):
    t = jnp.linspace(0.0, 1.0, length, dtype=F32)[:, None]
    w = (2.0 * math.pi / length) * jnp.arange(length, dtype=F32)[:, None]
    bands = jnp.linspace(1e-4, FILTER_BANDS - 1, FILTER_BANDS, dtype=F32)[None]
    z = jnp.concatenate([t, jnp.cos(bands * w), -jnp.sin(bands * w)], axis=-1)
    zp = _pad2(z, length, LANES)
    full = pl.BlockSpec((LANES, LANES), lambda i: (0, 0))
    row = pl.BlockSpec((1, LANES), lambda i: (0, 0))
    return pl.pallas_call(
        _hid_kernel,
        grid=(length // tl,),
        in_specs=[pl.BlockSpec((tl, LANES), lambda i: (i, 0)), full, row, full, row, full, row, row],
        out_specs=pl.BlockSpec((tl, LANES), lambda i: (i, 0)),
        out_shape=jax.ShapeDtypeStruct((length, LANES), F32),
        compiler_params=_params(("parallel",), 16 << 20),
        name="filter_hidden",
    )(zp, _pad2(w1, LANES, LANES), _pad2(b1[None], 1, LANES), _pad2(w2, LANES, LANES), _pad2(b2[None], 1, LANES),
      _pad2(w3, LANES, LANES), _pad2(b3[None], 1, LANES), _pad2(freq[None], 1, LANES))


def _filter_kernel(hid_ref, w_ref, delta_ref, f_ref, n_ref, *, length):
    li = pl.program_id(1)
    tl, dc = f_ref.shape[1], f_ref.shape[2]

    @pl.when(li == 0)
    def _():
        n_ref[...] = jnp.zeros_like(n_ref)

    pos = li * tl + lax.broadcasted_iota(jnp.int32, (tl, dc), 0)
    t = pos.astype(F32) / float(length - 1)
    decay = jnp.exp(-t * delta_ref[...])
    hid = hid_ref[...]
    for q in range(2 * HYENA_ORDER):
        f = jnp.dot(hid, w_ref[q], preferred_element_type=F32, precision=lax.Precision.HIGHEST) * decay
        f_ref[q] = f
        o = q // 2
        n_ref[o:o + 1, :] += jnp.sum(jnp.abs(f), axis=0, keepdims=True)


def hyena_filters(hid, w_o, length, tl=1024, dc=512):
    d = w_o.shape[2]
    delta = jnp.abs(jnp.linspace(MIN_DECAY, MAX_DECAY, d, dtype=F32))[None]
    nq = 2 * HYENA_ORDER
    return pl.pallas_call(
        functools.partial(_filter_kernel, length=length),
        grid=(d // dc, length // tl),
        in_specs=[
            pl.BlockSpec((tl, LANES), lambda c, l: (l, 0)),
            pl.BlockSpec((nq, LANES, dc), lambda c, l: (0, 0, c)),
            pl.BlockSpec((1, dc), lambda c, l: (0, c)),
        ],
        out_specs=[
            pl.BlockSpec((nq, tl, dc), lambda c, l: (0, l, c)),
            pl.BlockSpec((F32_SUBLANES, dc), lambda c, l: (0, c)),
        ],
        out_shape=[jax.ShapeDtypeStruct((nq, length, d), F32), jax.ShapeDtypeStruct((F32_SUBLANES, d), F32)],
        compiler_params=_params(("parallel", "arbitrary"), 2 * nq * tl * dc * 4 + 6 * tl * dc * 4 + (4 << 20)),
        name="hyena_filters",
    )(hid, w_o, delta)


def _hgate_kernel(gate_ref, y_ref, z_ref, n_ref, b_ref, o_ref):
    y = y_ref[...] / (n_ref[...] + 1e-6)
    o_ref[...] = (gate_ref[...] * (y + b_ref[...] * z_ref[...])).astype(o_ref.dtype)


def hyena_gate(u, gate_blk, y, z_arr, z_blk, nsum, bias, out_dtype, tm=512, tn=1024):
    s, d = y.shape
    nj = d // tn
    return pl.pallas_call(
        _hgate_kernel,
        grid=(s // tm, nj),
        in_specs=[
            pl.BlockSpec((tm, tn), lambda i, j: (i, j + gate_blk * nj)),
            pl.BlockSpec((tm, tn), lambda i, j: (i, j)),
            pl.BlockSpec((tm, tn), lambda i, j: (i, j + z_blk * nj)),
            pl.BlockSpec((1, tn), lambda i, j: (0, j)),
            pl.BlockSpec((1, tn), lambda i, j: (0, j)),
        ],
        out_specs=pl.BlockSpec((tm, tn), lambda i, j: (i, j)),
        out_shape=jax.ShapeDtypeStruct((s, d), out_dtype),
        compiler_params=_params(("parallel", "parallel"), 12 * tm * tn * 4 + (2 << 20)),
        name="hyena_gate",
    )(u, y, z_arr, nsum, bias)


def _fft_long_conv(u, h_fwd, h_bwd):
    length = u.shape[0]
    n = 2 * length
    k = jnp.concatenate([h_fwd[:1] + h_bwd[:1], h_fwd[1:], jnp.zeros_like(h_fwd[:1]), h_bwd[:0:-1]], axis=0)
    y = jnp.fft.irfft(jnp.fft.rfft(u, n=n, axis=0) * jnp.fft.rfft(k, n=n, axis=0), n=n, axis=0)
    return y[:length]


def kernel(x, c, ctx, c_ctx, ada_w, ada_b, norm_g, mix_w_in, mix_conv_w, mix_sink, mix_w_out, ffn_w_gu, ffn_w_down, hy_w_in, hy_b_in, hy_conv_w, hy_conv_b, hf_w1, hf_b1, hf_freq, hf_w2, hf_b2, hf_w3, hf_b3, hf_w_out, hf_bias, hy_w_out, hy_b_out, router_w, router_b, moe_w_gu, moe_w_down):
    assert x.shape[0] == 1 and ada_w.shape[0] == 2
    s, d = x.shape[1], x.shape[2]
    xs = x[0]
    ctxs = ctx[0]

    acts = jnp.zeros((F32_SUBLANES, d), F32).at[0].set(c[0]).at[1].set(c_ctx)
    mods = adaln_all(acts, ada_w, ada_b)
    row = lambda layer, r, k: mods[layer, r:r + 1, k * d:(k + 1) * d]

    g = norm_g[0]
    h = norm_mod(xs, g[0:1], row(0, 0, 0), row(0, 0, 1))
    hc = norm_mod(ctxs, g[0:1], row(0, 1, 0), row(0, 1, 1))
    w_in = mix_w_in[0].astype(BF16)
    p = matmul(h, w_in, out_dtype=BF16)
    ckv = matmul(hc, w_in[:, KV_START:], out_dtype=BF16)
    cs, sn = rope_tables(s)
    y = even_mixer_core(p, ckv, cs, sn, mix_conv_w[0], mix_sink[0])
    out = matmul(y, mix_w_out[0].astype(BF16))
    xs = post_residual(xs, out, g[1:2], row(0, 0, 2))

    h = norm_mod(xs, g[2:3], row(0, 0, 3), row(0, 0, 4))
    act = swiglu_up(h, ffn_w_gu[0].astype(BF16)[None], tm=2048, tn=256)
    out = matmul(act[0], ffn_w_down[0].astype(BF16), tm=512, tn=256)
    xs = post_residual(xs, out, g[3:4], row(0, 0, 5))

    g = norm_g[1]
    h = norm_mod(xs, g[0:1], row(1, 0, 0), row(1, 0, 1))
    u0 = matmul(h, hy_w_in[0].astype(BF16), bias=hy_b_in[0][None])
    u = short_conv(u0, hy_conv_w[0], hy_conv_b[0][None])
    hid = filter_hidden(s, hf_w1[0], hf_b1[0], hf_freq[0], hf_w2[0], hf_b2[0], hf_w3[0], hf_b3[0])
    w_o = hf_w_out[0].astype(F32).reshape(FILTER_HIDDEN, HYENA_ORDER * 2, d).transpose(1, 0, 2)
    w_o = jnp.zeros((HYENA_ORDER * 2, LANES, d), F32).at[:, :FILTER_HIDDEN].set(w_o)
    filt, nsum = hyena_filters(hid, w_o, s)
    yc = _fft_long_conv(u[:, 2 * d:], filt[0], filt[1])
    z1 = hyena_gate(u, 0, yc, u, 2, nsum[0:1], hf_bias[0, 0][None], F32)
    yc = _fft_long_conv(z1, filt[2], filt[3])
    z2 = hyena_gate(u, 1, yc, z1, 0, nsum[1:2], hf_bias[0, 1][None], BF16)
    out = matmul(z2, hy_w_out[0].astype(BF16), bias=hy_b_out[0][None])
    xs = post_residual(xs, out, g[1:2], row(1, 0, 2))

    h = norm_mod(xs, g[2:3], row(1, 0, 3), row(1, 0, 4))
    gates = router_gates(xs, g[2:3], row(1, 0, 3), row(1, 0, 4), router_w[0], router_b[0])
    act = swiglu_up(h, moe_w_gu[0].astype(BF16), tm=1024, tn=512)
    out = gated_down(act, moe_w_down[0].astype(BF16), gates, tm=1024, tn=1024)
    xs = post_residual(xs, out, g[3:4], row(1, 0, 5))
    return xs[None]
```

```python
import functools
import math

import jax
import jax.numpy as jnp
from jax import lax
from jax.experimental import pallas as pl
from jax.experimental.pallas import tpu as pltpu

F32 = jnp.float32
BF16 = jnp.bfloat16

D_MODEL = 4096
GRID_W = 64
HEAD_DIM = 128
CONV_WIDTH = D_MODEL // 2
N_HEADS = (D_MODEL // 2) // HEAD_DIM
N_KV_HEADS = N_HEADS // 4
GQA_GROUP = N_HEADS // N_KV_HEADS
Q_WIDTH = N_HEADS * HEAD_DIM
KV_WIDTH = N_KV_HEADS * HEAD_DIM
Q_START = 3 * CONV_WIDTH
KV_START = Q_START + Q_WIDTH
IN_WIDTH = KV_START + 2 * KV_WIDTH
BLOCK = 128
ATTN_SCALE = HEAD_DIM ** -0.5
ROPE_BASE = 10000.0
ROPE_PAIRS = HEAD_DIM // 4
HYENA_ORDER = 2
FILTER_EMB = 33
FILTER_BANDS = (FILTER_EMB - 1) // 2
FILTER_HIDDEN = 64
MIN_DECAY = math.log(1e-2) / 0.3
MAX_DECAY = math.log(1e-2) / 1.5
N_EXPERTS = 8
NORM_EPS = 1e-6

V7X_VMEM_BYTES = 64 * 1024 * 1024
LANES = 128
F32_SUBLANES = 8
BF16_SUBLANES = 16


def _params(semantics, vmem_bytes):
    limit = min(int(vmem_bytes), V7X_VMEM_BYTES - 4 * 1024 * 1024)
    return pltpu.CompilerParams(dimension_semantics=semantics, vmem_limit_bytes=limit)


def _sigmoid(v):
    return 1.0 / (1.0 + jnp.exp(-v))


def _adaln_kernel(a_ref, w_ref, b_ref, o_ref, acc_ref):
    k = pl.program_id(2)

    @pl.when(k == 0)
    def _():
        acc_ref[...] = jnp.zeros_like(acc_ref)

    a = a_ref[...]
    a = a * _sigmoid(a)
    acc_ref[...] += jnp.dot(a.astype(BF16), w_ref[0].astype(BF16), preferred_element_type=F32)

    @pl.when(k == pl.num_programs(2) - 1)
    def _():
        o_ref[0] = acc_ref[...] + b_ref[0]


def adaln_all(acts, ada_w, ada_b):
    depth, d, n = ada_w.shape
    tn, tk = 2048, 1024
    return pl.pallas_call(
        _adaln_kernel,
        grid=(depth, n // tn, d // tk),
        in_specs=[
            pl.BlockSpec((F32_SUBLANES, tk), lambda l, j, k: (0, k)),
            pl.BlockSpec((1, tk, tn), lambda l, j, k: (l, k, j)),
            pl.BlockSpec((1, 1, tn), lambda l, j, k: (l, 0, j)),
        ],
        out_specs=pl.BlockSpec((1, F32_SUBLANES, tn), lambda l, j, k: (l, 0, j)),
        out_shape=jax.ShapeDtypeStruct((depth, F32_SUBLANES, n), F32),
        scratch_shapes=[pltpu.VMEM((F32_SUBLANES, tn), F32)],
        compiler_params=_params(("parallel", "parallel", "arbitrary"), 2 * tk * tn * 4 + tk * tn * 2 + (4 << 20)),
        name="adaln",
    )(acts, ada_w, ada_b.reshape(depth, 1, n))


def _norm_mod(x, g, sh, sc):
    y = x * lax.rsqrt(jnp.mean(x * x, axis=-1, keepdims=True) + NORM_EPS)
    return (y * g) * (1.0 + sc) + sh


def _norm_mod_kernel(x_ref, g_ref, sh_ref, sc_ref, o_ref):
    o_ref[...] = _norm_mod(x_ref[...], g_ref[...], sh_ref[...], sc_ref[...]).astype(o_ref.dtype)


def norm_mod(x, g, shift, scale, tm=512):
    m, d = x.shape
    tm = min(tm, m)
    row = pl.BlockSpec((1, d), lambda i: (0, 0))
    return pl.pallas_call(
        _norm_mod_kernel,
        grid=(m // tm,),
        in_specs=[pl.BlockSpec((tm, d), lambda i: (i, 0)), row, row, row],
        out_specs=pl.BlockSpec((tm, d), lambda i: (i, 0)),
        out_shape=jax.ShapeDtypeStruct((m, d), BF16),
        compiler_params=_params(("parallel",), 2 * tm * d * 4 + 2 * tm * d * 2 + 3 * tm * d * 4 + (2 << 20)),
        name="norm_mod",
    )(x, g, shift, scale)


def _post_kernel(x_ref, y_ref, g_ref, gt_ref, o_ref):
    y = y_ref[...]
    yn = y * lax.rsqrt(jnp.mean(y * y, axis=-1, keepdims=True) + NORM_EPS)
    o_ref[...] = x_ref[...] + gt_ref[...] * (yn * g_ref[...])


def post_residual(x, y, g, gate, tm=512):
    m, d = x.shape
    row = pl.BlockSpec((1, d), lambda i: (0, 0))
    blk = pl.BlockSpec((tm, d), lambda i: (i, 0))
    return pl.pallas_call(
        _post_kernel,
        grid=(m // tm,),
        in_specs=[blk, blk, row, row],
        out_specs=blk,
        out_shape=jax.ShapeDtypeStruct((m, d), F32),
        compiler_params=_params(("parallel",), 6 * tm * d * 4 + 3 * tm * d * 4 + (2 << 20)),
        name="post_residual",
    )(x, y, g, gate)


def _mm_kernel(a_ref, w_ref, *rest, has_bias):
    o_ref = rest[-1]
    acc = jnp.dot(a_ref[...], w_ref[...], preferred_element_type=F32)
    if has_bias:
        acc = acc + rest[0][...]
    o_ref[...] = acc.astype(o_ref.dtype)


def matmul(a, w, bias=None, out_dtype=F32, tm=1024, tn=512):
    m, k = a.shape
    n = w.shape[1]
    tm, tn = min(tm, m), min(tn, n)
    in_specs = [pl.BlockSpec((tm, k), lambda i, j: (i, 0)), pl.BlockSpec((k, tn), lambda i, j: (0, j))]
    args = [a, w]
    if bias is not None:
        in_specs.append(pl.BlockSpec((1, tn), lambda i, j: (0, j)))
        args.append(bias)
    osz = jnp.dtype(out_dtype).itemsize
    vmem = 2 * tm * k * 2 + 2 * k * tn * 2 + 2 * tm * tn * osz + 2 * tm * tn * 4 + (2 << 20)
    return pl.pallas_call(
        functools.partial(_mm_kernel, has_bias=bias is not None),
        grid=(m // tm, n // tn),
        in_specs=in_specs,
        out_specs=pl.BlockSpec((tm, tn), lambda i, j: (i, j)),
        out_shape=jax.ShapeDtypeStruct((m, n), out_dtype),
        compiler_params=_params(("parallel", "arbitrary"), vmem),
        name="matmul",
    )(*args)


def _swiglu_up_kernel(a_ref, wg_ref, wu_ref, o_ref):
    a = a_ref[...]
    g = jnp.dot(a, wg_ref[0], preferred_element_type=F32)
    u = jnp.dot(a, wu_ref[0], preferred_element_type=F32)
    o_ref[0] = (g * _sigmoid(g) * u).astype(o_ref.dtype)


def swiglu_up(a, w_gu, tm, tn):
    m, k = a.shape
    e, _, f2 = w_gu.shape
    f = f2 // 2
    nj = f // tn
    vmem = 2 * tm * k * 2 + 4 * k * tn * 2 + 2 * tm * tn * 2 + 4 * tm * tn * 4 + (2 << 20)
    return pl.pallas_call(
        _swiglu_up_kernel,
        grid=(e, m // tm, nj),
        in_specs=[
            pl.BlockSpec((tm, k), lambda x, i, j: (i, 0)),
            pl.BlockSpec((1, k, tn), lambda x, i, j: (x, 0, j)),
            pl.BlockSpec((1, k, tn), lambda x, i, j: (x, 0, j + nj)),
        ],
        out_specs=pl.BlockSpec((1, tm, tn), lambda x, i, j: (x, i, j)),
        out_shape=jax.ShapeDtypeStruct((e, m, f), BF16),
        compiler_params=_params(("parallel", "parallel", "arbitrary"), vmem),
        name="swiglu_up",
    )(a, w_gu, w_gu)


def _down_kernel(a_ref, w_ref, g_ref, o_ref, acc_ref):
    e = pl.program_id(2)

    @pl.when(e == 0)
    def _():
        acc_ref[...] = jnp.zeros_like(acc_ref)

    y = jnp.dot(a_ref[0], w_ref[0], preferred_element_type=F32)
    g = g_ref[...]
    lane = lax.broadcasted_iota(jnp.int32, g.shape, 1)
    ge = jnp.sum(jnp.where(lane == e, g, 0.0), axis=1, keepdims=True)
    acc_ref[...] += ge * y

    @pl.when(e == pl.num_programs(2) - 1)
    def _():
        o_ref[...] = acc_ref[...]


def gated_down(act, w_down, gates, tm, tn):
    e, m, f = act.shape
    n = w_down.shape[2]
    vmem = 2 * tm * f * 2 + 2 * f * tn * 2 + 2 * tm * tn * 4 + 3 * tm * tn * 4 + 2 * tm * LANES * 4 + (2 << 20)
    return pl.pallas_call(
        _down_kernel,
        grid=(m // tm, n // tn, e),
        in_specs=[
            pl.BlockSpec((1, tm, f), lambda i, j, x: (x, i, 0)),
            pl.BlockSpec((1, f, tn), lambda i, j, x: (x, 0, j)),
            pl.BlockSpec((tm, LANES), lambda i, j, x: (i, 0)),
        ],
        out_specs=pl.BlockSpec((tm, tn), lambda i, j, x: (i, j)),
        out_shape=jax.ShapeDtypeStruct((m, n), F32),
        scratch_shapes=[pltpu.VMEM((tm, tn), F32)],
        compiler_params=_params(("parallel", "parallel", "arbitrary"), vmem),
        name="gated_down",
    )(act, w_down, gates)


def _router_kernel(x_ref, g_ref, sh_ref, sc_ref, w_ref, b_ref, o_ref):
    h = _norm_mod(x_ref[...], g_ref[...], sh_ref[...], sc_ref[...])
    logits = jnp.dot(h, w_ref[...], preferred_element_type=F32, precision=lax.Precision.HIGHEST) + b_ref[...]
    lane = lax.broadcasted_iota(jnp.int32, logits.shape, 1)
    neg = jnp.float32(-jnp.inf)
    logits = jnp.where(lane < N_EXPERTS, logits, neg)
    v1 = jnp.max(logits, axis=1, keepdims=True)
    i1 = jnp.min(jnp.where(logits == v1, lane, LANES), axis=1, keepdims=True)
    rest = jnp.where(lane == i1, neg, logits)
    v2 = jnp.max(rest, axis=1, keepdims=True)
    i2 = jnp.min(jnp.where(rest == v2, lane, LANES), axis=1, keepdims=True)
    e2 = jnp.exp(v2 - v1)
    w1 = 1.0 / (1.0 + e2)
    w2 = e2 / (1.0 + e2)
    o_ref[...] = jnp.where(lane == i1, w1, 0.0) + jnp.where(lane == i2, w2, 0.0)


def router_gates(x, g, shift, scale, router_w, router_b, tm=256):
    m, d = x.shape
    wp = jnp.zeros((d, LANES), F32).at[:, :N_EXPERTS].set(router_w)
    bp = jnp.zeros((1, LANES), F32).at[0, :N_EXPERTS].set(router_b)
    row = pl.BlockSpec((1, d), lambda i: (0, 0))
    return pl.pallas_call(
        _router_kernel,
        grid=(m // tm,),
        in_specs=[pl.BlockSpec((tm, d), lambda i: (i, 0)), row, row, row,
                  pl.BlockSpec((d, LANES), lambda i: (0, 0)), pl.BlockSpec((1, LANES), lambda i: (0, 0))],
        out_specs=pl.BlockSpec((tm, LANES), lambda i: (i, 0)),
        out_shape=jax.ShapeDtypeStruct((m, LANES), F32),
        compiler_params=_params(("parallel",), 2 * tm * d * 4 + 6 * tm * d * 4 + 2 * d * LANES * 4 + (4 << 20)),
        name="router",
    )(x, g, shift, scale, wp, bp)


def _rope(x, cs, sn):
    lane = lax.broadcasted_iota(jnp.int32, x.shape, 1)
    first = (lane % (2 * ROPE_PAIRS)) < ROPE_PAIRS
    partner = jnp.where(first, pltpu.roll(x, HEAD_DIM - ROPE_PAIRS, 1), pltpu.roll(x, ROPE_PAIRS, 1))
    return x * cs + partner * sn


def _shift_rows(u, halo_prev, halo_next):
    n = u.shape[0]
    row = lax.broadcasted_iota(jnp.int32, u.shape, 0)
    prev = jnp.where(row == 0, halo_prev, pltpu.roll(u, 1, 0))
    nxt = jnp.where(row == n - 1, halo_next, pltpu.roll(u, n - 1, 0))
    return prev, nxt


def _mixer_kernel(sink_ref, p_ref, kvp_ref, kvn_ref, cp_ref, hp_ref, cn_ref, hn_ref, ckv_ref,
                  cs_ref, sn_ref, csp_ref, snp_ref, csn_ref, snn_ref, cw_ref, o_ref):
    i = pl.program_id(0)
    nb = pl.num_programs(0)
    cs, sn = cs_ref[...], sn_ref[...]

    a_b = p_ref[:, 0:CONV_WIDTH].astype(F32)
    cu = p_ref[:, CONV_WIDTH:2 * CONV_WIDTH].astype(F32) * p_ref[:, 2 * CONV_WIDTH:3 * CONV_WIDTH].astype(F32)
    last = BF16_SUBLANES - 1
    halo_p = cp_ref[last:last + 1, :].astype(F32) * hp_ref[last:last + 1, :].astype(F32)
    halo_n = cn_ref[0:1, :].astype(F32) * hn_ref[0:1, :].astype(F32)
    halo_p = jnp.where(i > 0, halo_p, 0.0)
    halo_n = jnp.where(i < nb - 1, halo_n, 0.0)
    cu_prev, cu_next = _shift_rows(cu, halo_p, halo_n)
    y_conv = a_b * (cu_prev * cw_ref[0:1, :] + cu * cw_ref[1:2, :] + cu_next * cw_ref[2:3, :])
    o_ref[:, 0:CONV_WIDTH] = y_conv.astype(o_ref.dtype)

    rows = GQA_GROUP * BLOCK
    qi = lax.broadcasted_iota(jnp.int32, (rows, 3 * BLOCK), 0) % BLOCK
    kj = lax.broadcasted_iota(jnp.int32, (rows, 3 * BLOCK), 1)
    key_pos = i * BLOCK + kj - BLOCK
    mask = (jnp.abs(qi + BLOCK - kj) <= BLOCK) & (key_pos >= 0) & (key_pos < nb * BLOCK)
    nt = (((1,), (1,)), ((), ()))

    for kh in range(N_KV_HEADS):
        ko = KV_START + kh * HEAD_DIM
        vo = KV_START + KV_WIDTH + kh * HEAD_DIM
        kb = kh * HEAD_DIM
        vb = KV_WIDTH + kh * HEAD_DIM
        k_band = jnp.concatenate([
            _rope(kvp_ref[:, kb:kb + HEAD_DIM].astype(F32), csp_ref[...], snp_ref[...]),
            _rope(p_ref[:, ko:ko + HEAD_DIM].astype(F32), cs, sn),
            _rope(kvn_ref[:, kb:kb + HEAD_DIM].astype(F32), csn_ref[...], snn_ref[...]),
        ], axis=0).astype(BF16)
        v_band = jnp.concatenate([kvp_ref[:, vb:vb + HEAD_DIM], p_ref[:, vo:vo + HEAD_DIM],
                                  kvn_ref[:, vb:vb + HEAD_DIM]], axis=0)
        k_ctx = ckv_ref[:, kb:kb + HEAD_DIM]
        v_ctx = ckv_ref[:, vb:vb + HEAD_DIM]
        qs, sinks = [], []
        for g in range(GQA_GROUP):
            h = kh * GQA_GROUP + g
            qo = Q_START + h * HEAD_DIM
            qs.append(_rope(p_ref[:, qo:qo + HEAD_DIM].astype(F32), cs, sn).astype(BF16))
            sinks.append(jnp.full((BLOCK, 1), sink_ref[h], F32))
        q = jnp.concatenate(qs, axis=0)
        sink = jnp.concatenate(sinks, axis=0)
        s_loc = lax.dot_general(q, k_band, nt, preferred_element_type=F32) * ATTN_SCALE
        s_loc = jnp.where(mask, s_loc, -jnp.inf)
        s_ctx = lax.dot_general(q, k_ctx, nt, preferred_element_type=F32) * ATTN_SCALE
        m = jnp.maximum(sink, jnp.maximum(jnp.max(s_loc, axis=1, keepdims=True),
                                          jnp.max(s_ctx, axis=1, keepdims=True)))
        e_loc = jnp.exp(s_loc - m)
        e_ctx = jnp.exp(s_ctx - m)
        denom = jnp.exp(sink - m) + jnp.sum(e_loc, axis=1, keepdims=True) + jnp.sum(e_ctx, axis=1, keepdims=True)
        o = (jnp.dot(e_loc.astype(BF16), v_band, preferred_element_type=F32)
             + jnp.dot(e_ctx.astype(BF16), v_ctx, preferred_element_type=F32)) / denom
        for g in range(GQA_GROUP):
            h = kh * GQA_GROUP + g
            oo = CONV_WIDTH + h * HEAD_DIM
            o_ref[:, oo:oo + HEAD_DIM] = o[g * BLOCK:(g + 1) * BLOCK, :].astype(o_ref.dtype)


def even_mixer_core(p, ckv, cs, sn, conv_w, sink):
    s = p.shape[0]
    nb = s // BLOCK
    hb = BLOCK // BF16_SUBLANES
    kvc = KV_START // (2 * KV_WIDTH)
    prev = lambda i: jnp.maximum(i - 1, 0)
    nxt = lambda i: jnp.minimum(i + 1, nb - 1)
    tab = lambda f: pl.BlockSpec((BLOCK, HEAD_DIM), lambda i: (f(i), 0))
    same = lambda i: i
    in_specs = [
        pl.BlockSpec(memory_space=pltpu.SMEM),
        pl.BlockSpec((BLOCK, IN_WIDTH), lambda i: (i, 0)),
        pl.BlockSpec((BLOCK, 2 * KV_WIDTH), lambda i: (prev(i), kvc)),
        pl.BlockSpec((BLOCK, 2 * KV_WIDTH), lambda i: (nxt(i), kvc)),
        pl.BlockSpec((BF16_SUBLANES, CONV_WIDTH), lambda i: (jnp.maximum(i * hb - 1, 0), 1)),
        pl.BlockSpec((BF16_SUBLANES, CONV_WIDTH), lambda i: (jnp.maximum(i * hb - 1, 0), 2)),
        pl.BlockSpec((BF16_SUBLANES, CONV_WIDTH), lambda i: (jnp.minimum((i + 1) * hb, nb * hb - 1), 1)),
        pl.BlockSpec((BF16_SUBLANES, CONV_WIDTH), lambda i: (jnp.minimum((i + 1) * hb, nb * hb - 1), 2)),
        pl.BlockSpec(ckv.shape, lambda i: (0, 0)),
        tab(same), tab(same), tab(prev), tab(prev), tab(nxt), tab(nxt),
        pl.BlockSpec(conv_w.shape, lambda i: (0, 0)),
    ]
    return pl.pallas_call(
        _mixer_kernel,
        grid=(nb,),
        in_specs=in_specs,
        out_specs=pl.BlockSpec((BLOCK, CONV_WIDTH + Q_WIDTH), lambda i: (i, 0)),
        out_shape=jax.ShapeDtypeStruct((s, CONV_WIDTH + Q_WIDTH), BF16),
        compiler_params=_params(("parallel",), 32 << 20),
        name="even_mixer_core",
    )(sink, p, p, p, p, p, p, p, ckv, cs, sn, cs, sn, cs, sn, conv_w)


def rope_tables(s):
    t = jnp.arange(s, dtype=jnp.int32)
    row = (t // GRID_W).astype(F32)
    col = (t % GRID_W).astype(F32)
    inv = ROPE_BASE ** (-jnp.arange(ROPE_PAIRS, dtype=F32) / ROPE_PAIRS)
    ang_r = row[:, None] * inv[None, :]
    ang_c = col[:, None] * inv[None, :]
    cs = jnp.concatenate([jnp.cos(ang_r)] * 2 + [jnp.cos(ang_c)] * 2, axis=1)
    sn = jnp.concatenate([-jnp.sin(ang_r), jnp.sin(ang_r), -jnp.sin(ang_c), jnp.sin(ang_c)], axis=1)
    return cs, sn


def _sconv_kernel(u_ref, up_ref, un_ref, w_ref, b_ref, o_ref):
    i = pl.program_id(0)
    u = u_ref[...]
    halo_p = jnp.where(i > 0, up_ref[F32_SUBLANES - 1:F32_SUBLANES, :], 0.0)
    halo_n = jnp.where(i < pl.num_programs(0) - 1, un_ref[0:1, :], 0.0)
    prev, nxt = _shift_rows(u, halo_p, halo_n)
    o_ref[...] = prev * w_ref[0:1, :] + u * w_ref[1:2, :] + nxt * w_ref[2:3, :] + b_ref[...]


def short_conv(u, w, b, tm=256, tn=2048):
    s, c = u.shape
    hb = tm // F32_SUBLANES
    nh = s // F32_SUBLANES
    return pl.pallas_call(
        _sconv_kernel,
        grid=(s // tm, c // tn),
        in_specs=[
            pl.BlockSpec((tm, tn), lambda i, j: (i, j)),
            pl.BlockSpec((F32_SUBLANES, tn), lambda i, j: (jnp.maximum(i * hb - 1, 0), j)),
            pl.BlockSpec((F32_SUBLANES, tn), lambda i, j: (jnp.minimum((i + 1) * hb, nh - 1), j)),
            pl.BlockSpec((3, tn), lambda i, j: (0, j)),
            pl.BlockSpec((1, tn), lambda i, j: (0, j)),
        ],
        out_specs=pl.BlockSpec((tm, tn), lambda i, j: (i, j)),
        out_shape=jax.ShapeDtypeStruct((s, c), F32),
        compiler_params=_params(("parallel", "parallel"), 10 * tm * tn * 4 + (2 << 20)),
        name="short_conv",
    )(u, u, u, w, b)


def _hid_kernel(z_ref, w1_ref, b1_ref, w2_ref, b2_ref, w3_ref, b3_ref, fr_ref, o_ref):
    hp = lax.Precision.HIGHEST
    fr = fr_ref[...]
    h = jnp.sin(fr * (jnp.dot(z_ref[...], w1_ref[...], preferred_element_type=F32, precision=hp) + b1_ref[...]))
    h = jnp.sin(fr * (jnp.dot(h, w2_ref[...], preferred_element_type=F32, precision=hp) + b2_ref[...]))
    o_ref[...] = jnp.sin(fr * (jnp.dot(h, w3_ref[...], preferred_element_type=F32, precision=hp) + b3_ref[...]))


def _pad2(a, r, c):
    return jnp.zeros((r, c), F32).at[:a.shape[0], :a.shape[1]].set(a.astype(F32))


def filter_hidden(length, w1, b1, freq, w2, b2, w3, b3, tl=2048):
    t = jnp.linspace(0.0, 1.0, length, dtype=F32)[:, None]
    w = (2.0 * math.pi / length) * jnp.arange(length, dtype=F32)[:, None]
    bands = jnp.linspace(1e-4, FILTER_BANDS - 1, FILTER_BANDS, dtype=F32)[None]
    z = jnp.concatenate([t, jnp.cos(bands * w), -jnp.sin(bands * w)], axis=-1)
    zp = _pad2(z, length, LANES)
    full = pl.BlockSpec((LANES, LANES), lambda i: (0, 0))
    row = pl.BlockSpec((1, LANES), lambda i: (0, 0))
    return pl.pallas_call(
        _hid_kernel,
        grid=(length // tl,),
        in_specs=[pl.BlockSpec((tl, LANES), lambda i: (i, 0)), full, row, full, row, full, row, row],
        out_specs=pl.BlockSpec((tl, LANES), lambda i: (i, 0)),
        out_shape=jax.ShapeDtypeStruct((length, LANES), F32),
        compiler_params=_params(("parallel",), 16 << 20),
        name="filter_hidden",
    )(zp, _pad2(w1, LANES, LANES), _pad2(b1[None], 1, LANES), _pad2(w2, LANES, LANES), _pad2(b2[None], 1, LANES),
      _pad2(w3, LANES, LANES), _pad2(b3[None], 1, LANES), _pad2(freq[None], 1, LANES))


def _filter_kernel(hid_ref, w_ref, delta_ref, f_ref, n_ref, *, length):
    li = pl.program_id(1)
    tl, dc = f_ref.shape[1], f_ref.shape[2]

    @pl.when(li == 0)
    def _():
        n_ref[...] = jnp.zeros_like(n_ref)

    pos = li * tl + lax.broadcasted_iota(jnp.int32, (tl, dc), 0)
    t = pos.astype(F32) / float(length - 1)
    decay = jnp.exp(-t * delta_ref[...])
    hid = hid_ref[...]
    for q in range(2 * HYENA_ORDER):
        f = jnp.dot(hid, w_ref[q], preferred_element_type=F32, precision=lax.Precision.HIGHEST) * decay
        f_ref[q] = f
        o = q // 2
        n_ref[o:o + 1, :] += jnp.sum(jnp.abs(f), axis=0, keepdims=True)


def hyena_filters(hid, w_o, length, tl=1024, dc=512):
    d = w_o.shape[2]
    delta = jnp.abs(jnp.linspace(MIN_DECAY, MAX_DECAY, d, dtype=F32))[None]
    nq = 2 * HYENA_ORDER
    return pl.pallas_call(
        functools.partial(_filter_kernel, length=length),
        grid=(d // dc, length // tl),
        in_specs=[
            pl.BlockSpec((tl, LANES), lambda c, l: (l, 0)),
            pl.BlockSpec((nq, LANES, dc), lambda c, l: (0, 0, c)),
            pl.BlockSpec((1, dc), lambda c, l: (0, c)),
        ],
        out_specs=[
            pl.BlockSpec((nq, tl, dc), lambda c, l: (0, l, c)),
            pl.BlockSpec((F32_SUBLANES, dc), lambda c, l: (0, c)),
        ],
        out_shape=[jax.ShapeDtypeStruct((nq, length, d), F32), jax.ShapeDtypeStruct((F32_SUBLANES, d), F32)],
        compiler_params=_params(("parallel", "arbitrary"), 2 * nq * tl * dc * 4 + 6 * tl * dc * 4 + (4 << 20)),
        name="hyena_filters",
    )(hid, w_o, delta)


def _hgate_kernel(gate_ref, y_ref, z_ref, n_ref, b_ref, o_ref):
    y = y_ref[...] / (n_ref[...] + 1e-6)
    o_ref[...] = (gate_ref[...] * (y + b_ref[...] * z_ref[...])).astype(o_ref.dtype)


def hyena_gate(u, gate_blk, y, z_arr, z_blk, nsum, bias, out_dtype, tm=512, tn=1024):
    s, d = y.shape
    nj = d // tn
    return pl.pallas_call(
        _hgate_kernel,
        grid=(s // tm, nj),
        in_specs=[
            pl.BlockSpec((tm, tn), lambda i, j: (i, j + gate_blk * nj)),
            pl.BlockSpec((tm, tn), lambda i, j: (i, j)),
            pl.BlockSpec((tm, tn), lambda i, j: (i, j + z_blk * nj)),
            pl.BlockSpec((1, tn), lambda i, j: (0, j)),
            pl.BlockSpec((1, tn), lambda i, j: (0, j)),
        ],
        out_specs=pl.BlockSpec((tm, tn), lambda i, j: (i, j)),
        out_shape=jax.ShapeDtypeStruct((s, d), out_dtype),
        compiler_params=_params(("parallel", "parallel"), 12 * tm * tn * 4 + (2 << 20)),
        name="hyena_gate",
    )(u, y, z_arr, nsum, bias)


FFT_N1 = 256
FFT_N2 = 128
FFT_K1 = FFT_N1 // 2
FFT_GROUP = BF16_SUBLANES
FFT_PITCH = 3 * F32_SUBLANES


def _phase_tables(length):
    n1h, n2, k1n = FFT_K1, FFT_N2, FFT_K1
    assert length == n1h * n2
    n = 2 * length
    two_pi = 2.0 * math.pi
    ia = jnp.arange(n1h, dtype=jnp.int32)
    pa = (ia[None, :] * (2 * ia[:, None] + 1)) % (2 * FFT_N1)
    tha = pa.astype(F32) * (two_pi / (2 * FFT_N1))
    fa = jnp.concatenate([jnp.cos(tha), -jnp.sin(tha)], axis=0).astype(BF16)
    ca = (jnp.concatenate([jnp.cos(tha).T, -jnp.sin(tha).T], axis=1) * (2.0 / n)).astype(BF16)
    k1 = jnp.arange(k1n, dtype=jnp.int32)[:, None, None]
    k2 = jnp.arange(n2, dtype=jnp.int32)[None, :, None]
    m2 = jnp.arange(n2, dtype=jnp.int32)[None, None, :]
    pb = (m2 * (2 * (k1 + FFT_N1 * k2) + 1)) % (2 * n)
    phb = pb.astype(F32) * (two_pi / (2 * n))
    gr, gi = jnp.cos(phb), -jnp.sin(phb)
    gb = jnp.concatenate([jnp.concatenate([gr, -gi], axis=2), jnp.concatenate([gi, gr], axis=2)], axis=1)
    hr, hi = jnp.swapaxes(gr, 1, 2), -jnp.swapaxes(gi, 1, 2)
    gbi = jnp.concatenate([jnp.concatenate([hr, -hi], axis=2), jnp.concatenate([hi, hr], axis=2)], axis=1)
    return fa, ca, gb.astype(BF16), gbi.astype(BF16)


def _regroup_rows(k):
    return pl.ds(pl.multiple_of(k * FFT_PITCH, F32_SUBLANES), FFT_GROUP)


def _fft_a_kernel(*refs, ns):
    x_refs, fa_ref, o_ref = refs[:ns], refs[ns], refs[ns + 1]
    xs_refs = refs[ns + 2:2 * ns + 2]
    s_refs = refs[2 * ns + 2:]
    fa = fa_ref[...]
    for s in range(ns):
        xs_refs[s][...] = x_refs[s][...].reshape(FFT_K1 * FFT_GROUP, LANES)
    for j in range(FFT_GROUP):
        x = jnp.concatenate([r[pl.ds(j, FFT_K1, stride=FFT_GROUP), :] for r in xs_refs], axis=1)
        res = jnp.dot(fa, x.astype(BF16), preferred_element_type=F32)
        for p in range(2):
            for s in range(ns):
                s_refs[p * ns + s][pl.ds(j, FFT_K1, stride=FFT_PITCH), :] = (
                    res[p * FFT_K1:(p + 1) * FFT_K1, s * LANES:(s + 1) * LANES])

    def emit(k, carry):
        for p in range(2):
            for s in range(ns):
                o_ref[0, k, p, :, s * LANES:(s + 1) * LANES] = s_refs[p * ns + s][_regroup_rows(k), :].astype(o_ref.dtype)
        return carry

    lax.fori_loop(0, FFT_K1, emit, 0, unroll=8)


def fft_pass_a(x, col0, d, dc=256):
    q, length, c = x.shape
    ns = dc // LANES
    x4 = x.reshape(q, FFT_K1, FFT_N2, c)
    slab0 = col0 // LANES
    in_specs = [pl.BlockSpec((None, FFT_K1, FFT_GROUP, LANES),
                             functools.partial(lambda w, ci, g, s: (w, 0, g, slab0 + ci * ns + s), s=s))
                for s in range(ns)]
    fa = _phase_tables(length)[0]
    in_specs.append(pl.BlockSpec(fa.shape, lambda w, ci, g: (0, 0)))
    return pl.pallas_call(
        functools.partial(_fft_a_kernel, ns=ns),
        grid=(q, d // dc, FFT_N2 // FFT_GROUP),
        in_specs=in_specs,
        out_specs=pl.BlockSpec((1, FFT_K1, 2, FFT_GROUP, dc), lambda w, ci, g: (w, 0, 0, g, ci)),
        out_shape=jax.ShapeDtypeStruct((q, FFT_K1, 2, FFT_N2, d), BF16),
        scratch_shapes=([pltpu.VMEM((FFT_K1 * FFT_GROUP, LANES), F32)] * ns
                        + [pltpu.VMEM((FFT_K1 * FFT_PITCH, LANES), F32)] * (2 * ns)),
        compiler_params=_params(("parallel", "parallel", "parallel"), 40 << 20),
        name="fft_pass_a",
    )(*([x4] * ns), fa)


def _fft_bk_kernel(af_ref, ab_ref, gb_ref, o_ref):
    for i in range(af_ref.shape[1]):
        g = gb_ref[i]
        uf = jnp.dot(g, af_ref[0, i], preferred_element_type=F32)
        ub = jnp.dot(g, ab_ref[0, i], preferred_element_type=F32)
        n2 = FFT_N2
        o_ref[0, i, 0:n2, :] = (uf[0:n2] + ub[0:n2]).astype(o_ref.dtype)
        o_ref[0, i, n2:2 * n2, :] = (uf[n2:2 * n2] - ub[n2:2 * n2]).astype(o_ref.dtype)


def fft_filter_spectrum(a, gb, kb=8, dc=512):
    q, k1, r, d = a.shape
    blk = lambda f: pl.BlockSpec((1, kb, r, dc), f)
    return pl.pallas_call(
        _fft_bk_kernel,
        grid=(q // 2, k1 // kb, d // dc),
        in_specs=[blk(lambda o, k, c: (2 * o, k, 0, c)), blk(lambda o, k, c: (2 * o + 1, k, 0, c)),
                  pl.BlockSpec((kb, r, r), lambda o, k, c: (k, 0, 0))],
        out_specs=blk(lambda o, k, c: (o, k, 0, c)),
        out_shape=jax.ShapeDtypeStruct((q // 2, k1, r, d), BF16),
        compiler_params=_params(("parallel", "parallel", "parallel"), 40 << 20),
        name="fft_filter_spectrum",
    )(a, a, gb)


def _fft_b_kernel(a_ref, k_ref, gb_ref, gbi_ref, o_ref):
    n2 = FFT_N2
    for i in range(a_ref.shape[0]):
        u = jnp.dot(gb_ref[i], a_ref[i], preferred_element_type=F32)
        ur, ui = u[0:n2], u[n2:2 * n2]
        kr, ki = k_ref[0, i, 0:n2, :].astype(F32), k_ref[0, i, n2:2 * n2, :].astype(F32)
        v = jnp.concatenate([ur * kr - ui * ki, ur * ki + ui * kr], axis=0).astype(BF16)
        o_ref[i] = jnp.dot(gbi_ref[i], v, preferred_element_type=F32).astype(o_ref.dtype)


def fft_pass_b(a, kspec, order, gb, gbi, kb=8, dc=512):
    k1, r, d = a.shape
    return pl.pallas_call(
        _fft_b_kernel,
        grid=(k1 // kb, d // dc),
        in_specs=[pl.BlockSpec((kb, r, dc), lambda k, c: (k, 0, c)),
                  pl.BlockSpec((1, kb, r, dc), lambda k, c: (order, k, 0, c)),
                  pl.BlockSpec((kb, r, r), lambda k, c: (k, 0, 0)),
                  pl.BlockSpec((kb, r, r), lambda k, c: (k, 0, 0))],
        out_specs=pl.BlockSpec((kb, r, dc), lambda k, c: (k, 0, c)),
        out_shape=jax.ShapeDtypeStruct((k1, r, d), BF16),
        compiler_params=_params(("parallel", "parallel"), 40 << 20),
        name="fft_pass_b",
    )(a, kspec, gb, gbi)


def _fft_c_kernel(*refs, ns):
    b_ref, ca_ref, gate_refs, z_refs = refs[0], refs[1], refs[2:2 + ns], refs[2 + ns:2 + 2 * ns]
    n_ref, bias_ref, o_ref = refs[2 + 2 * ns:5 + 2 * ns]
    s_refs = refs[5 + 2 * ns:5 + 4 * ns]
    t_refs = refs[5 + 4 * ns:]
    ca = ca_ref[...]

    def spread(k, carry):
        for p in range(2):
            for s in range(ns):
                s_refs[p * ns + s][_regroup_rows(k), :] = b_ref[k, p, :, s * LANES:(s + 1) * LANES].astype(F32)
        return carry

    lax.fori_loop(0, FFT_K1, spread, 0, unroll=8)
    for j in range(FFT_GROUP):
        b = jnp.concatenate([
            jnp.concatenate([s_refs[p * ns + s][pl.ds(j, FFT_K1, stride=FFT_PITCH), :] for s in range(ns)], axis=1)
            for p in range(2)], axis=0)
        y = jnp.dot(ca, b.astype(BF16), preferred_element_type=F32)
        for s in range(ns):
            t_refs[s][pl.ds(j, FFT_K1, stride=FFT_PITCH), :] = y[:, s * LANES:(s + 1) * LANES]

    def emit(k, carry):
        for s in range(ns):
            sl = slice(s * LANES, (s + 1) * LANES)
            y = t_refs[s][_regroup_rows(k), :] / (n_ref[:, sl] + 1e-6)
            o_ref[k, :, sl] = (gate_refs[s][k] * (y + bias_ref[:, sl] * z_refs[s][k])).astype(o_ref.dtype)
        return carry

    lax.fori_loop(0, FFT_K1, emit, 0, unroll=8)


def fft_pass_c(bp, ca, gate_arr, gate_col0, z_arr, z_col0, nsum, bias, out_dtype, dc=256):
    k1, r, d = bp.shape
    ns = dc // LANES
    length = FFT_K1 * FFT_N2
    b4 = bp.reshape(k1, 2, FFT_N2, d)
    g3 = gate_arr.reshape(FFT_K1, FFT_N2, gate_arr.shape[1])
    z3 = z_arr.reshape(FFT_K1, FFT_N2, z_arr.shape[1])
    sig = lambda slab0: [pl.BlockSpec((FFT_K1, FFT_GROUP, LANES),
                                      functools.partial(lambda ci, g, s: (0, g, slab0 + ci * ns + s), s=s))
                         for s in range(ns)]
    row = pl.BlockSpec((1, dc), lambda ci, g: (0, ci))
    out = pl.pallas_call(
        functools.partial(_fft_c_kernel, ns=ns),
        grid=(d // dc, FFT_N2 // FFT_GROUP),
        in_specs=[pl.BlockSpec((k1, 2, FFT_GROUP, dc), lambda ci, g: (0, 0, g, ci)),
                  pl.BlockSpec(ca.shape, lambda ci, g: (0, 0)),
                  *sig(gate_col0 // LANES), *sig(z_col0 // LANES), row, row],
        out_specs=pl.BlockSpec((FFT_K1, FFT_GROUP, dc), lambda ci, g: (0, g, ci)),
        out_shape=jax.ShapeDtypeStruct((FFT_K1, FFT_N2, d), out_dtype),
        scratch_shapes=[pltpu.VMEM((FFT_K1 * FFT_PITCH, LANES), F32)] * (3 * ns),
        compiler_params=_params(("parallel", "parallel"), 40 << 20),
        name="fft_pass_c",
    )(b4, ca, *([g3] * ns), *([z3] * ns), nsum, bias)
    return out.reshape(length, d)


def kernel(x, c, ctx, c_ctx, ada_w, ada_b, norm_g, mix_w_in, mix_conv_w, mix_sink, mix_w_out, ffn_w_gu, ffn_w_down, hy_w_in, hy_b_in, hy_conv_w, hy_conv_b, hf_w1, hf_b1, hf_freq, hf_w2, hf_b2, hf_w3, hf_b3, hf_w_out, hf_bias, hy_w_out, hy_b_out, router_w, router_b, moe_w_gu, moe_w_down):
    assert x.shape[0] == 1 and ada_w.shape[0] == 2
    s, d = x.shape[1], x.shape[2]
    xs = x[0]
    ctxs = ctx[0]

    acts = jnp.zeros((F32_SUBLANES, d), F32).at[0].set(c[0]).at[1].set(c_ctx)
    mods = adaln_all(acts, ada_w, ada_b)
    row = lambda layer, r, k: mods[layer, r:r + 1, k * d:(k + 1) * d]

    g = norm_g[0]
    h = norm_mod(xs, g[0:1], row(0, 0, 0), row(0, 0, 1))
    hc = norm_mod(ctxs, g[0:1], row(0, 1, 0), row(0, 1, 1))
    w_in = mix_w_in[0].astype(BF16)
    p = matmul(h, w_in, out_dtype=BF16)
    ckv = matmul(hc, w_in[:, KV_START:], out_dtype=BF16)
    cs, sn = rope_tables(s)
    y = even_mixer_core(p, ckv, cs, sn, mix_conv_w[0], mix_sink[0])
    out = matmul(y, mix_w_out[0].astype(BF16))
    xs = post_residual(xs, out, g[1:2], row(0, 0, 2))

    h = norm_mod(xs, g[2:3], row(0, 0, 3), row(0, 0, 4))
    act = swiglu_up(h, ffn_w_gu[0].astype(BF16)[None], tm=2048, tn=256)
    out = matmul(act[0], ffn_w_down[0].astype(BF16), tm=512, tn=256)
    xs = post_residual(xs, out, g[3:4], row(0, 0, 5))

    g = norm_g[1]
    h = norm_mod(xs, g[0:1], row(1, 0, 0), row(1, 0, 1))
    u0 = matmul(h, hy_w_in[0].astype(BF16), bias=hy_b_in[0][None])
    u = short_conv(u0, hy_conv_w[0], hy_conv_b[0][None])
    hid = filter_hidden(s, hf_w1[0], hf_b1[0], hf_freq[0], hf_w2[0], hf_b2[0], hf_w3[0], hf_b3[0])
    w_o = hf_w_out[0].astype(F32).reshape(FILTER_HIDDEN, HYENA_ORDER * 2, d).transpose(1, 0, 2)
    w_o = jnp.zeros((HYENA_ORDER * 2, LANES, d), F32).at[:, :FILTER_HIDDEN].set(w_o)
    filt, nsum = hyena_filters(hid, w_o, s)
    _, ca, gb, gbi = _phase_tables(s)
    spec = lambda a: a.reshape(a.shape[0], FFT_K1, 2 * FFT_N2, d)
    kspec = fft_filter_spectrum(spec(fft_pass_a(filt, 0, d)), gb)
    a = spec(fft_pass_a(u[None], 2 * d, d))[0]
    z1 = fft_pass_c(fft_pass_b(a, kspec, 0, gb, gbi), ca, u, 0, u, 2 * d, nsum[0:1], hf_bias[0, 0][None], F32)
    a = spec(fft_pass_a(z1[None], 0, d))[0]
    z2 = fft_pass_c(fft_pass_b(a, kspec, 1, gb, gbi), ca, u, d, z1, 0, nsum[1:2], hf_bias[0, 1][None], BF16)
    out = matmul(z2, hy_w_out[0].astype(BF16), bias=hy_b_out[0][None])
    xs = post_residual(xs, out, g[1:2], row(1, 0, 2))

    h = norm_mod(xs, g[2:3], row(1, 0, 3), row(1, 0, 4))
    gates = router_gates(xs, g[2:3], row(1, 0, 3), row(1, 0, 4), router_w[0], router_b[0])
    act = swiglu_up(h, moe_w_gu[0].astype(BF16), tm=1024, tn=512)
    out = gated_down(act, moe_w_down[0].astype(BF16), gates, tm=1024, tn=1024)
    xs = post_residual(xs, out, g[3:4], row(1, 0, 5))
    return xs[None]
```

```python
import functools
import math

import jax
import jax.numpy as jnp
from jax import lax
from jax.experimental import pallas as pl
from jax.experimental.pallas import tpu as pltpu

F32 = jnp.float32
BF16 = jnp.bfloat16

D_MODEL = 4096
GRID_W = 64
HEAD_DIM = 128
CONV_WIDTH = D_MODEL // 2
N_HEADS = (D_MODEL // 2) // HEAD_DIM
N_KV_HEADS = N_HEADS // 4
GQA_GROUP = N_HEADS // N_KV_HEADS
Q_WIDTH = N_HEADS * HEAD_DIM
KV_WIDTH = N_KV_HEADS * HEAD_DIM
Q_START = 3 * CONV_WIDTH
KV_START = Q_START + Q_WIDTH
IN_WIDTH = KV_START + 2 * KV_WIDTH
BLOCK = 128
ATTN_SCALE = HEAD_DIM ** -0.5
ROPE_BASE = 10000.0
ROPE_PAIRS = HEAD_DIM // 4
HYENA_ORDER = 2
FILTER_EMB = 33
FILTER_BANDS = (FILTER_EMB - 1) // 2
FILTER_HIDDEN = 64
MIN_DECAY = math.log(1e-2) / 0.3
MAX_DECAY = math.log(1e-2) / 1.5
N_EXPERTS = 8
NORM_EPS = 1e-6

V7X_VMEM_BYTES = 64 * 1024 * 1024
LANES = 128
F32_SUBLANES = 8
BF16_SUBLANES = 16


def _params(semantics, vmem_bytes):
    limit = min(int(vmem_bytes), V7X_VMEM_BYTES - 4 * 1024 * 1024)
    return pltpu.CompilerParams(dimension_semantics=semantics, vmem_limit_bytes=limit)


def _sigmoid(v):
    return 1.0 / (1.0 + jnp.exp(-v))


def _adaln_kernel(a_ref, w_ref, b_ref, o_ref, acc_ref):
    k = pl.program_id(2)

    @pl.when(k == 0)
    def _():
        acc_ref[...] = jnp.zeros_like(acc_ref)

    a = a_ref[...]
    a = a * _sigmoid(a)
    acc_ref[...] += jnp.dot(a.astype(BF16), w_ref[0].astype(BF16), preferred_element_type=F32)

    @pl.when(k == pl.num_programs(2) - 1)
    def _():
        o_ref[0] = acc_ref[...] + b_ref[0]


def adaln_all(acts, ada_w, ada_b):
    depth, d, n = ada_w.shape
    tn, tk = 2048, 1024
    return pl.pallas_call(
        _adaln_kernel,
        grid=(depth, n // tn, d // tk),
        in_specs=[
            pl.BlockSpec((F32_SUBLANES, tk), lambda l, j, k: (0, k)),
            pl.BlockSpec((1, tk, tn), lambda l, j, k: (l, k, j)),
            pl.BlockSpec((1, 1, tn), lambda l, j, k: (l, 0, j)),
        ],
        out_specs=pl.BlockSpec((1, F32_SUBLANES, tn), lambda l, j, k: (l, 0, j)),
        out_shape=jax.ShapeDtypeStruct((depth, F32_SUBLANES, n), F32),
        scratch_shapes=[pltpu.VMEM((F32_SUBLANES, tn), F32)],
        compiler_params=_params(("parallel", "parallel", "arbitrary"), 2 * tk * tn * 4 + tk * tn * 2 + (4 << 20)),
        name="adaln",
    )(acts, ada_w, ada_b.reshape(depth, 1, n))


def _norm_mod(x, g, sh, sc):
    y = x * lax.rsqrt(jnp.mean(x * x, axis=-1, keepdims=True) + NORM_EPS)
    return (y * g) * (1.0 + sc) + sh


def _norm_mod_kernel(x_ref, g_ref, sh_ref, sc_ref, o_ref):
    o_ref[...] = _norm_mod(x_ref[...], g_ref[...], sh_ref[...], sc_ref[...]).astype(o_ref.dtype)


def norm_mod(x, g, shift, scale, tm=512):
    m, d = x.shape
    tm = min(tm, m)
    row = pl.BlockSpec((1, d), lambda i: (0, 0))
    return pl.pallas_call(
        _norm_mod_kernel,
        grid=(m // tm,),
        in_specs=[pl.BlockSpec((tm, d), lambda i: (i, 0)), row, row, row],
        out_specs=pl.BlockSpec((tm, d), lambda i: (i, 0)),
        out_shape=jax.ShapeDtypeStruct((m, d), BF16),
        compiler_params=_params(("parallel",), 2 * tm * d * 4 + 2 * tm * d * 2 + 3 * tm * d * 4 + (2 << 20)),
        name="norm_mod",
    )(x, g, shift, scale)


def _post_kernel(x_ref, y_ref, g_ref, gt_ref, o_ref):
    y = y_ref[...]
    yn = y * lax.rsqrt(jnp.mean(y * y, axis=-1, keepdims=True) + NORM_EPS)
    o_ref[...] = x_ref[...] + gt_ref[...] * (yn * g_ref[...])


def post_residual(x, y, g, gate, tm=512):
    m, d = x.shape
    row = pl.BlockSpec((1, d), lambda i: (0, 0))
    blk = pl.BlockSpec((tm, d), lambda i: (i, 0))
    return pl.pallas_call(
        _post_kernel,
        grid=(m // tm,),
        in_specs=[blk, blk, row, row],
        out_specs=blk,
        out_shape=jax.ShapeDtypeStruct((m, d), F32),
        compiler_params=_params(("parallel",), 6 * tm * d * 4 + 3 * tm * d * 4 + (2 << 20)),
        name="post_residual",
    )(x, y, g, gate)


def _mm_kernel(a_ref, w_ref, *rest, has_bias):
    o_ref = rest[-1]
    acc = jnp.dot(a_ref[...], w_ref[...], preferred_element_type=F32)
    if has_bias:
        acc = acc + rest[0][...]
    o_ref[...] = acc.astype(o_ref.dtype)


def matmul(a, w, bias=None, out_dtype=F32, tm=1024, tn=512):
    m, k = a.shape
    n = w.shape[1]
    tm, tn = min(tm, m), min(tn, n)
    in_specs = [pl.BlockSpec((tm, k), lambda i, j: (i, 0)), pl.BlockSpec((k, tn), lambda i, j: (0, j))]
    args = [a, w]
    if bias is not None:
        in_specs.append(pl.BlockSpec((1, tn), lambda i, j: (0, j)))
        args.append(bias)
    osz = jnp.dtype(out_dtype).itemsize
    vmem = 2 * tm * k * 2 + 2 * k * tn * 2 + 2 * tm * tn * osz + 2 * tm * tn * 4 + (2 << 20)
    return pl.pallas_call(
        functools.partial(_mm_kernel, has_bias=bias is not None),
        grid=(m // tm, n // tn),
        in_specs=in_specs,
        out_specs=pl.BlockSpec((tm, tn), lambda i, j: (i, j)),
        out_shape=jax.ShapeDtypeStruct((m, n), out_dtype),
        compiler_params=_params(("parallel", "arbitrary"), vmem),
        name="matmul",
    )(*args)


def _swiglu_up_kernel(a_ref, wg_ref, wu_ref, o_ref):
    a = a_ref[...]
    g = jnp.dot(a, wg_ref[0], preferred_element_type=F32)
    u = jnp.dot(a, wu_ref[0], preferred_element_type=F32)
    o_ref[0] = (g * _sigmoid(g) * u).astype(o_ref.dtype)


def swiglu_up(a, w_gu, tm, tn):
    m, k = a.shape
    e, _, f2 = w_gu.shape
    f = f2 // 2
    nj = f // tn
    vmem = 2 * tm * k * 2 + 4 * k * tn * 2 + 2 * tm * tn * 2 + 4 * tm * tn * 4 + (2 << 20)
    return pl.pallas_call(
        _swiglu_up_kernel,
        grid=(e, m // tm, nj),
        in_specs=[
            pl.BlockSpec((tm, k), lambda x, i, j: (i, 0)),
            pl.BlockSpec((1, k, tn), lambda x, i, j: (x, 0, j)),
            pl.BlockSpec((1, k, tn), lambda x, i, j: (x, 0, j + nj)),
        ],
        out_specs=pl.BlockSpec((1, tm, tn), lambda x, i, j: (x, i, j)),
        out_shape=jax.ShapeDtypeStruct((e, m, f), BF16),
        compiler_params=_params(("parallel", "parallel", "arbitrary"), vmem),
        name="swiglu_up",
    )(a, w_gu, w_gu)


def _router_kernel(x_ref, g_ref, sh_ref, sc_ref, w_ref, b_ref, sel_ref, wts_ref):
    h = _norm_mod(x_ref[...], g_ref[...], sh_ref[...], sc_ref[...])
    logits = jnp.dot(h, w_ref[...], preferred_element_type=F32, precision=lax.Precision.HIGHEST) + b_ref[...]
    lane = lax.broadcasted_iota(jnp.int32, logits.shape, 1)
    neg = jnp.float32(-jnp.inf)
    logits = jnp.where(lane < N_EXPERTS, logits, neg)
    v1 = jnp.max(logits, axis=1, keepdims=True)
    i1 = jnp.min(jnp.where(logits == v1, lane, LANES), axis=1, keepdims=True)
    rest = jnp.where(lane == i1, neg, logits)
    v2 = jnp.max(rest, axis=1, keepdims=True)
    i2 = jnp.min(jnp.where(rest == v2, lane, LANES), axis=1, keepdims=True)
    e2 = jnp.exp(v2 - v1)
    w1 = 1.0 / (1.0 + e2)
    w2 = e2 / (1.0 + e2)
    sel_ref[...] = jnp.where(lane == 0, i1, jnp.where(lane == 1, i2, 0))
    wts_ref[...] = jnp.where(lane == 0, w1, jnp.where(lane == 1, w2, 0.0))


def router_top2(x, g, shift, scale, router_w, router_b, tm=256):
    m, d = x.shape
    wp = jnp.zeros((d, LANES), F32).at[:, :N_EXPERTS].set(router_w)
    bp = jnp.zeros((1, LANES), F32).at[0, :N_EXPERTS].set(router_b)
    row = pl.BlockSpec((1, d), lambda i: (0, 0))
    out = pl.BlockSpec((tm, LANES), lambda i: (i, 0))
    return pl.pallas_call(
        _router_kernel,
        grid=(m // tm,),
        in_specs=[pl.BlockSpec((tm, d), lambda i: (i, 0)), row, row, row,
                  pl.BlockSpec((d, LANES), lambda i: (0, 0)), pl.BlockSpec((1, LANES), lambda i: (0, 0))],
        out_specs=[out, out],
        out_shape=[jax.ShapeDtypeStruct((m, LANES), jnp.int32), jax.ShapeDtypeStruct((m, LANES), F32)],
        compiler_params=_params(("parallel",), 2 * tm * d * 4 + 6 * tm * d * 4 + 2 * d * LANES * 4 + (4 << 20)),
        name="router",
    )(x, g, shift, scale, wp, bp)


MOE_TILE = 512


def moe_plan(sel):
    t = sel.shape[0]
    npairs = 2 * t
    rows = npairs + N_EXPERTS * MOE_TILE
    e_flat = sel[:, :2].reshape(npairs)
    order = jnp.argsort(e_flat, stable=True).astype(jnp.int32)
    e_sorted = e_flat[order]
    counts = jnp.sum((e_flat[:, None] == jnp.arange(N_EXPERTS, dtype=jnp.int32)[None, :]).astype(jnp.int32), axis=0)
    padded = (counts + MOE_TILE - 1) // MOE_TILE * MOE_TILE
    ends_p = jnp.cumsum(padded)
    starts_p = ends_p - padded
    starts_u = jnp.cumsum(counts) - counts
    dest_sorted = starts_p[e_sorted] + jnp.arange(npairs, dtype=jnp.int32) - starts_u[e_sorted]
    pos = jnp.zeros((npairs,), jnp.int32).at[order].set(dest_sorted)
    src_tok = jnp.zeros((rows,), jnp.int32).at[dest_sorted].set(order // 2)
    tile_start = jnp.arange(rows // MOE_TILE, dtype=jnp.int32) * MOE_TILE
    tile_expert = jnp.minimum(jnp.searchsorted(ends_p, tile_start, side="right"), N_EXPERTS - 1).astype(jnp.int32)
    n_valid = (ends_p[-1] // MOE_TILE).astype(jnp.int32).reshape(1)
    return pos, src_tok, tile_expert, n_valid


def _gather_rows(idx_ref, base, stride, src_hbm, dst_ref, sem, n):
    def row_copy(r):
        return pltpu.make_async_copy(src_hbm.at[pl.ds(idx_ref[base + stride * r], 1)], dst_ref.at[pl.ds(r, 1)], sem)

    def start(r, carry):
        row_copy(r).start()
        return carry

    def wait(r, carry):
        row_copy(r).wait()
        return carry

    lax.fori_loop(0, n, start, 0, unroll=8)
    lax.fori_loop(0, n, wait, 0, unroll=8)


def _moe_gather_kernel(src_ref, x_hbm, g_ref, sh_ref, sc_ref, o_ref, buf_ref, sem):
    tg = buf_ref.shape[0]
    _gather_rows(src_ref, pl.program_id(0) * tg, 1, x_hbm, buf_ref, sem, tg)
    o_ref[...] = _norm_mod(buf_ref[...], g_ref[...], sh_ref[...], sc_ref[...]).astype(o_ref.dtype)


def moe_gather_norm(x, src_tok, g, shift, scale, tg=256):
    d = x.shape[1]
    rows = src_tok.shape[0]
    row = pl.BlockSpec((1, d), lambda i, src: (0, 0))
    return pl.pallas_call(
        _moe_gather_kernel,
        grid_spec=pltpu.PrefetchScalarGridSpec(
            num_scalar_prefetch=1,
            grid=(rows // tg,),
            in_specs=[pl.BlockSpec(memory_space=pl.ANY), row, row, row],
            out_specs=pl.BlockSpec((tg, d), lambda i, src: (i, 0)),
            scratch_shapes=[pltpu.VMEM((tg, d), F32), pltpu.SemaphoreType.DMA(())],
        ),
        out_shape=jax.ShapeDtypeStruct((rows, d), BF16),
        compiler_params=_params(("arbitrary",), 6 * tg * d * 4 + (2 << 20)),
        name="moe_gather_norm",
    )(src_tok, x, g, shift, scale)


def _gswiglu_kernel(te_ref, nv_ref, a_ref, wg_ref, wu_ref, o_ref):
    valid = pl.program_id(0) < nv_ref[0]

    @pl.when(valid)
    def _():
        a = a_ref[...]
        g = jnp.dot(a, wg_ref[0], preferred_element_type=F32)
        u = jnp.dot(a, wu_ref[0], preferred_element_type=F32)
        o_ref[...] = (g * _sigmoid(g) * u).astype(o_ref.dtype)

    @pl.when(jnp.logical_not(valid))
    def _():
        o_ref[...] = jnp.zeros_like(o_ref)


def grouped_swiglu_up(a, w_gu, tile_expert, n_valid, tn=512):
    rows, k = a.shape
    f = w_gu.shape[2] // 2
    nj = f // tn
    tm = MOE_TILE
    vmem = 2 * tm * k * 2 + 4 * k * tn * 2 + 2 * tm * tn * 2 + 4 * tm * tn * 4 + (2 << 20)
    return pl.pallas_call(
        _gswiglu_kernel,
        grid_spec=pltpu.PrefetchScalarGridSpec(
            num_scalar_prefetch=2,
            grid=(rows // tm, nj),
            in_specs=[
                pl.BlockSpec((tm, k), lambda i, j, te, nv: (i, 0)),
                pl.BlockSpec((1, k, tn), lambda i, j, te, nv: (te[i], 0, j)),
                pl.BlockSpec((1, k, tn), lambda i, j, te, nv: (te[i], 0, j + nj)),
            ],
            out_specs=pl.BlockSpec((tm, tn), lambda i, j, te, nv: (i, j)),
        ),
        out_shape=jax.ShapeDtypeStruct((rows, f), BF16),
        compiler_params=_params(("parallel", "arbitrary"), vmem),
        name="grouped_swiglu_up",
    )(tile_expert, n_valid, a, w_gu, w_gu)


def _gdown_kernel(te_ref, nv_ref, a_ref, w_ref, o_ref):
    valid = pl.program_id(0) < nv_ref[0]

    @pl.when(valid)
    def _():
        o_ref[...] = jnp.dot(a_ref[...], w_ref[0], preferred_element_type=F32)

    @pl.when(jnp.logical_not(valid))
    def _():
        o_ref[...] = jnp.zeros_like(o_ref)


def grouped_down(act, w_down, tile_expert, n_valid, tn=1024):
    rows, f = act.shape
    n = w_down.shape[2]
    tm = MOE_TILE
    vmem = 2 * tm * f * 2 + 2 * f * tn * 2 + 2 * tm * tn * 4 + 2 * tm * tn * 4 + (2 << 20)
    return pl.pallas_call(
        _gdown_kernel,
        grid_spec=pltpu.PrefetchScalarGridSpec(
            num_scalar_prefetch=2,
            grid=(rows // tm, n // tn),
            in_specs=[
                pl.BlockSpec((tm, f), lambda i, j, te, nv: (i, 0)),
                pl.BlockSpec((1, f, tn), lambda i, j, te, nv: (te[i], 0, j)),
            ],
            out_specs=pl.BlockSpec((tm, tn), lambda i, j, te, nv: (i, j)),
        ),
        out_shape=jax.ShapeDtypeStruct((rows, n), F32),
        compiler_params=_params(("parallel", "arbitrary"), vmem),
        name="grouped_down",
    )(tile_expert, n_valid, act, w_down)


def _moe_combine_kernel(pos_ref, x_ref, w_ref, ys_hbm, g_ref, gt_ref, o_ref, buf_ref, sem):
    tm = x_ref.shape[0]
    base = 2 * pl.program_id(0) * tm
    _gather_rows(pos_ref, base, 2, ys_hbm, buf_ref.at[0], sem, tm)
    _gather_rows(pos_ref, base + 1, 2, ys_hbm, buf_ref.at[1], sem, tm)
    w = w_ref[...]
    y = w[:, 0:1] * buf_ref[0] + w[:, 1:2] * buf_ref[1]
    yn = y * lax.rsqrt(jnp.mean(y * y, axis=-1, keepdims=True) + NORM_EPS)
    o_ref[...] = x_ref[...] + gt_ref[...] * (yn * g_ref[...])


def moe_combine_residual(x, wts, ys, pos, g, gate, tm=128):
    t, d = x.shape
    row = pl.BlockSpec((1, d), lambda i, p: (0, 0))
    return pl.pallas_call(
        _moe_combine_kernel,
        grid_spec=pltpu.PrefetchScalarGridSpec(
            num_scalar_prefetch=1,
            grid=(t // tm,),
            in_specs=[pl.BlockSpec((tm, d), lambda i, p: (i, 0)), pl.BlockSpec((tm, LANES), lambda i, p: (i, 0)),
                      pl.BlockSpec(memory_space=pl.ANY), row, row],
            out_specs=pl.BlockSpec((tm, d), lambda i, p: (i, 0)),
            scratch_shapes=[pltpu.VMEM((2, tm, d), F32), pltpu.SemaphoreType.DMA(())],
        ),
        out_shape=jax.ShapeDtypeStruct((t, d), F32),
        compiler_params=_params(("arbitrary",), 10 * tm * d * 4 + (2 << 20)),
        name="moe_combine_residual",
    )(pos, x, wts, ys, g, gate)


def _rope(x, cs, sn):
    lane = lax.broadcasted_iota(jnp.int32, x.shape, 1)
    first = (lane % (2 * ROPE_PAIRS)) < ROPE_PAIRS
    partner = jnp.where(first, pltpu.roll(x, HEAD_DIM - ROPE_PAIRS, 1), pltpu.roll(x, ROPE_PAIRS, 1))
    return x * cs + partner * sn


def _shift_rows(u, halo_prev, halo_next):
    n = u.shape[0]
    row = lax.broadcasted_iota(jnp.int32, u.shape, 0)
    prev = jnp.where(row == 0, halo_prev, pltpu.roll(u, 1, 0))
    nxt = jnp.where(row == n - 1, halo_next, pltpu.roll(u, n - 1, 0))
    return prev, nxt


def _mixer_kernel(sink_ref, p_ref, kvp_ref, kvn_ref, cp_ref, hp_ref, cn_ref, hn_ref, ckv_ref,
                  cs_ref, sn_ref, csp_ref, snp_ref, csn_ref, snn_ref, cw_ref, o_ref):
    i = pl.program_id(0)
    nb = pl.num_programs(0)
    cs, sn = cs_ref[...], sn_ref[...]

    a_b = p_ref[:, 0:CONV_WIDTH].astype(F32)
    cu = p_ref[:, CONV_WIDTH:2 * CONV_WIDTH].astype(F32) * p_ref[:, 2 * CONV_WIDTH:3 * CONV_WIDTH].astype(F32)
    last = BF16_SUBLANES - 1
    halo_p = cp_ref[last:last + 1, :].astype(F32) * hp_ref[last:last + 1, :].astype(F32)
    halo_n = cn_ref[0:1, :].astype(F32) * hn_ref[0:1, :].astype(F32)
    halo_p = jnp.where(i > 0, halo_p, 0.0)
    halo_n = jnp.where(i < nb - 1, halo_n, 0.0)
    cu_prev, cu_next = _shift_rows(cu, halo_p, halo_n)
    y_conv = a_b * (cu_prev * cw_ref[0:1, :] + cu * cw_ref[1:2, :] + cu_next * cw_ref[2:3, :])
    o_ref[:, 0:CONV_WIDTH] = y_conv.astype(o_ref.dtype)

    rows = GQA_GROUP * BLOCK
    qi = lax.broadcasted_iota(jnp.int32, (rows, 3 * BLOCK), 0) % BLOCK
    kj = lax.broadcasted_iota(jnp.int32, (rows, 3 * BLOCK), 1)
    key_pos = i * BLOCK + kj - BLOCK
    mask = (jnp.abs(qi + BLOCK - kj) <= BLOCK) & (key_pos >= 0) & (key_pos < nb * BLOCK)
    nt = (((1,), (1,)), ((), ()))

    for kh in range(N_KV_HEADS):
        ko = KV_START + kh * HEAD_DIM
        vo = KV_START + KV_WIDTH + kh * HEAD_DIM
        kb = kh * HEAD_DIM
        vb = KV_WIDTH + kh * HEAD_DIM
        k_band = jnp.concatenate([
            _rope(kvp_ref[:, kb:kb + HEAD_DIM].astype(F32), csp_ref[...], snp_ref[...]),
            _rope(p_ref[:, ko:ko + HEAD_DIM].astype(F32), cs, sn),
            _rope(kvn_ref[:, kb:kb + HEAD_DIM].astype(F32), csn_ref[...], snn_ref[...]),
        ], axis=0).astype(BF16)
        v_band = jnp.concatenate([kvp_ref[:, vb:vb + HEAD_DIM], p_ref[:, vo:vo + HEAD_DIM],
                                  kvn_ref[:, vb:vb + HEAD_DIM]], axis=0)
        k_ctx = ckv_ref[:, kb:kb + HEAD_DIM]
        v_ctx = ckv_ref[:, vb:vb + HEAD_DIM]
        qs, sinks = [], []
        for g in range(GQA_GROUP):
            h = kh * GQA_GROUP + g
            qo = Q_START + h * HEAD_DIM
            qs.append(_rope(p_ref[:, qo:qo + HEAD_DIM].astype(F32), cs, sn).astype(BF16))
            sinks.append(jnp.full((BLOCK, 1), sink_ref[h], F32))
        q = jnp.concatenate(qs, axis=0)
        sink = jnp.concatenate(sinks, axis=0)
        s_loc = lax.dot_general(q, k_band, nt, preferred_element_type=F32) * ATTN_SCALE
        s_loc = jnp.where(mask, s_loc, -jnp.inf)
        s_ctx = lax.dot_general(q, k_ctx, nt, preferred_element_type=F32) * ATTN_SCALE
        m = jnp.maximum(sink, jnp.maximum(jnp.max(s_loc, axis=1, keepdims=True),
                                          jnp.max(s_ctx, axis=1, keepdims=True)))
        e_loc = jnp.exp(s_loc - m)
        e_ctx = jnp.exp(s_ctx - m)
        denom = jnp.exp(sink - m) + jnp.sum(e_loc, axis=1, keepdims=True) + jnp.sum(e_ctx, axis=1, keepdims=True)
        o = (jnp.dot(e_loc.astype(BF16), v_band, preferred_element_type=F32)
             + jnp.dot(e_ctx.astype(BF16), v_ctx, preferred_element_type=F32)) / denom
        for g in range(GQA_GROUP):
            h = kh * GQA_GROUP + g
            oo = CONV_WIDTH + h * HEAD_DIM
            o_ref[:, oo:oo + HEAD_DIM] = o[g * BLOCK:(g + 1) * BLOCK, :].astype(o_ref.dtype)


def even_mixer_core(p, ckv, cs, sn, conv_w, sink):
    s = p.shape[0]
    nb = s // BLOCK
    hb = BLOCK // BF16_SUBLANES
    kvc = KV_START // (2 * KV_WIDTH)
    prev = lambda i: jnp.maximum(i - 1, 0)
    nxt = lambda i: jnp.minimum(i + 1, nb - 1)
    tab = lambda f: pl.BlockSpec((BLOCK, HEAD_DIM), lambda i: (f(i), 0))
    same = lambda i: i
    in_specs = [
        pl.BlockSpec(memory_space=pltpu.SMEM),
        pl.BlockSpec((BLOCK, IN_WIDTH), lambda i: (i, 0)),
        pl.BlockSpec((BLOCK, 2 * KV_WIDTH), lambda i: (prev(i), kvc)),
        pl.BlockSpec((BLOCK, 2 * KV_WIDTH), lambda i: (nxt(i), kvc)),
        pl.BlockSpec((BF16_SUBLANES, CONV_WIDTH), lambda i: (jnp.maximum(i * hb - 1, 0), 1)),
        pl.BlockSpec((BF16_SUBLANES, CONV_WIDTH), lambda i: (jnp.maximum(i * hb - 1, 0), 2)),
        pl.BlockSpec((BF16_SUBLANES, CONV_WIDTH), lambda i: (jnp.minimum((i + 1) * hb, nb * hb - 1), 1)),
        pl.BlockSpec((BF16_SUBLANES, CONV_WIDTH), lambda i: (jnp.minimum((i + 1) * hb, nb * hb - 1), 2)),
        pl.BlockSpec(ckv.shape, lambda i: (0, 0)),
        tab(same), tab(same), tab(prev), tab(prev), tab(nxt), tab(nxt),
        pl.BlockSpec(conv_w.shape, lambda i: (0, 0)),
    ]
    return pl.pallas_call(
        _mixer_kernel,
        grid=(nb,),
        in_specs=in_specs,
        out_specs=pl.BlockSpec((BLOCK, CONV_WIDTH + Q_WIDTH), lambda i: (i, 0)),
        out_shape=jax.ShapeDtypeStruct((s, CONV_WIDTH + Q_WIDTH), BF16),
        compiler_params=_params(("parallel",), 32 << 20),
        name="even_mixer_core",
    )(sink, p, p, p, p, p, p, p, ckv, cs, sn, cs, sn, cs, sn, conv_w)


def rope_tables(s):
    t = jnp.arange(s, dtype=jnp.int32)
    row = (t // GRID_W).astype(F32)
    col = (t % GRID_W).astype(F32)
    inv = ROPE_BASE ** (-jnp.arange(ROPE_PAIRS, dtype=F32) / ROPE_PAIRS)
    ang_r = row[:, None] * inv[None, :]
    ang_c = col[:, None] * inv[None, :]
    cs = jnp.concatenate([jnp.cos(ang_r)] * 2 + [jnp.cos(ang_c)] * 2, axis=1)
    sn = jnp.concatenate([-jnp.sin(ang_r), jnp.sin(ang_r), -jnp.sin(ang_c), jnp.sin(ang_c)], axis=1)
    return cs, sn


def _sconv_kernel(u_ref, up_ref, un_ref, w_ref, b_ref, o_ref):
    i = pl.program_id(0)
    u = u_ref[...]
    halo_p = jnp.where(i > 0, up_ref[F32_SUBLANES - 1:F32_SUBLANES, :], 0.0)
    halo_n = jnp.where(i < pl.num_programs(0) - 1, un_ref[0:1, :], 0.0)
    prev, nxt = _shift_rows(u, halo_p, halo_n)
    o_ref[...] = prev * w_ref[0:1, :] + u * w_ref[1:2, :] + nxt * w_ref[2:3, :] + b_ref[...]


def short_conv(u, w, b, tm=256, tn=2048):
    s, c = u.shape
    hb = tm // F32_SUBLANES
    nh = s // F32_SUBLANES
    return pl.pallas_call(
        _sconv_kernel,
        grid=(s // tm, c // tn),
        in_specs=[
            pl.BlockSpec((tm, tn), lambda i, j: (i, j)),
            pl.BlockSpec((F32_SUBLANES, tn), lambda i, j: (jnp.maximum(i * hb - 1, 0), j)),
            pl.BlockSpec((F32_SUBLANES, tn), lambda i, j: (jnp.minimum((i + 1) * hb, nh - 1), j)),
            pl.BlockSpec((3, tn), lambda i, j: (0, j)),
            pl.BlockSpec((1, tn), lambda i, j: (0, j)),
        ],
        out_specs=pl.BlockSpec((tm, tn), lambda i, j: (i, j)),
        out_shape=jax.ShapeDtypeStruct((s, c), F32),
        compiler_params=_params(("parallel", "parallel"), 10 * tm * tn * 4 + (2 << 20)),
        name="short_conv",
    )(u, u, u, w, b)


def _hid_kernel(z_ref, w1_ref, b1_ref, w2_ref, b2_ref, w3_ref, b3_ref, fr_ref, o_ref):
    hp = lax.Precision.HIGHEST
    fr = fr_ref[...]
    h = jnp.sin(fr * (jnp.dot(z_ref[...], w1_ref[...], preferred_element_type=F32, precision=hp) + b1_ref[...]))
    h = jnp.sin(fr * (jnp.dot(h, w2_ref[...], preferred_element_type=F32, precision=hp) + b2_ref[...]))
    o_ref[...] = jnp.sin(fr * (jnp.dot(h, w3_ref[...], preferred_element_type=F32, precision=hp) + b3_ref[...]))


def _pad2(a, r, c):
    return jnp.zeros((r, c), F32).at[:a.shape[0], :a.shape[1]].set(a.astype(F32))


def filter_hidden(length, w1, b1, freq, w2, b2, w3, b3, tl=2048):
    t = jnp.linspace(0.0, 1.0, length, dtype=F32)[:, None]
    w = (2.0 * math.pi / length) * jnp.arange(length, dtype=F32)[:, None]
    bands = jnp.linspace(1e-4, FILTER_BANDS - 1, FILTER_BANDS, dtype=F32)[None]
    z = jnp.concatenate([t, jnp.cos(bands * w), -jnp.sin(bands * w)], axis=-1)
    zp = _pad2(z, length, LANES)
    full = pl.BlockSpec((LANES, LANES), lambda i: (0, 0))
    row = pl.BlockSpec((1, LANES), lambda i: (0, 0))
    return pl.pallas_call(
        _hid_kernel,
        grid=(length // tl,),
        in_specs=[pl.BlockSpec((tl, LANES), lambda i: (i, 0)), full, row, full, row, full, row, row],
        out_specs=pl.BlockSpec((tl, LANES), lambda i: (i, 0)),
        out_shape=jax.ShapeDtypeStruct((length, LANES), F32),
        compiler_params=_params(("parallel",), 16 << 20),
        name="filter_hidden",
    )(zp, _pad2(w1, LANES, LANES), _pad2(b1[None], 1, LANES), _pad2(w2, LANES, LANES), _pad2(b2[None], 1, LANES),
      _pad2(w3, LANES, LANES), _pad2(b3[None], 1, LANES), _pad2(freq[None], 1, LANES))


def _filter_kernel(hid_ref, w_ref, delta_ref, f_ref, n_ref, *, length):
    li = pl.program_id(1)
    tl, dc = f_ref.shape[1], f_ref.shape[2]

    @pl.when(li == 0)
    def _():
        n_ref[...] = jnp.zeros_like(n_ref)

    pos = li * tl + lax.broadcasted_iota(jnp.int32, (tl, dc), 0)
    t = pos.astype(F32) / float(length - 1)
    decay = jnp.exp(-t * delta_ref[...])
    hid = hid_ref[...]
    for q in range(2 * HYENA_ORDER):
        f = jnp.dot(hid, w_ref[q], preferred_element_type=F32, precision=lax.Precision.HIGHEST) * decay
        f_ref[q] = f
        o = q // 2
        n_ref[o:o + 1, :] += jnp.sum(jnp.abs(f), axis=0, keepdims=True)


def hyena_filters(hid, w_o, length, tl=1024, dc=512):
    d = w_o.shape[2]
    delta = jnp.abs(jnp.linspace(MIN_DECAY, MAX_DECAY, d, dtype=F32))[None]
    nq = 2 * HYENA_ORDER
    return pl.pallas_call(
        functools.partial(_filter_kernel, length=length),
        grid=(d // dc, length // tl),
        in_specs=[
            pl.BlockSpec((tl, LANES), lambda c, l: (l, 0)),
            pl.BlockSpec((nq, LANES, dc), lambda c, l: (0, 0, c)),
            pl.BlockSpec((1, dc), lambda c, l: (0, c)),
        ],
        out_specs=[
            pl.BlockSpec((nq, tl, dc), lambda c, l: (0, l, c)),
            pl.BlockSpec((F32_SUBLANES, dc), lambda c, l: (0, c)),
        ],
        out_shape=[jax.ShapeDtypeStruct((nq, length, d), F32), jax.ShapeDtypeStruct((F32_SUBLANES, d), F32)],
        compiler_params=_params(("parallel", "arbitrary"), 2 * nq * tl * dc * 4 + 6 * tl * dc * 4 + (4 << 20)),
        name="hyena_filters",
    )(hid, w_o, delta)


FFT_N1 = 256
FFT_N2 = 128
FFT_K1 = FFT_N1 // 2
FFT_GROUP = BF16_SUBLANES
FFT_PITCH = 3 * F32_SUBLANES


def _phase_tables(length):
    n1h, n2, k1n = FFT_K1, FFT_N2, FFT_K1
    assert length == n1h * n2
    n = 2 * length
    two_pi = 2.0 * math.pi
    ia = jnp.arange(n1h, dtype=jnp.int32)
    pa = (ia[None, :] * (2 * ia[:, None] + 1)) % (2 * FFT_N1)
    tha = pa.astype(F32) * (two_pi / (2 * FFT_N1))
    fa = jnp.concatenate([jnp.cos(tha), -jnp.sin(tha)], axis=0).astype(BF16)
    ca = (jnp.concatenate([jnp.cos(tha).T, -jnp.sin(tha).T], axis=1) * (2.0 / n)).astype(BF16)
    k1 = jnp.arange(k1n, dtype=jnp.int32)[:, None, None]
    k2 = jnp.arange(n2, dtype=jnp.int32)[None, :, None]
    m2 = jnp.arange(n2, dtype=jnp.int32)[None, None, :]
    pb = (m2 * (2 * (k1 + FFT_N1 * k2) + 1)) % (2 * n)
    phb = pb.astype(F32) * (two_pi / (2 * n))
    gr, gi = jnp.cos(phb), -jnp.sin(phb)
    gb = jnp.concatenate([jnp.concatenate([gr, -gi], axis=2), jnp.concatenate([gi, gr], axis=2)], axis=1)
    hr, hi = jnp.swapaxes(gr, 1, 2), -jnp.swapaxes(gi, 1, 2)
    gbi = jnp.concatenate([jnp.concatenate([hr, -hi], axis=2), jnp.concatenate([hi, hr], axis=2)], axis=1)
    return fa, ca, gb.astype(BF16), gbi.astype(BF16)


def _regroup_rows(k):
    return pl.ds(pl.multiple_of(k * FFT_PITCH, F32_SUBLANES), FFT_GROUP)


def _fft_a_kernel(*refs, ns):
    x_refs, fa_ref, o_ref = refs[:ns], refs[ns], refs[ns + 1]
    xs_refs = refs[ns + 2:2 * ns + 2]
    s_refs = refs[2 * ns + 2:]
    fa = fa_ref[...]
    for s in range(ns):
        xs_refs[s][...] = x_refs[s][...].reshape(FFT_K1 * FFT_GROUP, LANES)
    for j in range(FFT_GROUP):
        x = jnp.concatenate([r[pl.ds(j, FFT_K1, stride=FFT_GROUP), :] for r in xs_refs], axis=1)
        res = jnp.dot(fa, x.astype(BF16), preferred_element_type=F32)
        for p in range(2):
            for s in range(ns):
                s_refs[p * ns + s][pl.ds(j, FFT_K1, stride=FFT_PITCH), :] = (
                    res[p * FFT_K1:(p + 1) * FFT_K1, s * LANES:(s + 1) * LANES])

    def emit(k, carry):
        for p in range(2):
            for s in range(ns):
                o_ref[0, k, p, :, s * LANES:(s + 1) * LANES] = s_refs[p * ns + s][_regroup_rows(k), :].astype(o_ref.dtype)
        return carry

    lax.fori_loop(0, FFT_K1, emit, 0, unroll=8)


def fft_pass_a(x, col0, d, dc=256):
    q, length, c = x.shape
    ns = dc // LANES
    x4 = x.reshape(q, FFT_K1, FFT_N2, c)
    slab0 = col0 // LANES
    in_specs = [pl.BlockSpec((None, FFT_K1, FFT_GROUP, LANES),
                             functools.partial(lambda w, ci, g, s: (w, 0, g, slab0 + ci * ns + s), s=s))
                for s in range(ns)]
    fa = _phase_tables(length)[0]
    in_specs.append(pl.BlockSpec(fa.shape, lambda w, ci, g: (0, 0)))
    return pl.pallas_call(
        functools.partial(_fft_a_kernel, ns=ns),
        grid=(q, d // dc, FFT_N2 // FFT_GROUP),
        in_specs=in_specs,
        out_specs=pl.BlockSpec((1, FFT_K1, 2, FFT_GROUP, dc), lambda w, ci, g: (w, 0, 0, g, ci)),
        out_shape=jax.ShapeDtypeStruct((q, FFT_K1, 2, FFT_N2, d), BF16),
        scratch_shapes=([pltpu.VMEM((FFT_K1 * FFT_GROUP, LANES), F32)] * ns
                        + [pltpu.VMEM((FFT_K1 * FFT_PITCH, LANES), F32)] * (2 * ns)),
        compiler_params=_params(("parallel", "parallel", "parallel"), 40 << 20),
        name="fft_pass_a",
    )(*([x4] * ns), fa)


def _fft_bk_kernel(af_ref, ab_ref, gb_ref, o_ref):
    for i in range(af_ref.shape[1]):
        g = gb_ref[i]
        uf = jnp.dot(g, af_ref[0, i], preferred_element_type=F32)
        ub = jnp.dot(g, ab_ref[0, i], preferred_element_type=F32)
        n2 = FFT_N2
        o_ref[0, i, 0:n2, :] = (uf[0:n2] + ub[0:n2]).astype(o_ref.dtype)
        o_ref[0, i, n2:2 * n2, :] = (uf[n2:2 * n2] - ub[n2:2 * n2]).astype(o_ref.dtype)


def fft_filter_spectrum(a, gb, kb=8, dc=512):
    q, k1, r, d = a.shape
    blk = lambda f: pl.BlockSpec((1, kb, r, dc), f)
    return pl.pallas_call(
        _fft_bk_kernel,
        grid=(q // 2, k1 // kb, d // dc),
        in_specs=[blk(lambda o, k, c: (2 * o, k, 0, c)), blk(lambda o, k, c: (2 * o + 1, k, 0, c)),
                  pl.BlockSpec((kb, r, r), lambda o, k, c: (k, 0, 0))],
        out_specs=blk(lambda o, k, c: (o, k, 0, c)),
        out_shape=jax.ShapeDtypeStruct((q // 2, k1, r, d), BF16),
        compiler_params=_params(("parallel", "parallel", "parallel"), 40 << 20),
        name="fft_filter_spectrum",
    )(a, a, gb)


def _fft_b_kernel(a_ref, k_ref, gb_ref, gbi_ref, o_ref):
    n2 = FFT_N2
    for i in range(a_ref.shape[0]):
        u = jnp.dot(gb_ref[i], a_ref[i], preferred_element_type=F32)
        ur, ui = u[0:n2], u[n2:2 * n2]
        kr, ki = k_ref[0, i, 0:n2, :].astype(F32), k_ref[0, i, n2:2 * n2, :].astype(F32)
        v = jnp.concatenate([ur * kr - ui * ki, ur * ki + ui * kr], axis=0).astype(BF16)
        o_ref[i] = jnp.dot(gbi_ref[i], v, preferred_element_type=F32).astype(o_ref.dtype)


def fft_pass_b(a, kspec, order, gb, gbi, kb=8, dc=512):
    k1, r, d = a.shape
    return pl.pallas_call(
        _fft_b_kernel,
        grid=(k1 // kb, d // dc),
        in_specs=[pl.BlockSpec((kb, r, dc), lambda k, c: (k, 0, c)),
                  pl.BlockSpec((1, kb, r, dc), lambda k, c: (order, k, 0, c)),
                  pl.BlockSpec((kb, r, r), lambda k, c: (k, 0, 0)),
                  pl.BlockSpec((kb, r, r), lambda k, c: (k, 0, 0))],
        out_specs=pl.BlockSpec((kb, r, dc), lambda k, c: (k, 0, c)),
        out_shape=jax.ShapeDtypeStruct((k1, r, d), BF16),
        compiler_params=_params(("parallel", "parallel"), 40 << 20),
        name="fft_pass_b",
    )(a, kspec, gb, gbi)


def _fft_c_kernel(*refs, ns):
    b_ref, ca_ref, gate_refs, z_refs = refs[0], refs[1], refs[2:2 + ns], refs[2 + ns:2 + 2 * ns]
    n_ref, bias_ref, o_ref = refs[2 + 2 * ns:5 + 2 * ns]
    s_refs = refs[5 + 2 * ns:5 + 4 * ns]
    t_refs = refs[5 + 4 * ns:]
    ca = ca_ref[...]

    def spread(k, carry):
        for p in range(2):
            for s in range(ns):
                s_refs[p * ns + s][_regroup_rows(k), :] = b_ref[k, p, :, s * LANES:(s + 1) * LANES].astype(F32)
        return carry

    lax.fori_loop(0, FFT_K1, spread, 0, unroll=8)
    for j in range(FFT_GROUP):
        b = jnp.concatenate([
            jnp.concatenate([s_refs[p * ns + s][pl.ds(j, FFT_K1, stride=FFT_PITCH), :] for s in range(ns)], axis=1)
            for p in range(2)], axis=0)
        y = jnp.dot(ca, b.astype(BF16), preferred_element_type=F32)
        for s in range(ns):
            t_refs[s][pl.ds(j, FFT_K1, stride=FFT_PITCH), :] = y[:, s * LANES:(s + 1) * LANES]

    def emit(k, carry):
        for s in range(ns):
            sl = slice(s * LANES, (s + 1) * LANES)
            y = t_refs[s][_regroup_rows(k), :] / (n_ref[:, sl] + 1e-6)
            o_ref[k, :, sl] = (gate_refs[s][k] * (y + bias_ref[:, sl] * z_refs[s][k])).astype(o_ref.dtype)
        return carry

    lax.fori_loop(0, FFT_K1, emit, 0, unroll=8)


def fft_pass_c(bp, ca, gate_arr, gate_col0, z_arr, z_col0, nsum, bias, out_dtype, dc=256):
    k1, r, d = bp.shape
    ns = dc // LANES
    length = FFT_K1 * FFT_N2
    b4 = bp.reshape(k1, 2, FFT_N2, d)
    g3 = gate_arr.reshape(FFT_K1, FFT_N2, gate_arr.shape[1])
    z3 = z_arr.reshape(FFT_K1, FFT_N2, z_arr.shape[1])
    sig = lambda slab0: [pl.BlockSpec((FFT_K1, FFT_GROUP, LANES),
                                      functools.partial(lambda ci, g, s: (0, g, slab0 + ci * ns + s), s=s))
                         for s in range(ns)]
    row = pl.BlockSpec((1, dc), lambda ci, g: (0, ci))
    out = pl.pallas_call(
        functools.partial(_fft_c_kernel, ns=ns),
        grid=(d // dc, FFT_N2 // FFT_GROUP),
        in_specs=[pl.BlockSpec((k1, 2, FFT_GROUP, dc), lambda ci, g: (0, 0, g, ci)),
                  pl.BlockSpec(ca.shape, lambda ci, g: (0, 0)),
                  *sig(gate_col0 // LANES), *sig(z_col0 // LANES), row, row],
        out_specs=pl.BlockSpec((FFT_K1, FFT_GROUP, dc), lambda ci, g: (0, g, ci)),
        out_shape=jax.ShapeDtypeStruct((FFT_K1, FFT_N2, d), out_dtype),
        scratch_shapes=[pltpu.VMEM((FFT_K1 * FFT_PITCH, LANES), F32)] * (3 * ns),
        compiler_params=_params(("parallel", "parallel"), 40 << 20),
        name="fft_pass_c",
    )(b4, ca, *([g3] * ns), *([z3] * ns), nsum, bias)
    return out.reshape(length, d)


def kernel(x, c, ctx, c_ctx, ada_w, ada_b, norm_g, mix_w_in, mix_conv_w, mix_sink, mix_w_out, ffn_w_gu, ffn_w_down, hy_w_in, hy_b_in, hy_conv_w, hy_conv_b, hf_w1, hf_b1, hf_freq, hf_w2, hf_b2, hf_w3, hf_b3, hf_w_out, hf_bias, hy_w_out, hy_b_out, router_w, router_b, moe_w_gu, moe_w_down):
    assert x.shape[0] == 1 and ada_w.shape[0] == 2
    s, d = x.shape[1], x.shape[2]
    xs = x[0]
    ctxs = ctx[0]

    acts = jnp.zeros((F32_SUBLANES, d), F32).at[0].set(c[0]).at[1].set(c_ctx)
    mods = adaln_all(acts, ada_w, ada_b)
    row = lambda layer, r, k: mods[layer, r:r + 1, k * d:(k + 1) * d]

    g = norm_g[0]
    h = norm_mod(xs, g[0:1], row(0, 0, 0), row(0, 0, 1))
    hc = norm_mod(ctxs, g[0:1], row(0, 1, 0), row(0, 1, 1))
    w_in = mix_w_in[0].astype(BF16)
    p = matmul(h, w_in, out_dtype=BF16)
    ckv = matmul(hc, w_in[:, KV_START:], out_dtype=BF16)
    cs, sn = rope_tables(s)
    y = even_mixer_core(p, ckv, cs, sn, mix_conv_w[0], mix_sink[0])
    out = matmul(y, mix_w_out[0].astype(BF16))
    xs = post_residual(xs, out, g[1:2], row(0, 0, 2))

    h = norm_mod(xs, g[2:3], row(0, 0, 3), row(0, 0, 4))
    act = swiglu_up(h, ffn_w_gu[0].astype(BF16)[None], tm=2048, tn=256)
    out = matmul(act[0], ffn_w_down[0].astype(BF16), tm=512, tn=256)
    xs = post_residual(xs, out, g[3:4], row(0, 0, 5))

    g = norm_g[1]
    h = norm_mod(xs, g[0:1], row(1, 0, 0), row(1, 0, 1))
    u0 = matmul(h, hy_w_in[0].astype(BF16), bias=hy_b_in[0][None])
    u = short_conv(u0, hy_conv_w[0], hy_conv_b[0][None])
    hid = filter_hidden(s, hf_w1[0], hf_b1[0], hf_freq[0], hf_w2[0], hf_b2[0], hf_w3[0], hf_b3[0])
    w_o = hf_w_out[0].astype(F32).reshape(FILTER_HIDDEN, HYENA_ORDER * 2, d).transpose(1, 0, 2)
    w_o = jnp.zeros((HYENA_ORDER * 2, LANES, d), F32).at[:, :FILTER_HIDDEN].set(w_o)
    filt, nsum = hyena_filters(hid, w_o, s)
    _, ca, gb, gbi = _phase_tables(s)
    spec = lambda a: a.reshape(a.shape[0], FFT_K1, 2 * FFT_N2, d)
    kspec = fft_filter_spectrum(spec(fft_pass_a(filt, 0, d)), gb)
    a = spec(fft_pass_a(u[None], 2 * d, d))[0]
    z1 = fft_pass_c(fft_pass_b(a, kspec, 0, gb, gbi), ca, u, 0, u, 2 * d, nsum[0:1], hf_bias[0, 0][None], F32)
    a = spec(fft_pass_a(z1[None], 0, d))[0]
    z2 = fft_pass_c(fft_pass_b(a, kspec, 1, gb, gbi), ca, u, d, z1, 0, nsum[1:2], hf_bias[0, 1][None], BF16)
    out = matmul(z2, hy_w_out[0].astype(BF16), bias=hy_b_out[0][None])
    xs = post_residual(xs, out, g[1:2], row(1, 0, 2))

    sel, wts = router_top2(xs, g[2:3], row(1, 0, 3), row(1, 0, 4), router_w[0], router_b[0])
    pos, src_tok, tile_expert, n_valid = moe_plan(sel)
    hg = moe_gather_norm(xs, src_tok, g[2:3], row(1, 0, 3), row(1, 0, 4))
    act = grouped_swiglu_up(hg, moe_w_gu[0].astype(BF16), tile_expert, n_valid)
    ys = grouped_down(act, moe_w_down[0].astype(BF16), tile_expert, n_valid)
    xs = moe_combine_residual(xs, wts, ys, pos, g[3:4], row(1, 0, 5))
    return xs[None]
```

```python
import functools
import math

import jax
import jax.numpy as jnp
from jax import lax
from jax.experimental import pallas as pl
from jax.experimental.pallas import tpu as pltpu

F32 = jnp.float32
BF16 = jnp.bfloat16

D_MODEL = 4096
GRID_W = 64
HEAD_DIM = 128
CONV_WIDTH = D_MODEL // 2
N_HEADS = (D_MODEL // 2) // HEAD_DIM
N_KV_HEADS = N_HEADS // 4
GQA_GROUP = N_HEADS // N_KV_HEADS
Q_WIDTH = N_HEADS * HEAD_DIM
KV_WIDTH = N_KV_HEADS * HEAD_DIM
Q_START = 3 * CONV_WIDTH
KV_START = Q_START + Q_WIDTH
IN_WIDTH = KV_START + 2 * KV_WIDTH
BLOCK = 128
ATTN_SCALE = HEAD_DIM ** -0.5
ROPE_BASE = 10000.0
ROPE_PAIRS = HEAD_DIM // 4
HYENA_ORDER = 2
FILTER_EMB = 33
FILTER_BANDS = (FILTER_EMB - 1) // 2
FILTER_HIDDEN = 64
MIN_DECAY = math.log(1e-2) / 0.3
MAX_DECAY = math.log(1e-2) / 1.5
N_EXPERTS = 8
NORM_EPS = 1e-6

V7X_VMEM_BYTES = 64 * 1024 * 1024
LANES = 128
F32_SUBLANES = 8
BF16_SUBLANES = 16


def _params(semantics, vmem_bytes):
    limit = min(int(vmem_bytes), V7X_VMEM_BYTES - 4 * 1024 * 1024)
    return pltpu.CompilerParams(dimension_semantics=semantics, vmem_limit_bytes=limit)


def _sigmoid(v):
    return 1.0 / (1.0 + jnp.exp(-v))


def _adaln_kernel(a_ref, w_ref, b_ref, o_ref, acc_ref):
    k = pl.program_id(2)

    @pl.when(k == 0)
    def _():
        acc_ref[...] = jnp.zeros_like(acc_ref)

    a = a_ref[...]
    a = a * _sigmoid(a)
    acc_ref[...] += jnp.dot(a.astype(BF16), w_ref[0].astype(BF16), preferred_element_type=F32)

    @pl.when(k == pl.num_programs(2) - 1)
    def _():
        o_ref[0] = acc_ref[...] + b_ref[0]


def adaln_all(acts, ada_w, ada_b):
    depth, d, n = ada_w.shape
    tn, tk = 2048, 1024
    return pl.pallas_call(
        _adaln_kernel,
        grid=(depth, n // tn, d // tk),
        in_specs=[
            pl.BlockSpec((F32_SUBLANES, tk), lambda l, j, k: (0, k)),
            pl.BlockSpec((1, tk, tn), lambda l, j, k: (l, k, j)),
            pl.BlockSpec((1, 1, tn), lambda l, j, k: (l, 0, j)),
        ],
        out_specs=pl.BlockSpec((1, F32_SUBLANES, tn), lambda l, j, k: (l, 0, j)),
        out_shape=jax.ShapeDtypeStruct((depth, F32_SUBLANES, n), F32),
        scratch_shapes=[pltpu.VMEM((F32_SUBLANES, tn), F32)],
        compiler_params=_params(("parallel", "parallel", "arbitrary"), 2 * tk * tn * 4 + tk * tn * 2 + (4 << 20)),
        name="adaln",
    )(acts, ada_w, ada_b.reshape(depth, 1, n))


def _norm_mod(x, g, sh, sc):
    y = x * lax.rsqrt(jnp.mean(x * x, axis=-1, keepdims=True) + NORM_EPS)
    return (y * g) * (1.0 + sc) + sh


def _norm_mod_kernel(x_ref, g_ref, sh_ref, sc_ref, o_ref):
    o_ref[...] = _norm_mod(x_ref[...], g_ref[...], sh_ref[...], sc_ref[...]).astype(o_ref.dtype)


def norm_mod(x, g, shift, scale, tm=512):
    m, d = x.shape
    tm = min(tm, m)
    row = pl.BlockSpec((1, d), lambda i: (0, 0))
    return pl.pallas_call(
        _norm_mod_kernel,
        grid=(m // tm,),
        in_specs=[pl.BlockSpec((tm, d), lambda i: (i, 0)), row, row, row],
        out_specs=pl.BlockSpec((tm, d), lambda i: (i, 0)),
        out_shape=jax.ShapeDtypeStruct((m, d), BF16),
        compiler_params=_params(("parallel",), 2 * tm * d * 4 + 2 * tm * d * 2 + 3 * tm * d * 4 + (2 << 20)),
        name="norm_mod",
    )(x, g, shift, scale)


def _post_kernel(x_ref, y_ref, g_ref, gt_ref, o_ref):
    y = y_ref[...]
    yn = y * lax.rsqrt(jnp.mean(y * y, axis=-1, keepdims=True) + NORM_EPS)
    o_ref[...] = x_ref[...] + gt_ref[...] * (yn * g_ref[...])


def post_residual(x, y, g, gate, tm=512):
    m, d = x.shape
    row = pl.BlockSpec((1, d), lambda i: (0, 0))
    blk = pl.BlockSpec((tm, d), lambda i: (i, 0))
    return pl.pallas_call(
        _post_kernel,
        grid=(m // tm,),
        in_specs=[blk, blk, row, row],
        out_specs=blk,
        out_shape=jax.ShapeDtypeStruct((m, d), F32),
        compiler_params=_params(("parallel",), 6 * tm * d * 4 + 3 * tm * d * 4 + (2 << 20)),
        name="post_residual",
    )(x, y, g, gate)


def _post_norm_kernel(x_ref, y_ref, g_ref, gt_ref, g2_ref, sh_ref, sc_ref, o_ref, h_ref):
    y = y_ref[...]
    yn = y * lax.rsqrt(jnp.mean(y * y, axis=-1, keepdims=True) + NORM_EPS)
    x = x_ref[...] + gt_ref[...] * (yn * g_ref[...])
    o_ref[...] = x
    h_ref[...] = _norm_mod(x, g2_ref[...], sh_ref[...], sc_ref[...]).astype(h_ref.dtype)


def post_norm(x, y, g, gate, g_next, shift, scale, tm=256):
    m, d = x.shape
    row = pl.BlockSpec((1, d), lambda i: (0, 0))
    blk = pl.BlockSpec((tm, d), lambda i: (i, 0))
    return pl.pallas_call(
        _post_norm_kernel,
        grid=(m // tm,),
        in_specs=[blk, blk, row, row, row, row, row],
        out_specs=[blk, blk],
        out_shape=[jax.ShapeDtypeStruct((m, d), F32), jax.ShapeDtypeStruct((m, d), BF16)],
        compiler_params=_params(("parallel",), 7 * tm * d * 4 + 5 * tm * d * 4 + (2 << 20)),
        name="post_norm",
    )(x, y, g, gate, g_next, shift, scale)


def _mm_kernel(a_ref, w_ref, *rest, has_bias):
    o_ref = rest[-1]
    acc = jnp.dot(a_ref[...], w_ref[...], preferred_element_type=F32)
    if has_bias:
        acc = acc + rest[0][...]
    o_ref[...] = acc.astype(o_ref.dtype)


def matmul(a, w, bias=None, out_dtype=F32, tm=1024, tn=512):
    m, k = a.shape
    n = w.shape[1]
    tm, tn = min(tm, m), min(tn, n)
    in_specs = [pl.BlockSpec((tm, k), lambda i, j: (i, 0)), pl.BlockSpec((k, tn), lambda i, j: (0, j))]
    args = [a, w]
    if bias is not None:
        in_specs.append(pl.BlockSpec((1, tn), lambda i, j: (0, j)))
        args.append(bias)
    osz = jnp.dtype(out_dtype).itemsize
    vmem = 2 * tm * k * 2 + 2 * k * tn * 2 + 2 * tm * tn * osz + 2 * tm * tn * 4 + (2 << 20)
    return pl.pallas_call(
        functools.partial(_mm_kernel, has_bias=bias is not None),
        grid=(m // tm, n // tn),
        in_specs=in_specs,
        out_specs=pl.BlockSpec((tm, tn), lambda i, j: (i, j)),
        out_shape=jax.ShapeDtypeStruct((m, n), out_dtype),
        compiler_params=_params(("parallel", "arbitrary"), vmem),
        name="matmul",
    )(*args)


def _swiglu_up_kernel(a_ref, wg_ref, wu_ref, o_ref):
    a = a_ref[...]
    g = jnp.dot(a, wg_ref[0], preferred_element_type=F32)
    u = jnp.dot(a, wu_ref[0], preferred_element_type=F32)
    o_ref[0] = (g * _sigmoid(g) * u).astype(o_ref.dtype)


def swiglu_up(a, w_gu, tm, tn):
    m, k = a.shape
    e, _, f2 = w_gu.shape
    f = f2 // 2
    nj = f // tn
    vmem = 2 * tm * k * 2 + 4 * k * tn * 2 + 2 * tm * tn * 2 + 4 * tm * tn * 4 + (2 << 20)
    return pl.pallas_call(
        _swiglu_up_kernel,
        grid=(e, m // tm, nj),
        in_specs=[
            pl.BlockSpec((tm, k), lambda x, i, j: (i, 0)),
            pl.BlockSpec((1, k, tn), lambda x, i, j: (x, 0, j)),
            pl.BlockSpec((1, k, tn), lambda x, i, j: (x, 0, j + nj)),
        ],
        out_specs=pl.BlockSpec((1, tm, tn), lambda x, i, j: (x, i, j)),
        out_shape=jax.ShapeDtypeStruct((e, m, f), BF16),
        compiler_params=_params(("parallel", "parallel", "arbitrary"), vmem),
        name="swiglu_up",
    )(a, w_gu, w_gu)


def _router_kernel(x_ref, g_ref, sh_ref, sc_ref, w_ref, b_ref, sel_ref, wts_ref):
    h = _norm_mod(x_ref[...], g_ref[...], sh_ref[...], sc_ref[...])
    logits = jnp.dot(h, w_ref[...], preferred_element_type=F32, precision=lax.Precision.HIGHEST) + b_ref[...]
    lane = lax.broadcasted_iota(jnp.int32, logits.shape, 1)
    neg = jnp.float32(-jnp.inf)
    logits = jnp.where(lane < N_EXPERTS, logits, neg)
    v1 = jnp.max(logits, axis=1, keepdims=True)
    i1 = jnp.min(jnp.where(logits == v1, lane, LANES), axis=1, keepdims=True)
    rest = jnp.where(lane == i1, neg, logits)
    v2 = jnp.max(rest, axis=1, keepdims=True)
    i2 = jnp.min(jnp.where(rest == v2, lane, LANES), axis=1, keepdims=True)
    e2 = jnp.exp(v2 - v1)
    w1 = 1.0 / (1.0 + e2)
    w2 = e2 / (1.0 + e2)
    sel_ref[...] = jnp.where(lane == 0, i1, jnp.where(lane == 1, i2, 0))
    wts_ref[...] = jnp.where(lane == 0, w1, jnp.where(lane == 1, w2, 0.0))


def router_top2(x, g, shift, scale, router_w, router_b, tm=256):
    m, d = x.shape
    wp = jnp.zeros((d, LANES), F32).at[:, :N_EXPERTS].set(router_w)
    bp = jnp.zeros((1, LANES), F32).at[0, :N_EXPERTS].set(router_b)
    row = pl.BlockSpec((1, d), lambda i: (0, 0))
    out = pl.BlockSpec((tm, LANES), lambda i: (i, 0))
    return pl.pallas_call(
        _router_kernel,
        grid=(m // tm,),
        in_specs=[pl.BlockSpec((tm, d), lambda i: (i, 0)), row, row, row,
                  pl.BlockSpec((d, LANES), lambda i: (0, 0)), pl.BlockSpec((1, LANES), lambda i: (0, 0))],
        out_specs=[out, out],
        out_shape=[jax.ShapeDtypeStruct((m, LANES), jnp.int32), jax.ShapeDtypeStruct((m, LANES), F32)],
        compiler_params=_params(("parallel",), 2 * tm * d * 4 + 6 * tm * d * 4 + 2 * d * LANES * 4 + (4 << 20)),
        name="router",
    )(x, g, shift, scale, wp, bp)


MOE_TILE = 512


def moe_plan(sel):
    t = sel.shape[0]
    npairs = 2 * t
    rows = npairs + N_EXPERTS * MOE_TILE
    e_flat = sel[:, :2].reshape(npairs)
    order = jnp.argsort(e_flat, stable=True).astype(jnp.int32)
    e_sorted = e_flat[order]
    counts = jnp.sum((e_flat[:, None] == jnp.arange(N_EXPERTS, dtype=jnp.int32)[None, :]).astype(jnp.int32), axis=0)
    padded = (counts + MOE_TILE - 1) // MOE_TILE * MOE_TILE
    ends_p = jnp.cumsum(padded)
    starts_p = ends_p - padded
    starts_u = jnp.cumsum(counts) - counts
    dest_sorted = starts_p[e_sorted] + jnp.arange(npairs, dtype=jnp.int32) - starts_u[e_sorted]
    pos = jnp.zeros((npairs,), jnp.int32).at[order].set(dest_sorted)
    src_tok = jnp.zeros((rows,), jnp.int32).at[dest_sorted].set(order // 2)
    tile_start = jnp.arange(rows // MOE_TILE, dtype=jnp.int32) * MOE_TILE
    tile_expert = jnp.minimum(jnp.searchsorted(ends_p, tile_start, side="right"), N_EXPERTS - 1).astype(jnp.int32)
    n_valid = (ends_p[-1] // MOE_TILE).astype(jnp.int32).reshape(1)
    return pos, src_tok, tile_expert, n_valid


MOE_GATHER_CHUNKS = 4


def _gather_rows(idx_ref, base, stride, src_hbm, dst_ref, sem, r0, n, wait):
    def body(r, carry):
        copy = pltpu.make_async_copy(src_hbm.at[pl.ds(idx_ref[base + stride * r], 1)], dst_ref.at[pl.ds(r, 1)], sem)
        if wait:
            copy.wait()
        else:
            copy.start()
        return carry

    lax.fori_loop(r0, r0 + n, body, 0, unroll=8)


def _moe_gather_kernel(src_ref, x_hbm, g_ref, sh_ref, sc_ref, o_ref, buf_ref, sems):
    tg = buf_ref.shape[0]
    rc = tg // MOE_GATHER_CHUNKS
    base = pl.program_id(0) * tg
    for c in range(MOE_GATHER_CHUNKS):
        _gather_rows(src_ref, base, 1, x_hbm, buf_ref, sems.at[c], c * rc, rc, wait=False)
    for c in range(MOE_GATHER_CHUNKS):
        _gather_rows(src_ref, base, 1, x_hbm, buf_ref, sems.at[c], c * rc, rc, wait=True)
        rows = pl.ds(c * rc, rc)
        o_ref[rows, :] = _norm_mod(buf_ref[rows, :], g_ref[...], sh_ref[...], sc_ref[...]).astype(o_ref.dtype)


def moe_gather_norm(x, src_tok, g, shift, scale, tg=256):
    d = x.shape[1]
    rows = src_tok.shape[0]
    row = pl.BlockSpec((1, d), lambda i, src: (0, 0))
    return pl.pallas_call(
        _moe_gather_kernel,
        grid_spec=pltpu.PrefetchScalarGridSpec(
            num_scalar_prefetch=1,
            grid=(rows // tg,),
            in_specs=[pl.BlockSpec(memory_space=pl.ANY), row, row, row],
            out_specs=pl.BlockSpec((tg, d), lambda i, src: (i, 0)),
            scratch_shapes=[pltpu.VMEM((tg, d), F32), pltpu.SemaphoreType.DMA((MOE_GATHER_CHUNKS,))],
        ),
        out_shape=jax.ShapeDtypeStruct((rows, d), BF16),
        compiler_params=_params(("arbitrary",), 6 * tg * d * 4 + (2 << 20)),
        name="moe_gather_norm",
    )(src_tok, x, g, shift, scale)


def _gswiglu_kernel(te_ref, nv_ref, a_ref, wg_ref, wu_ref, o_ref):
    valid = pl.program_id(0) < nv_ref[0]

    @pl.when(valid)
    def _():
        a = a_ref[...]
        g = jnp.dot(a, wg_ref[0], preferred_element_type=F32)
        u = jnp.dot(a, wu_ref[0], preferred_element_type=F32)
        o_ref[...] = (g * _sigmoid(g) * u).astype(o_ref.dtype)

    @pl.when(jnp.logical_not(valid))
    def _():
        o_ref[...] = jnp.zeros_like(o_ref)


def grouped_swiglu_up(a, w_gu, tile_expert, n_valid, tn=512):
    rows, k = a.shape
    f = w_gu.shape[2] // 2
    nj = f // tn
    tm = MOE_TILE
    vmem = 2 * tm * k * 2 + 4 * k * tn * 2 + 2 * tm * tn * 2 + 4 * tm * tn * 4 + (2 << 20)
    return pl.pallas_call(
        _gswiglu_kernel,
        grid_spec=pltpu.PrefetchScalarGridSpec(
            num_scalar_prefetch=2,
            grid=(rows // tm, nj),
            in_specs=[
                pl.BlockSpec((tm, k), lambda i, j, te, nv: (i, 0)),
                pl.BlockSpec((1, k, tn), lambda i, j, te, nv: (te[i], 0, j)),
                pl.BlockSpec((1, k, tn), lambda i, j, te, nv: (te[i], 0, j + nj)),
            ],
            out_specs=pl.BlockSpec((tm, tn), lambda i, j, te, nv: (i, j)),
        ),
        out_shape=jax.ShapeDtypeStruct((rows, f), BF16),
        compiler_params=_params(("parallel", "arbitrary"), vmem),
        name="grouped_swiglu_up",
    )(tile_expert, n_valid, a, w_gu, w_gu)


def _gdown_kernel(te_ref, nv_ref, a_ref, w_ref, o_ref):
    valid = pl.program_id(0) < nv_ref[0]

    @pl.when(valid)
    def _():
        o_ref[...] = jnp.dot(a_ref[...], w_ref[0], preferred_element_type=F32)

    @pl.when(jnp.logical_not(valid))
    def _():
        o_ref[...] = jnp.zeros_like(o_ref)


def grouped_down(act, w_down, tile_expert, n_valid, tn=1024):
    rows, f = act.shape
    n = w_down.shape[2]
    tm = MOE_TILE
    vmem = 2 * tm * f * 2 + 2 * f * tn * 2 + 2 * tm * tn * 4 + 2 * tm * tn * 4 + (2 << 20)
    return pl.pallas_call(
        _gdown_kernel,
        grid_spec=pltpu.PrefetchScalarGridSpec(
            num_scalar_prefetch=2,
            grid=(rows // tm, n // tn),
            in_specs=[
                pl.BlockSpec((tm, f), lambda i, j, te, nv: (i, 0)),
                pl.BlockSpec((1, f, tn), lambda i, j, te, nv: (te[i], 0, j)),
            ],
            out_specs=pl.BlockSpec((tm, tn), lambda i, j, te, nv: (i, j)),
        ),
        out_shape=jax.ShapeDtypeStruct((rows, n), F32),
        compiler_params=_params(("parallel", "arbitrary"), vmem),
        name="grouped_down",
    )(tile_expert, n_valid, act, w_down)


def _moe_combine_kernel(pos_ref, x_ref, w_ref, ys_hbm, g_ref, gt_ref, o_ref, buf_ref, sems):
    tm = x_ref.shape[0]
    rc = tm // MOE_GATHER_CHUNKS
    base = 2 * pl.program_id(0) * tm
    for wait in (False, True):
        for c in range(MOE_GATHER_CHUNKS):
            for slot in range(2):
                _gather_rows(pos_ref, base + slot, 2, ys_hbm, buf_ref.at[slot], sems.at[c], c * rc, rc, wait=wait)
            if wait:
                rows = pl.ds(c * rc, rc)
                w = w_ref[rows, :]
                y = w[:, 0:1] * buf_ref[0, rows, :] + w[:, 1:2] * buf_ref[1, rows, :]
                yn = y * lax.rsqrt(jnp.mean(y * y, axis=-1, keepdims=True) + NORM_EPS)
                o_ref[rows, :] = x_ref[rows, :] + gt_ref[...] * (yn * g_ref[...])


def moe_combine_residual(x, wts, ys, pos, g, gate, tm=128):
    t, d = x.shape
    row = pl.BlockSpec((1, d), lambda i, p: (0, 0))
    return pl.pallas_call(
        _moe_combine_kernel,
        grid_spec=pltpu.PrefetchScalarGridSpec(
            num_scalar_prefetch=1,
            grid=(t // tm,),
            in_specs=[pl.BlockSpec((tm, d), lambda i, p: (i, 0)), pl.BlockSpec((tm, LANES), lambda i, p: (i, 0)),
                      pl.BlockSpec(memory_space=pl.ANY), row, row],
            out_specs=pl.BlockSpec((tm, d), lambda i, p: (i, 0)),
            scratch_shapes=[pltpu.VMEM((2, tm, d), F32), pltpu.SemaphoreType.DMA((MOE_GATHER_CHUNKS,))],
        ),
        out_shape=jax.ShapeDtypeStruct((t, d), F32),
        compiler_params=_params(("arbitrary",), 10 * tm * d * 4 + (2 << 20)),
        name="moe_combine_residual",
    )(pos, x, wts, ys, g, gate)


def _rope(x, cs, sn):
    lane = lax.broadcasted_iota(jnp.int32, x.shape, 1)
    first = (lane % (2 * ROPE_PAIRS)) < ROPE_PAIRS
    partner = jnp.where(first, pltpu.roll(x, HEAD_DIM - ROPE_PAIRS, 1), pltpu.roll(x, ROPE_PAIRS, 1))
    return x * cs + partner * sn


def _shift_rows(u, halo_prev, halo_next):
    n = u.shape[0]
    row = lax.broadcasted_iota(jnp.int32, u.shape, 0)
    prev = jnp.where(row == 0, halo_prev, pltpu.roll(u, 1, 0))
    nxt = jnp.where(row == n - 1, halo_next, pltpu.roll(u, n - 1, 0))
    return prev, nxt


def _mixer_kernel(sink_ref, p_ref, kvp_ref, kvn_ref, cp_ref, hp_ref, cn_ref, hn_ref, ckv_ref,
                  cs_ref, sn_ref, csp_ref, snp_ref, csn_ref, snn_ref, cw_ref, o_ref):
    i = pl.program_id(0)
    nb = pl.num_programs(0)
    cs, sn = cs_ref[...], sn_ref[...]

    a_b = p_ref[:, 0:CONV_WIDTH].astype(F32)
    cu = p_ref[:, CONV_WIDTH:2 * CONV_WIDTH].astype(F32) * p_ref[:, 2 * CONV_WIDTH:3 * CONV_WIDTH].astype(F32)
    last = BF16_SUBLANES - 1
    halo_p = cp_ref[last:last + 1, :].astype(F32) * hp_ref[last:last + 1, :].astype(F32)
    halo_n = cn_ref[0:1, :].astype(F32) * hn_ref[0:1, :].astype(F32)
    halo_p = jnp.where(i > 0, halo_p, 0.0)
    halo_n = jnp.where(i < nb - 1, halo_n, 0.0)
    cu_prev, cu_next = _shift_rows(cu, halo_p, halo_n)
    y_conv = a_b * (cu_prev * cw_ref[0:1, :] + cu * cw_ref[1:2, :] + cu_next * cw_ref[2:3, :])
    o_ref[:, 0:CONV_WIDTH] = y_conv.astype(o_ref.dtype)

    rows = GQA_GROUP * BLOCK
    qi = lax.broadcasted_iota(jnp.int32, (rows, 3 * BLOCK), 0) % BLOCK
    kj = lax.broadcasted_iota(jnp.int32, (rows, 3 * BLOCK), 1)
    key_pos = i * BLOCK + kj - BLOCK
    mask = (jnp.abs(qi + BLOCK - kj) <= BLOCK) & (key_pos >= 0) & (key_pos < nb * BLOCK)
    nt = (((1,), (1,)), ((), ()))

    for kh in range(N_KV_HEADS):
        ko = KV_START + kh * HEAD_DIM
        vo = KV_START + KV_WIDTH + kh * HEAD_DIM
        kb = kh * HEAD_DIM
        vb = KV_WIDTH + kh * HEAD_DIM
        k_band = jnp.concatenate([
            _rope(kvp_ref[:, kb:kb + HEAD_DIM].astype(F32), csp_ref[...], snp_ref[...]),
            _rope(p_ref[:, ko:ko + HEAD_DIM].astype(F32), cs, sn),
            _rope(kvn_ref[:, kb:kb + HEAD_DIM].astype(F32), csn_ref[...], snn_ref[...]),
        ], axis=0).astype(BF16)
        v_band = jnp.concatenate([kvp_ref[:, vb:vb + HEAD_DIM], p_ref[:, vo:vo + HEAD_DIM],
                                  kvn_ref[:, vb:vb + HEAD_DIM]], axis=0)
        k_ctx = ckv_ref[:, kb:kb + HEAD_DIM]
        v_ctx = ckv_ref[:, vb:vb + HEAD_DIM]
        qs, sinks = [], []
        for g in range(GQA_GROUP):
            h = kh * GQA_GROUP + g
            qo = Q_START + h * HEAD_DIM
            qs.append(_rope(p_ref[:, qo:qo + HEAD_DIM].astype(F32), cs, sn).astype(BF16))
            sinks.append(jnp.full((BLOCK, 1), sink_ref[h], F32))
        q = jnp.concatenate(qs, axis=0)
        sink = jnp.concatenate(sinks, axis=0)
        s_loc = lax.dot_general(q, k_band, nt, preferred_element_type=F32) * ATTN_SCALE
        s_loc = jnp.where(mask, s_loc, -jnp.inf)
        s_ctx = lax.dot_general(q, k_ctx, nt, preferred_element_type=F32) * ATTN_SCALE
        m = jnp.maximum(sink, jnp.maximum(jnp.max(s_loc, axis=1, keepdims=True),
                                          jnp.max(s_ctx, axis=1, keepdims=True)))
        e_loc = jnp.exp(s_loc - m)
        e_ctx = jnp.exp(s_ctx - m)
        denom = jnp.exp(sink - m) + jnp.sum(e_loc, axis=1, keepdims=True) + jnp.sum(e_ctx, axis=1, keepdims=True)
        o = (jnp.dot(e_loc.astype(BF16), v_band, preferred_element_type=F32)
             + jnp.dot(e_ctx.astype(BF16), v_ctx, preferred_element_type=F32)) / denom
        for g in range(GQA_GROUP):
            h = kh * GQA_GROUP + g
            oo = CONV_WIDTH + h * HEAD_DIM
            o_ref[:, oo:oo + HEAD_DIM] = o[g * BLOCK:(g + 1) * BLOCK, :].astype(o_ref.dtype)


def even_mixer_core(p, ckv, cs, sn, conv_w, sink):
    s = p.shape[0]
    nb = s // BLOCK
    hb = BLOCK // BF16_SUBLANES
    kvc = KV_START // (2 * KV_WIDTH)
    prev = lambda i: jnp.maximum(i - 1, 0)
    nxt = lambda i: jnp.minimum(i + 1, nb - 1)
    tab = lambda f: pl.BlockSpec((BLOCK, HEAD_DIM), lambda i: (f(i), 0))
    same = lambda i: i
    in_specs = [
        pl.BlockSpec(memory_space=pltpu.SMEM),
        pl.BlockSpec((BLOCK, IN_WIDTH), lambda i: (i, 0)),
        pl.BlockSpec((BLOCK, 2 * KV_WIDTH), lambda i: (prev(i), kvc)),
        pl.BlockSpec((BLOCK, 2 * KV_WIDTH), lambda i: (nxt(i), kvc)),
        pl.BlockSpec((BF16_SUBLANES, CONV_WIDTH), lambda i: (jnp.maximum(i * hb - 1, 0), 1)),
        pl.BlockSpec((BF16_SUBLANES, CONV_WIDTH), lambda i: (jnp.maximum(i * hb - 1, 0), 2)),
        pl.BlockSpec((BF16_SUBLANES, CONV_WIDTH), lambda i: (jnp.minimum((i + 1) * hb, nb * hb - 1), 1)),
        pl.BlockSpec((BF16_SUBLANES, CONV_WIDTH), lambda i: (jnp.minimum((i + 1) * hb, nb * hb - 1), 2)),
        pl.BlockSpec(ckv.shape, lambda i: (0, 0)),
        tab(same), tab(same), tab(prev), tab(prev), tab(nxt), tab(nxt),
        pl.BlockSpec(conv_w.shape, lambda i: (0, 0)),
    ]
    return pl.pallas_call(
        _mixer_kernel,
        grid=(nb,),
        in_specs=in_specs,
        out_specs=pl.BlockSpec((BLOCK, CONV_WIDTH + Q_WIDTH), lambda i: (i, 0)),
        out_shape=jax.ShapeDtypeStruct((s, CONV_WIDTH + Q_WIDTH), BF16),
        compiler_params=_params(("parallel",), 32 << 20),
        name="even_mixer_core",
    )(sink, p, p, p, p, p, p, p, ckv, cs, sn, cs, sn, cs, sn, conv_w)


def rope_tables(s):
    t = jnp.arange(s, dtype=jnp.int32)
    row = (t // GRID_W).astype(F32)
    col = (t % GRID_W).astype(F32)
    inv = ROPE_BASE ** (-jnp.arange(ROPE_PAIRS, dtype=F32) / ROPE_PAIRS)
    ang_r = row[:, None] * inv[None, :]
    ang_c = col[:, None] * inv[None, :]
    cs = jnp.concatenate([jnp.cos(ang_r)] * 2 + [jnp.cos(ang_c)] * 2, axis=1)
    sn = jnp.concatenate([-jnp.sin(ang_r), jnp.sin(ang_r), -jnp.sin(ang_c), jnp.sin(ang_c)], axis=1)
    return cs, sn


def _sconv_kernel(u_ref, up_ref, un_ref, w_ref, b_ref, o_ref):
    i = pl.program_id(0)
    u = u_ref[...].astype(F32)
    last = up_ref.shape[0] - 1
    halo_p = jnp.where(i > 0, up_ref[last:last + 1, :].astype(F32), 0.0)
    halo_n = jnp.where(i < pl.num_programs(0) - 1, un_ref[0:1, :].astype(F32), 0.0)
    prev, nxt = _shift_rows(u, halo_p, halo_n)
    o_ref[...] = (prev * w_ref[0:1, :] + u * w_ref[1:2, :] + nxt * w_ref[2:3, :] + b_ref[...]).astype(o_ref.dtype)


def short_conv(u, w, b, col0, width, out_dtype, tm=256, tn=2048):
    s = u.shape[0]
    hr = BF16_SUBLANES
    hb = tm // hr
    nh = s // hr
    j0 = col0 // tn
    return pl.pallas_call(
        _sconv_kernel,
        grid=(s // tm, width // tn),
        in_specs=[
            pl.BlockSpec((tm, tn), lambda i, j: (i, j + j0)),
            pl.BlockSpec((hr, tn), lambda i, j: (jnp.maximum(i * hb - 1, 0), j + j0)),
            pl.BlockSpec((hr, tn), lambda i, j: (jnp.minimum((i + 1) * hb, nh - 1), j + j0)),
            pl.BlockSpec((3, tn), lambda i, j: (0, j + j0)),
            pl.BlockSpec((1, tn), lambda i, j: (0, j + j0)),
        ],
        out_specs=pl.BlockSpec((tm, tn), lambda i, j: (i, j)),
        out_shape=jax.ShapeDtypeStruct((s, width), out_dtype),
        compiler_params=_params(("parallel", "parallel"), 10 * tm * tn * 4 + (2 << 20)),
        name="short_conv",
    )(u, u, u, w, b)


def _hid_kernel(z_ref, w1_ref, b1_ref, w2_ref, b2_ref, w3_ref, b3_ref, fr_ref, o_ref):
    hp = lax.Precision.HIGHEST
    fr = fr_ref[...]
    h = jnp.sin(fr * (jnp.dot(z_ref[...], w1_ref[...], preferred_element_type=F32, precision=hp) + b1_ref[...]))
    h = jnp.sin(fr * (jnp.dot(h, w2_ref[...], preferred_element_type=F32, precision=hp) + b2_ref[...]))
    o_ref[...] = jnp.sin(fr * (jnp.dot(h, w3_ref[...], preferred_element_type=F32, precision=hp) + b3_ref[...]))


def _pad2(a, r, c):
    return jnp.zeros((r, c), F32).at[:a.shape[0], :a.shape[1]].set(a.astype(F32))


def filter_hidden(length, w1, b1, freq, w2, b2, w3, b3, tl=2048):
    t = jnp.linspace(0.0, 1.0, length, dtype=F32)[:, None]
    w = (2.0 * math.pi / length) * jnp.arange(length, dtype=F32)[:, None]
    bands = jnp.linspace(1e-4, FILTER_BANDS - 1, FILTER_BANDS, dtype=F32)[None]
    z = jnp.concatenate([t, jnp.cos(bands * w), -jnp.sin(bands * w)], axis=-1)
    zp = _pad2(z, length, LANES)
    full = pl.BlockSpec((LANES, LANES), lambda i: (0, 0))
    row = pl.BlockSpec((1, LANES), lambda i: (0, 0))
    return pl.pallas_call(
        _hid_kernel,
        grid=(length // tl,),
        in_specs=[pl.BlockSpec((tl, LANES), lambda i: (i, 0)), full, row, full, row, full, row, row],
        out_specs=pl.BlockSpec((tl, LANES), lambda i: (i, 0)),
        out_shape=jax.ShapeDtypeStruct((length, LANES), F32),
        compiler_params=_params(("parallel",), 16 << 20),
        name="filter_hidden",
    )(zp, _pad2(w1, LANES, LANES), _pad2(b1[None], 1, LANES), _pad2(w2, LANES, LANES), _pad2(b2[None], 1, LANES),
      _pad2(w3, LANES, LANES), _pad2(b3[None], 1, LANES), _pad2(freq[None], 1, LANES))


FFT_N1 = 256
FFT_N2 = 128
FFT_K1 = FFT_N1 // 2
FFT_GROUP = BF16_SUBLANES
FFT_PITCH = 3 * F32_SUBLANES


def _phase_tables(length):
    n1h, n2, k1n = FFT_K1, FFT_N2, FFT_K1
    assert length == n1h * n2
    n = 2 * length
    two_pi = 2.0 * math.pi
    ia = jnp.arange(n1h, dtype=jnp.int32)
    pa = (ia[None, :] * (2 * ia[:, None] + 1)) % (2 * FFT_N1)
    tha = pa.astype(F32) * (two_pi / (2 * FFT_N1))
    fa = jnp.concatenate([jnp.cos(tha), -jnp.sin(tha)], axis=0).astype(BF16)
    ca = (jnp.concatenate([jnp.cos(tha).T, -jnp.sin(tha).T], axis=1) * (2.0 / n)).astype(BF16)
    k1 = jnp.arange(k1n, dtype=jnp.int32)[:, None, None]
    k2 = jnp.arange(n2, dtype=jnp.int32)[None, :, None]
    m2 = jnp.arange(n2, dtype=jnp.int32)[None, None, :]
    pb = (m2 * (2 * (k1 + FFT_N1 * k2) + 1)) % (2 * n)
    phb = pb.astype(F32) * (two_pi / (2 * n))
    gr, gi = jnp.cos(phb), -jnp.sin(phb)
    gb = jnp.concatenate([jnp.concatenate([gr, -gi], axis=2), jnp.concatenate([gi, gr], axis=2)], axis=1)
    hr, hi = jnp.swapaxes(gr, 1, 2), -jnp.swapaxes(gi, 1, 2)
    gbi = jnp.concatenate([jnp.concatenate([hr, -hi], axis=2), jnp.concatenate([hi, hr], axis=2)], axis=1)
    return fa, ca, gb.astype(BF16), gbi.astype(BF16)


def _regroup_rows(k):
    return pl.ds(pl.multiple_of(k * FFT_PITCH, F32_SUBLANES), FFT_GROUP)


def _fft_a_core(rows_of, fa_ref, o_ref, s_refs, ns):
    fa = fa_ref[...]
    for j in range(FFT_GROUP):
        res = jnp.dot(fa, rows_of(j).astype(BF16), preferred_element_type=F32)
        for p in range(2):
            for s in range(ns):
                s_refs[p * ns + s][pl.ds(j, FFT_K1, stride=FFT_PITCH), :] = (
                    res[p * FFT_K1:(p + 1) * FFT_K1, s * LANES:(s + 1) * LANES])

    def emit(k, carry):
        for p in range(2):
            for s in range(ns):
                o_ref[0, k, p, :, s * LANES:(s + 1) * LANES] = s_refs[p * ns + s][_regroup_rows(k), :].astype(o_ref.dtype)
        return carry

    lax.fori_loop(0, FFT_K1, emit, 0, unroll=8)


def _fft_a_kernel(*refs, ns):
    x_refs, fa_ref, o_ref = refs[:ns], refs[ns], refs[ns + 1]
    xs_refs = refs[ns + 2:2 * ns + 2]
    for s in range(ns):
        xs_refs[s][...] = x_refs[s][...].reshape(FFT_K1 * FFT_GROUP, LANES)
    rows_of = lambda j: jnp.concatenate([r[pl.ds(j, FFT_K1, stride=FFT_GROUP), :] for r in xs_refs], axis=1)
    _fft_a_core(rows_of, fa_ref, o_ref, refs[2 * ns + 2:], ns)


def _fft_a_filter_kernel(hid_ref, w_ref, delta_ref, fa_ref, o_ref, n_ref, hs_ref, *s_refs, ns, length):
    g = pl.program_id(2)

    @pl.when(g == 0)
    def _():
        n_ref[...] = jnp.zeros_like(n_ref)

    hs_ref[...] = hid_ref[...].reshape(FFT_K1 * FFT_GROUP, LANES)
    w = w_ref[0].astype(BF16)
    delta = delta_ref[...]
    n1 = lax.broadcasted_iota(jnp.int32, (FFT_K1, ns * LANES), 0)

    def rows_of(j):
        hid = hs_ref[pl.ds(j, FFT_K1, stride=FFT_GROUP), :]
        t = (n1 * FFT_N2 + (g * FFT_GROUP + j)).astype(F32) / float(length - 1)
        f = jnp.dot(hid.astype(BF16), w, preferred_element_type=F32) * jnp.exp(-t * delta)
        n_ref[0, 0:1, :] += jnp.sum(jnp.abs(f), axis=0, keepdims=True)
        return f

    _fft_a_core(rows_of, fa_ref, o_ref, s_refs, ns)


def _fft_a_scratch(ns):
    return [pltpu.VMEM((FFT_K1 * FFT_PITCH, LANES), F32)] * (2 * ns)


def _fft_a_vmem(ns):
    blk = FFT_K1 * FFT_GROUP * LANES
    return 2 * ns * blk * 4 + 2 * 2 * ns * blk * 2 + ns * blk * 4 + 2 * ns * FFT_K1 * FFT_PITCH * LANES * 4 + (8 << 20)


def fft_filter_pass_a(hid, w_o, length, dc=256):
    q, _, d = w_o.shape
    ns = dc // LANES
    delta = jnp.abs(jnp.linspace(MIN_DECAY, MAX_DECAY, d, dtype=F32))[None]
    hid3 = hid.reshape(FFT_K1, FFT_N2, LANES)
    fa = _phase_tables(length)[0]
    return pl.pallas_call(
        functools.partial(_fft_a_filter_kernel, ns=ns, length=length),
        grid=(q, d // dc, FFT_N2 // FFT_GROUP),
        in_specs=[pl.BlockSpec((FFT_K1, FFT_GROUP, LANES), lambda w, ci, g: (0, g, 0)),
                  pl.BlockSpec((1, LANES, dc), lambda w, ci, g: (w, 0, ci)),
                  pl.BlockSpec((1, dc), lambda w, ci, g: (0, ci)),
                  pl.BlockSpec(fa.shape, lambda w, ci, g: (0, 0))],
        out_specs=[pl.BlockSpec((1, FFT_K1, 2, FFT_GROUP, dc), lambda w, ci, g: (w, 0, 0, g, ci)),
                   pl.BlockSpec((1, F32_SUBLANES, dc), lambda w, ci, g: (w, 0, ci))],
        out_shape=[jax.ShapeDtypeStruct((q, FFT_K1, 2, FFT_N2, d), BF16),
                   jax.ShapeDtypeStruct((q, F32_SUBLANES, d), F32)],
        scratch_shapes=[pltpu.VMEM((FFT_K1 * FFT_GROUP, LANES), F32)] + _fft_a_scratch(ns),
        compiler_params=_params(("parallel", "parallel", "arbitrary"), _fft_a_vmem(ns)),
        name="fft_filter_pass_a",
    )(hid3, w_o, delta, fa)


def fft_pass_a(x, col0, d, dc=256):
    q, length, c = x.shape
    ns = dc // LANES
    x4 = x.reshape(q, FFT_K1, FFT_N2, c)
    slab0 = col0 // LANES
    in_specs = [pl.BlockSpec((None, FFT_K1, FFT_GROUP, LANES),
                             functools.partial(lambda w, ci, g, s: (w, 0, g, slab0 + ci * ns + s), s=s))
                for s in range(ns)]
    fa = _phase_tables(length)[0]
    in_specs.append(pl.BlockSpec(fa.shape, lambda w, ci, g: (0, 0)))
    return pl.pallas_call(
        functools.partial(_fft_a_kernel, ns=ns),
        grid=(q, d // dc, FFT_N2 // FFT_GROUP),
        in_specs=in_specs,
        out_specs=pl.BlockSpec((1, FFT_K1, 2, FFT_GROUP, dc), lambda w, ci, g: (w, 0, 0, g, ci)),
        out_shape=jax.ShapeDtypeStruct((q, FFT_K1, 2, FFT_N2, d), BF16),
        scratch_shapes=[pltpu.VMEM((FFT_K1 * FFT_GROUP, LANES), F32)] * ns + _fft_a_scratch(ns),
        compiler_params=_params(("parallel", "parallel", "parallel"), _fft_a_vmem(ns)),
        name="fft_pass_a",
    )(*([x4] * ns), fa)


def _fft_bk_kernel(af_ref, ab_ref, gb_ref, o_ref):
    for i in range(af_ref.shape[1]):
        g = gb_ref[i]
        uf = jnp.dot(g, af_ref[0, i], preferred_element_type=F32)
        ub = jnp.dot(g, ab_ref[0, i], preferred_element_type=F32)
        n2 = FFT_N2
        o_ref[0, i, 0:n2, :] = (uf[0:n2] + ub[0:n2]).astype(o_ref.dtype)
        o_ref[0, i, n2:2 * n2, :] = (uf[n2:2 * n2] - ub[n2:2 * n2]).astype(o_ref.dtype)


def _fft_b_vmem(kb, r, dc, n_blocks, n_tables):
    return 2 * n_blocks * kb * r * dc * 2 + 2 * n_tables * kb * r * r * 2 + 6 * r * dc * 4 + (4 << 20)


def fft_filter_spectrum(a, gb, kb=8, dc=512):
    q, k1, r, d = a.shape
    blk = lambda f: pl.BlockSpec((1, kb, r, dc), f)
    return pl.pallas_call(
        _fft_bk_kernel,
        grid=(q // 2, k1 // kb, d // dc),
        in_specs=[blk(lambda o, k, c: (2 * o, k, 0, c)), blk(lambda o, k, c: (2 * o + 1, k, 0, c)),
                  pl.BlockSpec((kb, r, r), lambda o, k, c: (k, 0, 0))],
        out_specs=blk(lambda o, k, c: (o, k, 0, c)),
        out_shape=jax.ShapeDtypeStruct((q // 2, k1, r, d), BF16),
        compiler_params=_params(("parallel", "parallel", "parallel"), _fft_b_vmem(kb, r, dc, 3, 1)),
        name="fft_filter_spectrum",
    )(a, a, gb)


def _fft_b_kernel(a_ref, k_ref, gb_ref, gbi_ref, o_ref):
    n2 = FFT_N2
    for i in range(a_ref.shape[0]):
        u = jnp.dot(gb_ref[i], a_ref[i], preferred_element_type=F32)
        ur, ui = u[0:n2], u[n2:2 * n2]
        kr, ki = k_ref[0, i, 0:n2, :].astype(F32), k_ref[0, i, n2:2 * n2, :].astype(F32)
        v = jnp.concatenate([ur * kr - ui * ki, ur * ki + ui * kr], axis=0).astype(BF16)
        o_ref[i] = jnp.dot(gbi_ref[i], v, preferred_element_type=F32).astype(o_ref.dtype)


def fft_pass_b(a, kspec, order, gb, gbi, kb=8, dc=512):
    k1, r, d = a.shape
    return pl.pallas_call(
        _fft_b_kernel,
        grid=(k1 // kb, d // dc),
        in_specs=[pl.BlockSpec((kb, r, dc), lambda k, c: (k, 0, c)),
                  pl.BlockSpec((1, kb, r, dc), lambda k, c: (order, k, 0, c)),
                  pl.BlockSpec((kb, r, r), lambda k, c: (k, 0, 0)),
                  pl.BlockSpec((kb, r, r), lambda k, c: (k, 0, 0))],
        out_specs=pl.BlockSpec((kb, r, dc), lambda k, c: (k, 0, c)),
        out_shape=jax.ShapeDtypeStruct((k1, r, d), BF16),
        compiler_params=_params(("parallel", "parallel"), _fft_b_vmem(kb, r, dc, 3, 2)),
        name="fft_pass_b",
    )(a, kspec, gb, gbi)


def _fft_c_kernel(*refs, ns):
    b_ref, ca_ref, gate_refs, z_refs = refs[0], refs[1], refs[2:2 + ns], refs[2 + ns:2 + 2 * ns]
    n_ref, bias_ref, o_ref = refs[2 + 2 * ns:5 + 2 * ns]
    s_refs = refs[5 + 2 * ns:5 + 4 * ns]
    t_refs = refs[5 + 4 * ns:]
    ca = ca_ref[...]

    def spread(k, carry):
        for p in range(2):
            for s in range(ns):
                s_refs[p * ns + s][_regroup_rows(k), :] = b_ref[k, p, :, s * LANES:(s + 1) * LANES].astype(F32)
        return carry

    lax.fori_loop(0, FFT_K1, spread, 0, unroll=8)
    for j in range(FFT_GROUP):
        b = jnp.concatenate([
            jnp.concatenate([s_refs[p * ns + s][pl.ds(j, FFT_K1, stride=FFT_PITCH), :] for s in range(ns)], axis=1)
            for p in range(2)], axis=0)
        y = jnp.dot(ca, b.astype(BF16), preferred_element_type=F32)
        for s in range(ns):
            t_refs[s][pl.ds(j, FFT_K1, stride=FFT_PITCH), :] = y[:, s * LANES:(s + 1) * LANES]

    def emit(k, carry):
        for s in range(ns):
            sl = slice(s * LANES, (s + 1) * LANES)
            y = t_refs[s][_regroup_rows(k), :] / (n_ref[0:1, sl] + n_ref[1:2, sl] + 1e-6)
            o_ref[k, :, sl] = (gate_refs[s][k].astype(F32) * (y + bias_ref[:, sl] * z_refs[s][k])).astype(o_ref.dtype)
        return carry

    lax.fori_loop(0, FFT_K1, emit, 0, unroll=8)


def fft_pass_c(bp, ca, gate_arr, gate_col0, z_arr, z_col0, nsum, bias, out_dtype, dc=256):
    k1, r, d = bp.shape
    ns = dc // LANES
    length = FFT_K1 * FFT_N2
    b4 = bp.reshape(k1, 2, FFT_N2, d)
    g3 = gate_arr.reshape(FFT_K1, FFT_N2, gate_arr.shape[1])
    z3 = z_arr.reshape(FFT_K1, FFT_N2, z_arr.shape[1])
    sig = lambda slab0: [pl.BlockSpec((FFT_K1, FFT_GROUP, LANES),
                                      functools.partial(lambda ci, g, s: (0, g, slab0 + ci * ns + s), s=s))
                         for s in range(ns)]
    row = pl.BlockSpec((1, dc), lambda ci, g: (0, ci))
    blk = FFT_K1 * FFT_GROUP * dc
    vmem = 2 * 2 * blk * 2 + 2 * 3 * blk * 4 + 3 * ns * FFT_K1 * FFT_PITCH * LANES * 4 + (8 << 20)
    out = pl.pallas_call(
        functools.partial(_fft_c_kernel, ns=ns),
        grid=(d // dc, FFT_N2 // FFT_GROUP),
        in_specs=[pl.BlockSpec((k1, 2, FFT_GROUP, dc), lambda ci, g: (0, 0, g, ci)),
                  pl.BlockSpec(ca.shape, lambda ci, g: (0, 0)),
                  *sig(gate_col0 // LANES), *sig(z_col0 // LANES),
                  pl.BlockSpec((2, dc), lambda ci, g: (0, ci)), row],
        out_specs=pl.BlockSpec((FFT_K1, FFT_GROUP, dc), lambda ci, g: (0, g, ci)),
        out_shape=jax.ShapeDtypeStruct((FFT_K1, FFT_N2, d), out_dtype),
        scratch_shapes=[pltpu.VMEM((FFT_K1 * FFT_PITCH, LANES), F32)] * (3 * ns),
        compiler_params=_params(("parallel", "parallel"), vmem),
        name="fft_pass_c",
    )(b4, ca, *([g3] * ns), *([z3] * ns), nsum, bias)
    return out.reshape(length, d)


def kernel(x, c, ctx, c_ctx, ada_w, ada_b, norm_g, mix_w_in, mix_conv_w, mix_sink, mix_w_out, ffn_w_gu, ffn_w_down, hy_w_in, hy_b_in, hy_conv_w, hy_conv_b, hf_w1, hf_b1, hf_freq, hf_w2, hf_b2, hf_w3, hf_b3, hf_w_out, hf_bias, hy_w_out, hy_b_out, router_w, router_b, moe_w_gu, moe_w_down):
    assert x.shape[0] == 1 and ada_w.shape[0] == 2
    s, d = x.shape[1], x.shape[2]
    xs = x[0]
    ctxs = ctx[0]

    acts = jnp.zeros((F32_SUBLANES, d), F32).at[0].set(c[0]).at[1].set(c_ctx)
    mods = adaln_all(acts, ada_w, ada_b)
    row = lambda layer, r, k: mods[layer, r:r + 1, k * d:(k + 1) * d]

    g = norm_g[0]
    h = norm_mod(xs, g[0:1], row(0, 0, 0), row(0, 0, 1))
    hc = norm_mod(ctxs, g[0:1], row(0, 1, 0), row(0, 1, 1))
    w_in = mix_w_in[0].astype(BF16)
    p = matmul(h, w_in, out_dtype=BF16)
    ckv = matmul(hc, w_in[:, KV_START:], out_dtype=BF16)
    cs, sn = rope_tables(s)
    y = even_mixer_core(p, ckv, cs, sn, mix_conv_w[0], mix_sink[0])
    out = matmul(y, mix_w_out[0].astype(BF16))
    xs, h = post_norm(xs, out, g[1:2], row(0, 0, 2), g[2:3], row(0, 0, 3), row(0, 0, 4))

    act = swiglu_up(h, ffn_w_gu[0].astype(BF16)[None], tm=2048, tn=256)
    out = matmul(act[0], ffn_w_down[0].astype(BF16), tm=512, tn=256)
    g1 = norm_g[1]
    xs, h = post_norm(xs, out, g[3:4], row(0, 0, 5), g1[0:1], row(1, 0, 0), row(1, 0, 1))

    g = g1
    u0 = matmul(h, hy_w_in[0].astype(BF16), bias=hy_b_in[0][None], out_dtype=BF16)
    x12 = short_conv(u0, hy_conv_w[0], hy_conv_b[0][None], 0, 2 * d, BF16)
    v = short_conv(u0, hy_conv_w[0], hy_conv_b[0][None], 2 * d, d, F32)
    hid = filter_hidden(s, hf_w1[0], hf_b1[0], hf_freq[0], hf_w2[0], hf_b2[0], hf_w3[0], hf_b3[0])
    w_o = hf_w_out[0].astype(F32).reshape(FILTER_HIDDEN, HYENA_ORDER * 2, d).transpose(1, 0, 2)
    w_o = jnp.zeros((HYENA_ORDER * 2, LANES, d), F32).at[:, :FILTER_HIDDEN].set(w_o)
    filt_a, fsum = fft_filter_pass_a(hid, w_o, s)
    fsum = fsum[:, 0, :]
    _, ca, gb, gbi = _phase_tables(s)
    spec = lambda a: a.reshape(a.shape[0], FFT_K1, 2 * FFT_N2, d)
    kspec = fft_filter_spectrum(spec(filt_a), gb)
    a = spec(fft_pass_a(v[None], 0, d))[0]
    z1 = fft_pass_c(fft_pass_b(a, kspec, 0, gb, gbi), ca, x12, 0, v, 0, fsum[0:2], hf_bias[0, 0][None], F32)
    a = spec(fft_pass_a(z1[None], 0, d))[0]
    z2 = fft_pass_c(fft_pass_b(a, kspec, 1, gb, gbi), ca, x12, d, z1, 0, fsum[2:4], hf_bias[0, 1][None], BF16)
    out = matmul(z2, hy_w_out[0].astype(BF16), bias=hy_b_out[0][None])
    xs = post_residual(xs, out, g[1:2], row(1, 0, 2))

    sel, wts = router_top2(xs, g[2:3], row(1, 0, 3), row(1, 0, 4), router_w[0], router_b[0])
    pos, src_tok, tile_expert, n_valid = moe_plan(sel)
    hg = moe_gather_norm(xs, src_tok, g[2:3], row(1, 0, 3), row(1, 0, 4))
    act = grouped_swiglu_up(hg, moe_w_gu[0].astype(BF16), tile_expert, n_valid)
    ys = grouped_down(act, moe_w_down[0].astype(BF16), tile_expert, n_valid)
    xs = moe_combine_residual(xs, wts, ys, pos, g[3:4], row(1, 0, 5))
    return xs[None]
```

```python
import functools
import math

import jax
import jax.numpy as jnp
from jax import lax
from jax.experimental import pallas as pl
from jax.experimental.pallas import tpu as pltpu

F32 = jnp.float32
BF16 = jnp.bfloat16

D_MODEL = 4096
GRID_W = 64
HEAD_DIM = 128
CONV_WIDTH = D_MODEL // 2
N_HEADS = (D_MODEL // 2) // HEAD_DIM
N_KV_HEADS = N_HEADS // 4
GQA_GROUP = N_HEADS // N_KV_HEADS
Q_WIDTH = N_HEADS * HEAD_DIM
KV_WIDTH = N_KV_HEADS * HEAD_DIM
Q_START = 3 * CONV_WIDTH
KV_START = Q_START + Q_WIDTH
IN_WIDTH = KV_START + 2 * KV_WIDTH
BLOCK = 128
ATTN_SCALE = HEAD_DIM ** -0.5
ROPE_BASE = 10000.0
ROPE_PAIRS = HEAD_DIM // 4
HYENA_ORDER = 2
FILTER_EMB = 33
FILTER_BANDS = (FILTER_EMB - 1) // 2
FILTER_HIDDEN = 64
MIN_DECAY = math.log(1e-2) / 0.3
MAX_DECAY = math.log(1e-2) / 1.5
N_EXPERTS = 8
NORM_EPS = 1e-6

V7X_VMEM_BYTES = 64 * 1024 * 1024
LANES = 128
F32_SUBLANES = 8
BF16_SUBLANES = 16


def _params(semantics, vmem_bytes):
    limit = min(int(vmem_bytes), V7X_VMEM_BYTES - 4 * 1024 * 1024)
    return pltpu.CompilerParams(dimension_semantics=semantics, vmem_limit_bytes=limit)


def _sigmoid(v):
    return 1.0 / (1.0 + jnp.exp(-v))


def _adaln_kernel(a_ref, w_ref, b_ref, o_ref, acc_ref):
    k = pl.program_id(2)

    @pl.when(k == 0)
    def _():
        acc_ref[...] = jnp.zeros_like(acc_ref)

    a = a_ref[...]
    a = a * _sigmoid(a)
    acc_ref[...] += jnp.dot(a.astype(BF16), w_ref[0].astype(BF16), preferred_element_type=F32)

    @pl.when(k == pl.num_programs(2) - 1)
    def _():
        o_ref[0] = acc_ref[...] + b_ref[0]


def adaln_all(acts, ada_w, ada_b):
    depth, d, n = ada_w.shape
    tn, tk = 2048, 1024
    return pl.pallas_call(
        _adaln_kernel,
        grid=(depth, n // tn, d // tk),
        in_specs=[
            pl.BlockSpec((F32_SUBLANES, tk), lambda l, j, k: (0, k)),
            pl.BlockSpec((1, tk, tn), lambda l, j, k: (l, k, j)),
            pl.BlockSpec((1, 1, tn), lambda l, j, k: (l, 0, j)),
        ],
        out_specs=pl.BlockSpec((1, F32_SUBLANES, tn), lambda l, j, k: (l, 0, j)),
        out_shape=jax.ShapeDtypeStruct((depth, F32_SUBLANES, n), F32),
        scratch_shapes=[pltpu.VMEM((F32_SUBLANES, tn), F32)],
        compiler_params=_params(("parallel", "parallel", "arbitrary"), 2 * tk * tn * 4 + tk * tn * 2 + (4 << 20)),
        name="adaln",
    )(acts, ada_w, ada_b.reshape(depth, 1, n))


def _norm_mod(x, g, sh, sc):
    y = x * lax.rsqrt(jnp.mean(x * x, axis=-1, keepdims=True) + NORM_EPS)
    return (y * g) * (1.0 + sc) + sh


def _norm_mod_kernel(x_ref, g_ref, sh_ref, sc_ref, o_ref):
    o_ref[...] = _norm_mod(x_ref[...], g_ref[...], sh_ref[...], sc_ref[...]).astype(o_ref.dtype)


def norm_mod(x, g, shift, scale, tm=512):
    m, d = x.shape
    tm = min(tm, m)
    row = pl.BlockSpec((1, d), lambda i: (0, 0))
    return pl.pallas_call(
        _norm_mod_kernel,
        grid=(m // tm,),
        in_specs=[pl.BlockSpec((tm, d), lambda i: (i, 0)), row, row, row],
        out_specs=pl.BlockSpec((tm, d), lambda i: (i, 0)),
        out_shape=jax.ShapeDtypeStruct((m, d), BF16),
        compiler_params=_params(("parallel",), 2 * tm * d * 4 + 2 * tm * d * 2 + 3 * tm * d * 4 + (2 << 20)),
        name="norm_mod",
    )(x, g, shift, scale)


def _post_kernel(x_ref, y_ref, g_ref, gt_ref, o_ref):
    y = y_ref[...]
    yn = y * lax.rsqrt(jnp.mean(y * y, axis=-1, keepdims=True) + NORM_EPS)
    o_ref[...] = x_ref[...] + gt_ref[...] * (yn * g_ref[...])


def post_residual(x, y, g, gate, tm=512):
    m, d = x.shape
    row = pl.BlockSpec((1, d), lambda i: (0, 0))
    blk = pl.BlockSpec((tm, d), lambda i: (i, 0))
    return pl.pallas_call(
        _post_kernel,
        grid=(m // tm,),
        in_specs=[blk, blk, row, row],
        out_specs=blk,
        out_shape=jax.ShapeDtypeStruct((m, d), F32),
        compiler_params=_params(("parallel",), 6 * tm * d * 4 + 3 * tm * d * 4 + (2 << 20)),
        name="post_residual",
    )(x, y, g, gate)


def _post_norm_kernel(x_ref, y_ref, g_ref, gt_ref, g2_ref, sh_ref, sc_ref, o_ref, h_ref):
    y = y_ref[...]
    yn = y * lax.rsqrt(jnp.mean(y * y, axis=-1, keepdims=True) + NORM_EPS)
    x = x_ref[...] + gt_ref[...] * (yn * g_ref[...])
    o_ref[...] = x
    h_ref[...] = _norm_mod(x, g2_ref[...], sh_ref[...], sc_ref[...]).astype(h_ref.dtype)


def post_norm(x, y, g, gate, g_next, shift, scale, tm=256):
    m, d = x.shape
    row = pl.BlockSpec((1, d), lambda i: (0, 0))
    blk = pl.BlockSpec((tm, d), lambda i: (i, 0))
    return pl.pallas_call(
        _post_norm_kernel,
        grid=(m // tm,),
        in_specs=[blk, blk, row, row, row, row, row],
        out_specs=[blk, blk],
        out_shape=[jax.ShapeDtypeStruct((m, d), F32), jax.ShapeDtypeStruct((m, d), BF16)],
        compiler_params=_params(("parallel",), 7 * tm * d * 4 + 5 * tm * d * 4 + (2 << 20)),
        name="post_norm",
    )(x, y, g, gate, g_next, shift, scale)


def _side_cast(src_ref, dst_ref, step, n_chunks):
    @pl.when(step < n_chunks)
    def _():
        dst_ref[...] = src_ref[...].astype(dst_ref.dtype)


def _side_cast_plumbing(side, n_chunks, step_of):
    rows, cols = side.shape
    spec = pl.BlockSpec((rows // n_chunks, cols), lambda *g: (jnp.minimum(step_of(*g), n_chunks - 1), 0))
    vmem = 2 * (rows // n_chunks) * cols * (4 + 2)
    return spec, jax.ShapeDtypeStruct((rows, cols), BF16), vmem


def _mm_kernel(*refs, has_bias, side_chunks):
    a_ref, w_ref = refs[0], refs[1]
    n_in = 2 + has_bias + (side_chunks > 0)
    o_ref = refs[n_in]
    acc = jnp.dot(a_ref[...], w_ref[...], preferred_element_type=F32)
    if has_bias:
        acc = acc + refs[2][...]
    o_ref[...] = acc.astype(o_ref.dtype)
    if side_chunks:
        step = pl.program_id(0) * pl.num_programs(1) + pl.program_id(1)
        _side_cast(refs[n_in - 1], refs[n_in + 1], step, side_chunks)


def matmul(a, w, bias=None, out_dtype=F32, tm=1024, tn=512, side=None, side_chunks=0):
    m, k = a.shape
    n = w.shape[1]
    tm, tn = min(tm, m), min(tn, n)
    nj = n // tn
    in_specs = [pl.BlockSpec((tm, k), lambda i, j: (i, 0)), pl.BlockSpec((k, tn), lambda i, j: (0, j))]
    args = [a, w]
    if bias is not None:
        in_specs.append(pl.BlockSpec((1, tn), lambda i, j: (0, j)))
        args.append(bias)
    osz = jnp.dtype(out_dtype).itemsize
    vmem = 2 * tm * k * 2 + 2 * k * tn * 2 + 2 * tm * tn * osz + 2 * tm * tn * 4 + (2 << 20)
    out_specs = pl.BlockSpec((tm, tn), lambda i, j: (i, j))
    out_shape = jax.ShapeDtypeStruct((m, n), out_dtype)
    if side is not None:
        assert side_chunks <= (m // tm) * nj
        spec, shape, side_vmem = _side_cast_plumbing(side, side_chunks, lambda i, j: i * nj + j)
        in_specs.append(spec)
        args.append(side)
        out_specs, out_shape, vmem = [out_specs, spec], [out_shape, shape], vmem + side_vmem
    return pl.pallas_call(
        functools.partial(_mm_kernel, has_bias=bias is not None, side_chunks=side_chunks),
        grid=(m // tm, nj),
        in_specs=in_specs,
        out_specs=out_specs,
        out_shape=out_shape,
        compiler_params=_params(("arbitrary", "arbitrary") if side is not None else ("parallel", "arbitrary"), vmem),
        name="matmul",
    )(*args)


def _swiglu_up_kernel(a_ref, wg_ref, wu_ref, side_ref, o_ref, side_o_ref, *, side_chunks):
    a = a_ref[...]
    g = jnp.dot(a, wg_ref[...], preferred_element_type=F32)
    u = jnp.dot(a, wu_ref[...], preferred_element_type=F32)
    o_ref[...] = (g * _sigmoid(g) * u).astype(o_ref.dtype)
    _side_cast(side_ref, side_o_ref, pl.program_id(0) * pl.num_programs(1) + pl.program_id(1), side_chunks)


def swiglu_up(a, w_gu, tm, tn, side, side_chunks):
    m, k = a.shape
    f = w_gu.shape[1] // 2
    nj = f // tn
    assert side_chunks <= (m // tm) * nj
    spec, shape, side_vmem = _side_cast_plumbing(side, side_chunks, lambda i, j: i * nj + j)
    vmem = 2 * tm * k * 2 + 4 * k * tn * 2 + 2 * tm * tn * 2 + 4 * tm * tn * 4 + side_vmem + (2 << 20)
    return pl.pallas_call(
        functools.partial(_swiglu_up_kernel, side_chunks=side_chunks),
        grid=(m // tm, nj),
        in_specs=[
            pl.BlockSpec((tm, k), lambda i, j: (i, 0)),
            pl.BlockSpec((k, tn), lambda i, j: (0, j)),
            pl.BlockSpec((k, tn), lambda i, j: (0, j + nj)),
            spec,
        ],
        out_specs=[pl.BlockSpec((tm, tn), lambda i, j: (i, j)), spec],
        out_shape=[jax.ShapeDtypeStruct((m, f), BF16), shape],
        compiler_params=_params(("arbitrary", "arbitrary"), vmem),
        name="swiglu_up",
    )(a, w_gu, w_gu, side)


def _router_kernel(x_ref, g_ref, sh_ref, sc_ref, w_ref, b_ref, sel_ref, wts_ref):
    h = _norm_mod(x_ref[...], g_ref[...], sh_ref[...], sc_ref[...])
    logits = jnp.dot(h, w_ref[...], preferred_element_type=F32, precision=lax.Precision.HIGHEST) + b_ref[...]
    lane = lax.broadcasted_iota(jnp.int32, logits.shape, 1)
    neg = jnp.float32(-jnp.inf)
    logits = jnp.where(lane < N_EXPERTS, logits, neg)
    v1 = jnp.max(logits, axis=1, keepdims=True)
    i1 = jnp.min(jnp.where(logits == v1, lane, LANES), axis=1, keepdims=True)
    rest = jnp.where(lane == i1, neg, logits)
    v2 = jnp.max(rest, axis=1, keepdims=True)
    i2 = jnp.min(jnp.where(rest == v2, lane, LANES), axis=1, keepdims=True)
    e2 = jnp.exp(v2 - v1)
    w1 = 1.0 / (1.0 + e2)
    w2 = e2 / (1.0 + e2)
    sel_ref[...] = jnp.where(lane == 0, i1, jnp.where(lane == 1, i2, 0))
    wts_ref[...] = jnp.where(lane == 0, w1, jnp.where(lane == 1, w2, 0.0))


def router_top2(x, g, shift, scale, router_w, router_b, tm=256):
    m, d = x.shape
    wp = jnp.zeros((d, LANES), F32).at[:, :N_EXPERTS].set(router_w)
    bp = jnp.zeros((1, LANES), F32).at[0, :N_EXPERTS].set(router_b)
    row = pl.BlockSpec((1, d), lambda i: (0, 0))
    out = pl.BlockSpec((tm, LANES), lambda i: (i, 0))
    return pl.pallas_call(
        _router_kernel,
        grid=(m // tm,),
        in_specs=[pl.BlockSpec((tm, d), lambda i: (i, 0)), row, row, row,
                  pl.BlockSpec((d, LANES), lambda i: (0, 0)), pl.BlockSpec((1, LANES), lambda i: (0, 0))],
        out_specs=[out, out],
        out_shape=[jax.ShapeDtypeStruct((m, LANES), jnp.int32), jax.ShapeDtypeStruct((m, LANES), F32)],
        compiler_params=_params(("parallel",), 2 * tm * d * 4 + 6 * tm * d * 4 + 2 * d * LANES * 4 + (4 << 20)),
        name="router",
    )(x, g, shift, scale, wp, bp)


MOE_TILE = 512


def moe_plan(sel):
    t = sel.shape[0]
    npairs = 2 * t
    rows = npairs + N_EXPERTS * MOE_TILE
    e_flat = sel[:, :2].reshape(npairs)
    order = jnp.argsort(e_flat, stable=True).astype(jnp.int32)
    rank = jnp.argsort(order).astype(jnp.int32)
    counts = jnp.sum((e_flat[:, None] == jnp.arange(N_EXPERTS, dtype=jnp.int32)[None, :]).astype(jnp.int32), axis=0)
    padded = (counts + MOE_TILE - 1) // MOE_TILE * MOE_TILE
    ends_p = jnp.cumsum(padded)
    starts_p = ends_p - padded
    starts_u = jnp.cumsum(counts) - counts
    pos = starts_p[e_flat] + rank - starts_u[e_flat]
    tile_start = jnp.arange(rows // MOE_TILE, dtype=jnp.int32) * MOE_TILE
    tile_expert = jnp.minimum(jnp.searchsorted(ends_p, tile_start, side="right"), N_EXPERTS - 1).astype(jnp.int32)
    n_valid = (ends_p[-1] // MOE_TILE).astype(jnp.int32).reshape(1)
    r = jnp.arange(rows, dtype=jnp.int32)
    e_row = jnp.repeat(tile_expert, MOE_TILE)
    in_group = r - starts_p[e_row]
    valid = (in_group < counts[e_row]) & (r < ends_p[-1])
    src_tok = jnp.where(valid, order[jnp.clip(starts_u[e_row] + in_group, 0, npairs - 1)] // 2, 0)
    return pos.astype(jnp.int32), src_tok.astype(jnp.int32), tile_expert, n_valid


MOE_GATHER_CHUNKS = 4


def _gather_rows(idx_ref, base, stride, src_hbm, dst_ref, sem, r0, n, wait):
    def body(r, carry):
        copy = pltpu.make_async_copy(src_hbm.at[pl.ds(idx_ref[base + stride * r], 1)], dst_ref.at[pl.ds(r, 1)], sem)
        if wait:
            copy.wait()
        else:
            copy.start()
        return carry

    lax.fori_loop(r0, r0 + n, body, 0, unroll=8)


def _moe_gather_kernel(src_ref, x_hbm, g_ref, sh_ref, sc_ref, o_ref, buf_ref, sems):
    tg = buf_ref.shape[0]
    rc = tg // MOE_GATHER_CHUNKS
    base = pl.program_id(0) * tg
    for c in range(MOE_GATHER_CHUNKS):
        _gather_rows(src_ref, base, 1, x_hbm, buf_ref, sems.at[c], c * rc, rc, wait=False)
    for c in range(MOE_GATHER_CHUNKS):
        _gather_rows(src_ref, base, 1, x_hbm, buf_ref, sems.at[c], c * rc, rc, wait=True)
        rows = pl.ds(c * rc, rc)
        o_ref[rows, :] = _norm_mod(buf_ref[rows, :], g_ref[...], sh_ref[...], sc_ref[...]).astype(o_ref.dtype)


def moe_gather_norm(x, src_tok, g, shift, scale, tg=256):
    d = x.shape[1]
    rows = src_tok.shape[0]
    row = pl.BlockSpec((1, d), lambda i, src: (0, 0))
    return pl.pallas_call(
        _moe_gather_kernel,
        grid_spec=pltpu.PrefetchScalarGridSpec(
            num_scalar_prefetch=1,
            grid=(rows // tg,),
            in_specs=[pl.BlockSpec(memory_space=pl.ANY), row, row, row],
            out_specs=pl.BlockSpec((tg, d), lambda i, src: (i, 0)),
            scratch_shapes=[pltpu.VMEM((tg, d), F32), pltpu.SemaphoreType.DMA((MOE_GATHER_CHUNKS,))],
        ),
        out_shape=jax.ShapeDtypeStruct((rows, d), BF16),
        compiler_params=_params(("arbitrary",), 6 * tg * d * 4 + (2 << 20)),
        name="moe_gather_norm",
    )(src_tok, x, g, shift, scale)


def _gswiglu_kernel(te_ref, nv_ref, a_ref, wg_ref, wu_ref, o_ref):
    valid = pl.program_id(0) < nv_ref[0]

    @pl.when(valid)
    def _():
        a = a_ref[...]
        g = jnp.dot(a, wg_ref[0], preferred_element_type=F32)
        u = jnp.dot(a, wu_ref[0], preferred_element_type=F32)
        o_ref[...] = (g * _sigmoid(g) * u).astype(o_ref.dtype)

    @pl.when(jnp.logical_not(valid))
    def _():
        o_ref[...] = jnp.zeros_like(o_ref)


def grouped_swiglu_up(a, w_gu, tile_expert, n_valid, tn=512):
    rows, k = a.shape
    f = w_gu.shape[2] // 2
    nj = f // tn
    tm = MOE_TILE
    vmem = 2 * tm * k * 2 + 4 * k * tn * 2 + 2 * tm * tn * 2 + 4 * tm * tn * 4 + (2 << 20)
    return pl.pallas_call(
        _gswiglu_kernel,
        grid_spec=pltpu.PrefetchScalarGridSpec(
            num_scalar_prefetch=2,
            grid=(rows // tm, nj),
            in_specs=[
                pl.BlockSpec((tm, k), lambda i, j, te, nv: (i, 0)),
                pl.BlockSpec((1, k, tn), lambda i, j, te, nv: (te[i], 0, j)),
                pl.BlockSpec((1, k, tn), lambda i, j, te, nv: (te[i], 0, j + nj)),
            ],
            out_specs=pl.BlockSpec((tm, tn), lambda i, j, te, nv: (i, j)),
        ),
        out_shape=jax.ShapeDtypeStruct((rows, f), BF16),
        compiler_params=_params(("parallel", "arbitrary"), vmem),
        name="grouped_swiglu_up",
    )(tile_expert, n_valid, a, w_gu, w_gu)


def _gdown_kernel(te_ref, nv_ref, a_ref, w_ref, o_ref):
    valid = pl.program_id(0) < nv_ref[0]

    @pl.when(valid)
    def _():
        o_ref[...] = jnp.dot(a_ref[...], w_ref[0], preferred_element_type=F32)

    @pl.when(jnp.logical_not(valid))
    def _():
        o_ref[...] = jnp.zeros_like(o_ref)


def grouped_down(act, w_down, tile_expert, n_valid, tn=1024):
    rows, f = act.shape
    n = w_down.shape[2]
    tm = MOE_TILE
    vmem = 2 * tm * f * 2 + 2 * f * tn * 2 + 2 * tm * tn * 4 + 2 * tm * tn * 4 + (2 << 20)
    return pl.pallas_call(
        _gdown_kernel,
        grid_spec=pltpu.PrefetchScalarGridSpec(
            num_scalar_prefetch=2,
            grid=(rows // tm, n // tn),
            in_specs=[
                pl.BlockSpec((tm, f), lambda i, j, te, nv: (i, 0)),
                pl.BlockSpec((1, f, tn), lambda i, j, te, nv: (te[i], 0, j)),
            ],
            out_specs=pl.BlockSpec((tm, tn), lambda i, j, te, nv: (i, j)),
        ),
        out_shape=jax.ShapeDtypeStruct((rows, n), F32),
        compiler_params=_params(("parallel", "arbitrary"), vmem),
        name="grouped_down",
    )(tile_expert, n_valid, act, w_down)


def _moe_combine_kernel(pos_ref, x_ref, w_ref, ys_hbm, g_ref, gt_ref, o_ref, buf_ref, sems):
    tm = x_ref.shape[0]
    rc = tm // MOE_GATHER_CHUNKS
    base = 2 * pl.program_id(0) * tm
    for wait in (False, True):
        for c in range(MOE_GATHER_CHUNKS):
            for slot in range(2):
                _gather_rows(pos_ref, base + slot, 2, ys_hbm, buf_ref.at[slot], sems.at[c], c * rc, rc, wait=wait)
            if wait:
                rows = pl.ds(c * rc, rc)
                w = w_ref[rows, :]
                y = w[:, 0:1] * buf_ref[0, rows, :] + w[:, 1:2] * buf_ref[1, rows, :]
                yn = y * lax.rsqrt(jnp.mean(y * y, axis=-1, keepdims=True) + NORM_EPS)
                o_ref[rows, :] = x_ref[rows, :] + gt_ref[...] * (yn * g_ref[...])


def moe_combine_residual(x, wts, ys, pos, g, gate, tm=128):
    t, d = x.shape
    row = pl.BlockSpec((1, d), lambda i, p: (0, 0))
    return pl.pallas_call(
        _moe_combine_kernel,
        grid_spec=pltpu.PrefetchScalarGridSpec(
            num_scalar_prefetch=1,
            grid=(t // tm,),
            in_specs=[pl.BlockSpec((tm, d), lambda i, p: (i, 0)), pl.BlockSpec((tm, LANES), lambda i, p: (i, 0)),
                      pl.BlockSpec(memory_space=pl.ANY), row, row],
            out_specs=pl.BlockSpec((tm, d), lambda i, p: (i, 0)),
            scratch_shapes=[pltpu.VMEM((2, tm, d), F32), pltpu.SemaphoreType.DMA((MOE_GATHER_CHUNKS,))],
        ),
        out_shape=jax.ShapeDtypeStruct((t, d), F32),
        compiler_params=_params(("arbitrary",), 10 * tm * d * 4 + (2 << 20)),
        name="moe_combine_residual",
    )(pos, x, wts, ys, g, gate)


def _rope(x, cs, sn):
    lane = lax.broadcasted_iota(jnp.int32, x.shape, 1)
    first = (lane % (2 * ROPE_PAIRS)) < ROPE_PAIRS
    partner = jnp.where(first, pltpu.roll(x, HEAD_DIM - ROPE_PAIRS, 1), pltpu.roll(x, ROPE_PAIRS, 1))
    return x * cs + partner * sn


def _shift_rows(u, halo_prev, halo_next):
    n = u.shape[0]
    row = lax.broadcasted_iota(jnp.int32, u.shape, 0)
    prev = jnp.where(row == 0, halo_prev, pltpu.roll(u, 1, 0))
    nxt = jnp.where(row == n - 1, halo_next, pltpu.roll(u, n - 1, 0))
    return prev, nxt


def _mixer_kernel(sink_ref, p_ref, kvp_ref, kvn_ref, cp_ref, hp_ref, cn_ref, hn_ref, ckv_ref,
                  cs_ref, sn_ref, csp_ref, snp_ref, csn_ref, snn_ref, cw_ref, o_ref):
    i = pl.program_id(0)
    nb = pl.num_programs(0)
    cs, sn = cs_ref[...], sn_ref[...]

    a_b = p_ref[:, 0:CONV_WIDTH].astype(F32)
    cu = p_ref[:, CONV_WIDTH:2 * CONV_WIDTH].astype(F32) * p_ref[:, 2 * CONV_WIDTH:3 * CONV_WIDTH].astype(F32)
    last = BF16_SUBLANES - 1
    halo_p = cp_ref[last:last + 1, :].astype(F32) * hp_ref[last:last + 1, :].astype(F32)
    halo_n = cn_ref[0:1, :].astype(F32) * hn_ref[0:1, :].astype(F32)
    halo_p = jnp.where(i > 0, halo_p, 0.0)
    halo_n = jnp.where(i < nb - 1, halo_n, 0.0)
    cu_prev, cu_next = _shift_rows(cu, halo_p, halo_n)
    y_conv = a_b * (cu_prev * cw_ref[0:1, :] + cu * cw_ref[1:2, :] + cu_next * cw_ref[2:3, :])
    o_ref[:, 0:CONV_WIDTH] = y_conv.astype(o_ref.dtype)

    rows = GQA_GROUP * BLOCK
    qi = lax.broadcasted_iota(jnp.int32, (rows, 3 * BLOCK), 0) % BLOCK
    kj = lax.broadcasted_iota(jnp.int32, (rows, 3 * BLOCK), 1)
    key_pos = i * BLOCK + kj - BLOCK
    mask = (jnp.abs(qi + BLOCK - kj) <= BLOCK) & (key_pos >= 0) & (key_pos < nb * BLOCK)
    nt = (((1,), (1,)), ((), ()))

    for kh in range(N_KV_HEADS):
        ko = KV_START + kh * HEAD_DIM
        vo = KV_START + KV_WIDTH + kh * HEAD_DIM
        kb = kh * HEAD_DIM
        vb = KV_WIDTH + kh * HEAD_DIM
        k_band = jnp.concatenate([
            _rope(kvp_ref[:, kb:kb + HEAD_DIM].astype(F32), csp_ref[...], snp_ref[...]),
            _rope(p_ref[:, ko:ko + HEAD_DIM].astype(F32), cs, sn),
            _rope(kvn_ref[:, kb:kb + HEAD_DIM].astype(F32), csn_ref[...], snn_ref[...]),
        ], axis=0).astype(BF16)
        v_band = jnp.concatenate([kvp_ref[:, vb:vb + HEAD_DIM], p_ref[:, vo:vo + HEAD_DIM],
                                  kvn_ref[:, vb:vb + HEAD_DIM]], axis=0)
        k_ctx = ckv_ref[:, kb:kb + HEAD_DIM]
        v_ctx = ckv_ref[:, vb:vb + HEAD_DIM]
        qs, sinks = [], []
        for g in range(GQA_GROUP):
            h = kh * GQA_GROUP + g
            qo = Q_START + h * HEAD_DIM
            qs.append(_rope(p_ref[:, qo:qo + HEAD_DIM].astype(F32), cs, sn).astype(BF16))
            sinks.append(jnp.full((BLOCK, 1), sink_ref[h], F32))
        q = jnp.concatenate(qs, axis=0)
        sink = jnp.concatenate(sinks, axis=0)
        s_loc = lax.dot_general(q, k_band, nt, preferred_element_type=F32) * ATTN_SCALE
        s_loc = jnp.where(mask, s_loc, -jnp.inf)
        s_ctx = lax.dot_general(q, k_ctx, nt, preferred_element_type=F32) * ATTN_SCALE
        m = jnp.maximum(sink, jnp.maximum(jnp.max(s_loc, axis=1, keepdims=True),
                                          jnp.max(s_ctx, axis=1, keepdims=True)))
        e_loc = jnp.exp(s_loc - m)
        e_ctx = jnp.exp(s_ctx - m)
        denom = jnp.exp(sink - m) + jnp.sum(e_loc, axis=1, keepdims=True) + jnp.sum(e_ctx, axis=1, keepdims=True)
        o = (jnp.dot(e_loc.astype(BF16), v_band, preferred_element_type=F32)
             + jnp.dot(e_ctx.astype(BF16), v_ctx, preferred_element_type=F32)) / denom
        for g in range(GQA_GROUP):
            h = kh * GQA_GROUP + g
            oo = CONV_WIDTH + h * HEAD_DIM
            o_ref[:, oo:oo + HEAD_DIM] = o[g * BLOCK:(g + 1) * BLOCK, :].astype(o_ref.dtype)


def even_mixer_core(p, ckv, cs, sn, conv_w, sink):
    s = p.shape[0]
    nb = s // BLOCK
    hb = BLOCK // BF16_SUBLANES
    kvc = KV_START // (2 * KV_WIDTH)
    prev = lambda i: jnp.maximum(i - 1, 0)
    nxt = lambda i: jnp.minimum(i + 1, nb - 1)
    tab = lambda f: pl.BlockSpec((BLOCK, HEAD_DIM), lambda i: (f(i), 0))
    same = lambda i: i
    in_specs = [
        pl.BlockSpec(memory_space=pltpu.SMEM),
        pl.BlockSpec((BLOCK, IN_WIDTH), lambda i: (i, 0)),
        pl.BlockSpec((BLOCK, 2 * KV_WIDTH), lambda i: (prev(i), kvc)),
        pl.BlockSpec((BLOCK, 2 * KV_WIDTH), lambda i: (nxt(i), kvc)),
        pl.BlockSpec((BF16_SUBLANES, CONV_WIDTH), lambda i: (jnp.maximum(i * hb - 1, 0), 1)),
        pl.BlockSpec((BF16_SUBLANES, CONV_WIDTH), lambda i: (jnp.maximum(i * hb - 1, 0), 2)),
        pl.BlockSpec((BF16_SUBLANES, CONV_WIDTH), lambda i: (jnp.minimum((i + 1) * hb, nb * hb - 1), 1)),
        pl.BlockSpec((BF16_SUBLANES, CONV_WIDTH), lambda i: (jnp.minimum((i + 1) * hb, nb * hb - 1), 2)),
        pl.BlockSpec(ckv.shape, lambda i: (0, 0)),
        tab(same), tab(same), tab(prev), tab(prev), tab(nxt), tab(nxt),
        pl.BlockSpec(conv_w.shape, lambda i: (0, 0)),
    ]
    return pl.pallas_call(
        _mixer_kernel,
        grid=(nb,),
        in_specs=in_specs,
        out_specs=pl.BlockSpec((BLOCK, CONV_WIDTH + Q_WIDTH), lambda i: (i, 0)),
        out_shape=jax.ShapeDtypeStruct((s, CONV_WIDTH + Q_WIDTH), BF16),
        compiler_params=_params(("parallel",), 32 << 20),
        name="even_mixer_core",
    )(sink, p, p, p, p, p, p, p, ckv, cs, sn, cs, sn, cs, sn, conv_w)


def rope_tables(s):
    t = jnp.arange(s, dtype=jnp.int32)
    row = (t // GRID_W).astype(F32)
    col = (t % GRID_W).astype(F32)
    inv = ROPE_BASE ** (-jnp.arange(ROPE_PAIRS, dtype=F32) / ROPE_PAIRS)
    ang_r = row[:, None] * inv[None, :]
    ang_c = col[:, None] * inv[None, :]
    cs = jnp.concatenate([jnp.cos(ang_r)] * 2 + [jnp.cos(ang_c)] * 2, axis=1)
    sn = jnp.concatenate([-jnp.sin(ang_r), jnp.sin(ang_r), -jnp.sin(ang_c), jnp.sin(ang_c)], axis=1)
    return cs, sn


def _sconv_kernel(u_ref, up_ref, un_ref, w_ref, b_ref, o_ref):
    i = pl.program_id(0)
    u = u_ref[...].astype(F32)
    last = up_ref.shape[0] - 1
    halo_p = jnp.where(i > 0, up_ref[last:last + 1, :].astype(F32), 0.0)
    halo_n = jnp.where(i < pl.num_programs(0) - 1, un_ref[0:1, :].astype(F32), 0.0)
    prev, nxt = _shift_rows(u, halo_p, halo_n)
    o_ref[...] = (prev * w_ref[0:1, :] + u * w_ref[1:2, :] + nxt * w_ref[2:3, :] + b_ref[...]).astype(o_ref.dtype)


def short_conv(u, w, b, col0, width, out_dtype, tm=256, tn=2048):
    s = u.shape[0]
    hr = BF16_SUBLANES
    hb = tm // hr
    nh = s // hr
    j0 = col0 // tn
    return pl.pallas_call(
        _sconv_kernel,
        grid=(s // tm, width // tn),
        in_specs=[
            pl.BlockSpec((tm, tn), lambda i, j: (i, j + j0)),
            pl.BlockSpec((hr, tn), lambda i, j: (jnp.maximum(i * hb - 1, 0), j + j0)),
            pl.BlockSpec((hr, tn), lambda i, j: (jnp.minimum((i + 1) * hb, nh - 1), j + j0)),
            pl.BlockSpec((3, tn), lambda i, j: (0, j + j0)),
            pl.BlockSpec((1, tn), lambda i, j: (0, j + j0)),
        ],
        out_specs=pl.BlockSpec((tm, tn), lambda i, j: (i, j)),
        out_shape=jax.ShapeDtypeStruct((s, width), out_dtype),
        compiler_params=_params(("parallel", "parallel"), 10 * tm * tn * 4 + (2 << 20)),
        name="short_conv",
    )(u, u, u, w, b)


def _hid_kernel(z_ref, w1_ref, b1_ref, w2_ref, b2_ref, w3_ref, b3_ref, fr_ref, o_ref):
    hp = lax.Precision.HIGHEST
    fr = fr_ref[...]
    h = jnp.sin(fr * (jnp.dot(z_ref[...], w1_ref[...], preferred_element_type=F32, precision=hp) + b1_ref[...]))
    h = jnp.sin(fr * (jnp.dot(h, w2_ref[...], preferred_element_type=F32, precision=hp) + b2_ref[...]))
    o_ref[...] = jnp.sin(fr * (jnp.dot(h, w3_ref[...], preferred_element_type=F32, precision=hp) + b3_ref[...]))


def _pad2(a, r, c):
    return jnp.zeros((r, c), F32).at[:a.shape[0], :a.shape[1]].set(a.astype(F32))


def filter_hidden(length, w1, b1, freq, w2, b2, w3, b3, tl=2048):
    t = jnp.linspace(0.0, 1.0, length, dtype=F32)[:, None]
    w = (2.0 * math.pi / length) * jnp.arange(length, dtype=F32)[:, None]
    bands = jnp.linspace(1e-4, FILTER_BANDS - 1, FILTER_BANDS, dtype=F32)[None]
    z = jnp.concatenate([t, jnp.cos(bands * w), -jnp.sin(bands * w)], axis=-1)
    zp = _pad2(z, length, LANES)
    full = pl.BlockSpec((LANES, LANES), lambda i: (0, 0))
    row = pl.BlockSpec((1, LANES), lambda i: (0, 0))
    return pl.pallas_call(
        _hid_kernel,
        grid=(length // tl,),
        in_specs=[pl.BlockSpec((tl, LANES), lambda i: (i, 0)), full, row, full, row, full, row, row],
        out_specs=pl.BlockSpec((tl, LANES), lambda i: (i, 0)),
        out_shape=jax.ShapeDtypeStruct((length, LANES), F32),
        compiler_params=_params(("parallel",), 16 << 20),
        name="filter_hidden",
    )(zp, _pad2(w1, LANES, LANES), _pad2(b1[None], 1, LANES), _pad2(w2, LANES, LANES), _pad2(b2[None], 1, LANES),
      _pad2(w3, LANES, LANES), _pad2(b3[None], 1, LANES), _pad2(freq[None], 1, LANES))


FFT_N1 = 256
FFT_N2 = 128
FFT_K1 = FFT_N1 // 2
FFT_GROUP = BF16_SUBLANES
FFT_PITCH = 3 * F32_SUBLANES


def _phase_tables(length):
    n1h, n2, k1n = FFT_K1, FFT_N2, FFT_K1
    assert length == n1h * n2
    n = 2 * length
    two_pi = 2.0 * math.pi
    ia = jnp.arange(n1h, dtype=jnp.int32)
    pa = (ia[None, :] * (2 * ia[:, None] + 1)) % (2 * FFT_N1)
    tha = pa.astype(F32) * (two_pi / (2 * FFT_N1))
    fa = jnp.concatenate([jnp.cos(tha), -jnp.sin(tha)], axis=0).astype(BF16)
    ca = (jnp.concatenate([jnp.cos(tha).T, -jnp.sin(tha).T], axis=1) * (2.0 / n)).astype(BF16)
    k1 = jnp.arange(k1n, dtype=jnp.int32)[:, None, None]
    k2 = jnp.arange(n2, dtype=jnp.int32)[None, :, None]
    m2 = jnp.arange(n2, dtype=jnp.int32)[None, None, :]
    pb = (m2 * (2 * (k1 + FFT_N1 * k2) + 1)) % (2 * n)
    phb = pb.astype(F32) * (two_pi / (2 * n))
    gr, gi = jnp.cos(phb), -jnp.sin(phb)
    gb = jnp.concatenate([jnp.concatenate([gr, -gi], axis=2), jnp.concatenate([gi, gr], axis=2)], axis=1)
    hr, hi = jnp.swapaxes(gr, 1, 2), -jnp.swapaxes(gi, 1, 2)
    gbi = jnp.concatenate([jnp.concatenate([hr, -hi], axis=2), jnp.concatenate([hi, hr], axis=2)], axis=1)
    return fa, ca, gb.astype(BF16), gbi.astype(BF16)


def _regroup_rows(k):
    return pl.ds(pl.multiple_of(k * FFT_PITCH, F32_SUBLANES), FFT_GROUP)


def _fft_a_core(rows_of, fa_ref, o_ref, s_refs, ns):
    fa = fa_ref[...]
    for j in range(FFT_GROUP):
        res = jnp.dot(fa, rows_of(j).astype(BF16), preferred_element_type=F32)
        for p in range(2):
            for s in range(ns):
                s_refs[p * ns + s][pl.ds(j, FFT_K1, stride=FFT_PITCH), :] = (
                    res[p * FFT_K1:(p + 1) * FFT_K1, s * LANES:(s + 1) * LANES])

    def emit(k, carry):
        for p in range(2):
            for s in range(ns):
                o_ref[0, k, p, :, s * LANES:(s + 1) * LANES] = s_refs[p * ns + s][_regroup_rows(k), :].astype(o_ref.dtype)
        return carry

    lax.fori_loop(0, FFT_K1, emit, 0, unroll=8)


def _fft_a_kernel(*refs, ns):
    x_refs, fa_ref, o_ref = refs[:ns], refs[ns], refs[ns + 1]
    xs_refs = refs[ns + 2:2 * ns + 2]
    for s in range(ns):
        xs_refs[s][...] = x_refs[s][...].reshape(FFT_K1 * FFT_GROUP, LANES)
    rows_of = lambda j: jnp.concatenate([r[pl.ds(j, FFT_K1, stride=FFT_GROUP), :] for r in xs_refs], axis=1)
    _fft_a_core(rows_of, fa_ref, o_ref, refs[2 * ns + 2:], ns)


def _fft_a_filter_kernel(hid_ref, w_ref, delta_ref, fa_ref, o_ref, n_ref, hs_ref, *s_refs, ns, length):
    g = pl.program_id(2)

    @pl.when(g == 0)
    def _():
        n_ref[...] = jnp.zeros_like(n_ref)

    hs_ref[...] = hid_ref[...].reshape(FFT_K1 * FFT_GROUP, LANES)
    w = w_ref[0].astype(BF16)
    delta = delta_ref[...]
    n1 = lax.broadcasted_iota(jnp.int32, (FFT_K1, ns * LANES), 0)
    decay_slab = jnp.exp(-((n1 * FFT_N2).astype(F32) / float(length - 1)) * delta)

    def rows_of(j):
        hid = hs_ref[pl.ds(j, FFT_K1, stride=FFT_GROUP), :]
        decay_row = jnp.exp(-((g * FFT_GROUP + j).astype(F32) / float(length - 1)) * delta)
        f = jnp.dot(hid.astype(BF16), w, preferred_element_type=F32) * decay_slab * decay_row
        n_ref[0, 0:1, :] += jnp.sum(jnp.abs(f), axis=0, keepdims=True)
        return f

    _fft_a_core(rows_of, fa_ref, o_ref, s_refs, ns)


def _fft_a_scratch(ns):
    return [pltpu.VMEM((FFT_K1 * FFT_PITCH, LANES), F32)] * (2 * ns)


def _fft_a_vmem(ns):
    blk = FFT_K1 * FFT_GROUP * LANES
    return 2 * ns * blk * 4 + 2 * 2 * ns * blk * 2 + ns * blk * 4 + 2 * ns * FFT_K1 * FFT_PITCH * LANES * 4 + (8 << 20)


def fft_filter_pass_a(hid, w_o, length, dc=256):
    q, _, d = w_o.shape
    ns = dc // LANES
    delta = jnp.abs(jnp.linspace(MIN_DECAY, MAX_DECAY, d, dtype=F32))[None]
    hid3 = hid.reshape(FFT_K1, FFT_N2, LANES)
    fa = _phase_tables(length)[0]
    return pl.pallas_call(
        functools.partial(_fft_a_filter_kernel, ns=ns, length=length),
        grid=(q, d // dc, FFT_N2 // FFT_GROUP),
        in_specs=[pl.BlockSpec((FFT_K1, FFT_GROUP, LANES), lambda w, ci, g: (0, g, 0)),
                  pl.BlockSpec((1, LANES, dc), lambda w, ci, g: (w, 0, ci)),
                  pl.BlockSpec((1, dc), lambda w, ci, g: (0, ci)),
                  pl.BlockSpec(fa.shape, lambda w, ci, g: (0, 0))],
        out_specs=[pl.BlockSpec((1, FFT_K1, 2, FFT_GROUP, dc), lambda w, ci, g: (w, 0, 0, g, ci)),
                   pl.BlockSpec((1, F32_SUBLANES, dc), lambda w, ci, g: (w, 0, ci))],
        out_shape=[jax.ShapeDtypeStruct((q, FFT_K1, 2, FFT_N2, d), BF16),
                   jax.ShapeDtypeStruct((q, F32_SUBLANES, d), F32)],
        scratch_shapes=[pltpu.VMEM((FFT_K1 * FFT_GROUP, LANES), F32)] + _fft_a_scratch(ns),
        compiler_params=_params(("parallel", "parallel", "arbitrary"), _fft_a_vmem(ns)),
        name="fft_filter_pass_a",
    )(hid3, w_o, delta, fa)


def fft_pass_a(x, col0, d, dc=256):
    q, length, c = x.shape
    ns = dc // LANES
    x4 = x.reshape(q, FFT_K1, FFT_N2, c)
    slab0 = col0 // LANES
    in_specs = [pl.BlockSpec((None, FFT_K1, FFT_GROUP, LANES),
                             functools.partial(lambda w, ci, g, s: (w, 0, g, slab0 + ci * ns + s), s=s))
                for s in range(ns)]
    fa = _phase_tables(length)[0]
    in_specs.append(pl.BlockSpec(fa.shape, lambda w, ci, g: (0, 0)))
    return pl.pallas_call(
        functools.partial(_fft_a_kernel, ns=ns),
        grid=(q, d // dc, FFT_N2 // FFT_GROUP),
        in_specs=in_specs,
        out_specs=pl.BlockSpec((1, FFT_K1, 2, FFT_GROUP, dc), lambda w, ci, g: (w, 0, 0, g, ci)),
        out_shape=jax.ShapeDtypeStruct((q, FFT_K1, 2, FFT_N2, d), BF16),
        scratch_shapes=[pltpu.VMEM((FFT_K1 * FFT_GROUP, LANES), F32)] * ns + _fft_a_scratch(ns),
        compiler_params=_params(("parallel", "parallel", "parallel"), _fft_a_vmem(ns)),
        name="fft_pass_a",
    )(*([x4] * ns), fa)


def _fft_bk_kernel(af_ref, ab_ref, gb_ref, o_ref):
    for i in range(af_ref.shape[1]):
        g = gb_ref[i]
        uf = jnp.dot(g, af_ref[0, i], preferred_element_type=F32)
        ub = jnp.dot(g, ab_ref[0, i], preferred_element_type=F32)
        n2 = FFT_N2
        o_ref[0, i, 0:n2, :] = (uf[0:n2] + ub[0:n2]).astype(o_ref.dtype)
        o_ref[0, i, n2:2 * n2, :] = (uf[n2:2 * n2] - ub[n2:2 * n2]).astype(o_ref.dtype)


def _fft_b_vmem(kb, r, dc, n_blocks, n_tables):
    return 2 * n_blocks * kb * r * dc * 2 + 2 * n_tables * kb * r * r * 2 + 6 * r * dc * 4 + (4 << 20)


def fft_filter_spectrum(a, gb, kb=8, dc=512):
    q, k1, r, d = a.shape
    blk = lambda f: pl.BlockSpec((1, kb, r, dc), f)
    return pl.pallas_call(
        _fft_bk_kernel,
        grid=(q // 2, k1 // kb, d // dc),
        in_specs=[blk(lambda o, k, c: (2 * o, k, 0, c)), blk(lambda o, k, c: (2 * o + 1, k, 0, c)),
                  pl.BlockSpec((kb, r, r), lambda o, k, c: (k, 0, 0))],
        out_specs=blk(lambda o, k, c: (o, k, 0, c)),
        out_shape=jax.ShapeDtypeStruct((q // 2, k1, r, d), BF16),
        compiler_params=_params(("parallel", "parallel", "parallel"), _fft_b_vmem(kb, r, dc, 3, 1)),
        name="fft_filter_spectrum",
    )(a, a, gb)


def _fft_b_kernel(a_ref, k_ref, gb_ref, gbi_ref, o_ref):
    n2 = FFT_N2
    for i in range(a_ref.shape[0]):
        u = jnp.dot(gb_ref[i], a_ref[i], preferred_element_type=F32)
        ur, ui = u[0:n2], u[n2:2 * n2]
        kr, ki = k_ref[0, i, 0:n2, :].astype(F32), k_ref[0, i, n2:2 * n2, :].astype(F32)
        v = jnp.concatenate([ur * kr - ui * ki, ur * ki + ui * kr], axis=0).astype(BF16)
        o_ref[i] = jnp.dot(gbi_ref[i], v, preferred_element_type=F32).astype(o_ref.dtype)


def fft_pass_b(a, kspec, order, gb, gbi, kb=8, dc=512):
    k1, r, d = a.shape
    return pl.pallas_call(
        _fft_b_kernel,
        grid=(k1 // kb, d // dc),
        in_specs=[pl.BlockSpec((kb, r, dc), lambda k, c: (k, 0, c)),
                  pl.BlockSpec((1, kb, r, dc), lambda k, c: (order, k, 0, c)),
                  pl.BlockSpec((kb, r, r), lambda k, c: (k, 0, 0)),
                  pl.BlockSpec((kb, r, r), lambda k, c: (k, 0, 0))],
        out_specs=pl.BlockSpec((kb, r, dc), lambda k, c: (k, 0, c)),
        out_shape=jax.ShapeDtypeStruct((k1, r, d), BF16),
        compiler_params=_params(("parallel", "parallel"), _fft_b_vmem(kb, r, dc, 3, 2)),
        name="fft_pass_b",
    )(a, kspec, gb, gbi)


def _fft_c_kernel(*refs, ns):
    b_ref, ca_ref, gate_refs, z_refs = refs[0], refs[1], refs[2:2 + ns], refs[2 + ns:2 + 2 * ns]
    n_ref, bias_ref, o_ref = refs[2 + 2 * ns:5 + 2 * ns]
    s_refs = refs[5 + 2 * ns:5 + 4 * ns]
    t_refs = refs[5 + 4 * ns:]
    ca = ca_ref[...]

    def spread(k, carry):
        for p in range(2):
            for s in range(ns):
                s_refs[p * ns + s][_regroup_rows(k), :] = b_ref[k, p, :, s * LANES:(s + 1) * LANES].astype(F32)
        return carry

    lax.fori_loop(0, FFT_K1, spread, 0, unroll=8)
    for j in range(FFT_GROUP):
        b = jnp.concatenate([
            jnp.concatenate([s_refs[p * ns + s][pl.ds(j, FFT_K1, stride=FFT_PITCH), :] for s in range(ns)], axis=1)
            for p in range(2)], axis=0)
        y = jnp.dot(ca, b.astype(BF16), preferred_element_type=F32)
        for s in range(ns):
            t_refs[s][pl.ds(j, FFT_K1, stride=FFT_PITCH), :] = y[:, s * LANES:(s + 1) * LANES]

    def emit(k, carry):
        for s in range(ns):
            sl = slice(s * LANES, (s + 1) * LANES)
            y = t_refs[s][_regroup_rows(k), :] / (n_ref[0:1, sl] + n_ref[1:2, sl] + 1e-6)
            o_ref[k, :, sl] = (gate_refs[s][k].astype(F32) * (y + bias_ref[:, sl] * z_refs[s][k])).astype(o_ref.dtype)
        return carry

    lax.fori_loop(0, FFT_K1, emit, 0, unroll=8)


def fft_pass_c(bp, ca, gate_arr, gate_col0, z_arr, z_col0, nsum, bias, out_dtype, dc=256):
    k1, r, d = bp.shape
    ns = dc // LANES
    length = FFT_K1 * FFT_N2
    b4 = bp.reshape(k1, 2, FFT_N2, d)
    g3 = gate_arr.reshape(FFT_K1, FFT_N2, gate_arr.shape[1])
    z3 = z_arr.reshape(FFT_K1, FFT_N2, z_arr.shape[1])
    sig = lambda slab0: [pl.BlockSpec((FFT_K1, FFT_GROUP, LANES),
                                      functools.partial(lambda ci, g, s: (0, g, slab0 + ci * ns + s), s=s))
                         for s in range(ns)]
    row = pl.BlockSpec((1, dc), lambda ci, g: (0, ci))
    blk = FFT_K1 * FFT_GROUP * dc
    vmem = 2 * 2 * blk * 2 + 2 * 3 * blk * 4 + 3 * ns * FFT_K1 * FFT_PITCH * LANES * 4 + (8 << 20)
    out = pl.pallas_call(
        functools.partial(_fft_c_kernel, ns=ns),
        grid=(d // dc, FFT_N2 // FFT_GROUP),
        in_specs=[pl.BlockSpec((k1, 2, FFT_GROUP, dc), lambda ci, g: (0, 0, g, ci)),
                  pl.BlockSpec(ca.shape, lambda ci, g: (0, 0)),
                  *sig(gate_col0 // LANES), *sig(z_col0 // LANES),
                  pl.BlockSpec((2, dc), lambda ci, g: (0, ci)), row],
        out_specs=pl.BlockSpec((FFT_K1, FFT_GROUP, dc), lambda ci, g: (0, g, ci)),
        out_shape=jax.ShapeDtypeStruct((FFT_K1, FFT_N2, d), out_dtype),
        scratch_shapes=[pltpu.VMEM((FFT_K1 * FFT_PITCH, LANES), F32)] * (3 * ns),
        compiler_params=_params(("parallel", "parallel"), vmem),
        name="fft_pass_c",
    )(b4, ca, *([g3] * ns), *([z3] * ns), nsum, bias)
    return out.reshape(length, d)


def kernel(x, c, ctx, c_ctx, ada_w, ada_b, norm_g, mix_w_in, mix_conv_w, mix_sink, mix_w_out, ffn_w_gu, ffn_w_down, hy_w_in, hy_b_in, hy_conv_w, hy_conv_b, hf_w1, hf_b1, hf_freq, hf_w2, hf_b2, hf_w3, hf_b3, hf_w_out, hf_bias, hy_w_out, hy_b_out, router_w, router_b, moe_w_gu, moe_w_down):
    assert x.shape[0] == 1 and ada_w.shape[0] == 2
    s, d = x.shape[1], x.shape[2]
    xs = x[0]
    ctxs = ctx[0]

    acts = jnp.zeros((F32_SUBLANES, d), F32).at[0].set(c[0]).at[1].set(c_ctx)
    mods = adaln_all(acts, ada_w, ada_b)
    row = lambda layer, r, k: mods[layer, r:r + 1, k * d:(k + 1) * d]

    g = norm_g[0]
    h = norm_mod(xs, g[0:1], row(0, 0, 0), row(0, 0, 1))
    hc = norm_mod(ctxs, g[0:1], row(0, 1, 0), row(0, 1, 1))
    w_in = mix_w_in[0].astype(BF16)
    p, ffn_gu = matmul(h, w_in, out_dtype=BF16, side=ffn_w_gu[0], side_chunks=256)
    ckv = matmul(hc, w_in[:, KV_START:], out_dtype=BF16)
    cs, sn = rope_tables(s)
    y = even_mixer_core(p, ckv, cs, sn, mix_conv_w[0], mix_sink[0])
    out = matmul(y, mix_w_out[0].astype(BF16))
    xs, h = post_norm(xs, out, g[1:2], row(0, 0, 2), g[2:3], row(0, 0, 3), row(0, 0, 4))

    act, ffn_down = swiglu_up(h, ffn_gu, tm=2048, tn=256, side=ffn_w_down[0], side_chunks=344)
    out, hy_in = matmul(act, ffn_down, tm=512, tn=256, side=hy_w_in[0], side_chunks=256)
    g1 = norm_g[1]
    xs, h = post_norm(xs, out, g[3:4], row(0, 0, 5), g1[0:1], row(1, 0, 0), row(1, 0, 1))

    g = g1
    moe_gu_f32 = moe_w_gu[0].reshape(-1, moe_w_gu.shape[-1])
    u0, moe_gu = matmul(h, hy_in, bias=hy_b_in[0][None], out_dtype=BF16,
                        side=moe_gu_f32, side_chunks=256)
    x12 = short_conv(u0, hy_conv_w[0], hy_conv_b[0][None], 0, 2 * d, BF16)
    v = short_conv(u0, hy_conv_w[0], hy_conv_b[0][None], 2 * d, d, F32)
    hid = filter_hidden(s, hf_w1[0], hf_b1[0], hf_freq[0], hf_w2[0], hf_b2[0], hf_w3[0], hf_b3[0])
    w_o = hf_w_out[0].astype(F32).reshape(FILTER_HIDDEN, HYENA_ORDER * 2, d).transpose(1, 0, 2)
    w_o = jnp.zeros((HYENA_ORDER * 2, LANES, d), F32).at[:, :FILTER_HIDDEN].set(w_o)
    filt_a, fsum = fft_filter_pass_a(hid, w_o, s)
    fsum = fsum[:, 0, :]
    _, ca, gb, gbi = _phase_tables(s)
    spec = lambda a: a.reshape(a.shape[0], FFT_K1, 2 * FFT_N2, d)
    kspec = fft_filter_spectrum(spec(filt_a), gb)
    a = spec(fft_pass_a(v[None], 0, d))[0]
    z1 = fft_pass_c(fft_pass_b(a, kspec, 0, gb, gbi), ca, x12, 0, v, 0, fsum[0:2], hf_bias[0, 0][None], F32)
    a = spec(fft_pass_a(z1[None], 0, d))[0]
    z2 = fft_pass_c(fft_pass_b(a, kspec, 1, gb, gbi), ca, x12, d, z1, 0, fsum[2:4], hf_bias[0, 1][None], BF16)
    moe_down_f32 = moe_w_down[0].reshape(-1, moe_w_down.shape[-1])
    out, moe_down = matmul(z2, hy_w_out[0].astype(BF16), bias=hy_b_out[0][None], side=moe_down_f32, side_chunks=128)
    xs = post_residual(xs, out, g[1:2], row(1, 0, 2))

    sel, wts = router_top2(xs, g[2:3], row(1, 0, 3), row(1, 0, 4), router_w[0], router_b[0])
    pos, src_tok, tile_expert, n_valid = moe_plan(sel)
    hg = moe_gather_norm(xs, src_tok, g[2:3], row(1, 0, 3), row(1, 0, 4))
    act = grouped_swiglu_up(hg, moe_gu.reshape(moe_w_gu.shape[1:]), tile_expert, n_valid)
    ys = grouped_down(act, moe_down.reshape(moe_w_down.shape[1:]), tile_expert, n_valid)
    xs = moe_combine_residual(xs, wts, ys, pos, g[3:4], row(1, 0, 5))
    return xs[None]
```

```python
import functools
import math

import jax
import jax.numpy as jnp
from jax import lax
from jax.experimental import pallas as pl
from jax.experimental.pallas import tpu as pltpu

F32 = jnp.float32
BF16 = jnp.bfloat16

D_MODEL = 4096
GRID_W = 64
HEAD_DIM = 128
CONV_WIDTH = D_MODEL // 2
N_HEADS = (D_MODEL // 2) // HEAD_DIM
N_KV_HEADS = N_HEADS // 4
GQA_GROUP = N_HEADS // N_KV_HEADS
Q_WIDTH = N_HEADS * HEAD_DIM
KV_WIDTH = N_KV_HEADS * HEAD_DIM
Q_START = 3 * CONV_WIDTH
KV_START = Q_START + Q_WIDTH
IN_WIDTH = KV_START + 2 * KV_WIDTH
BLOCK = 128
ATTN_SCALE = HEAD_DIM ** -0.5
ROPE_BASE = 10000.0
ROPE_PAIRS = HEAD_DIM // 4
HYENA_ORDER = 2
FILTER_EMB = 33
FILTER_BANDS = (FILTER_EMB - 1) // 2
FILTER_HIDDEN = 64
MIN_DECAY = math.log(1e-2) / 0.3
MAX_DECAY = math.log(1e-2) / 1.5
N_EXPERTS = 8
NORM_EPS = 1e-6

V7X_VMEM_BYTES = 64 * 1024 * 1024
LANES = 128
F32_SUBLANES = 8
BF16_SUBLANES = 16


def _params(semantics, vmem_bytes):
    limit = min(int(vmem_bytes), V7X_VMEM_BYTES - 4 * 1024 * 1024)
    return pltpu.CompilerParams(dimension_semantics=semantics, vmem_limit_bytes=limit)


def _sigmoid(v):
    return 1.0 / (1.0 + jnp.exp(-v))


def _adaln_kernel(a_ref, w_ref, b_ref, o_ref, acc_ref):
    k = pl.program_id(2)

    @pl.when(k == 0)
    def _():
        acc_ref[...] = jnp.zeros_like(acc_ref)

    a = a_ref[...]
    a = a * _sigmoid(a)
    acc_ref[...] += jnp.dot(a.astype(BF16), w_ref[0].astype(BF16), preferred_element_type=F32)

    @pl.when(k == pl.num_programs(2) - 1)
    def _():
        o_ref[0] = acc_ref[...] + b_ref[0]


def adaln_all(acts, ada_w, ada_b):
    depth, d, n = ada_w.shape
    tn, tk = 2048, 1024
    return pl.pallas_call(
        _adaln_kernel,
        grid=(depth, n // tn, d // tk),
        in_specs=[
            pl.BlockSpec((F32_SUBLANES, tk), lambda l, j, k: (0, k)),
            pl.BlockSpec((1, tk, tn), lambda l, j, k: (l, k, j)),
            pl.BlockSpec((1, 1, tn), lambda l, j, k: (l, 0, j)),
        ],
        out_specs=pl.BlockSpec((1, F32_SUBLANES, tn), lambda l, j, k: (l, 0, j)),
        out_shape=jax.ShapeDtypeStruct((depth, F32_SUBLANES, n), F32),
        scratch_shapes=[pltpu.VMEM((F32_SUBLANES, tn), F32)],
        compiler_params=_params(("parallel", "parallel", "arbitrary"), 2 * tk * tn * 4 + tk * tn * 2 + (4 << 20)),
        name="adaln",
    )(acts, ada_w, ada_b.reshape(depth, 1, n))


def _norm_mod(x, g, sh, sc):
    y = x * lax.rsqrt(jnp.mean(x * x, axis=-1, keepdims=True) + NORM_EPS)
    return (y * g) * (1.0 + sc) + sh


def _norm_mod_kernel(x_ref, g_ref, sh_ref, sc_ref, o_ref):
    o_ref[...] = _norm_mod(x_ref[...], g_ref[...], sh_ref[...], sc_ref[...]).astype(o_ref.dtype)


def norm_mod(x, g, shift, scale, tm=512):
    m, d = x.shape
    tm = min(tm, m)
    row = pl.BlockSpec((1, d), lambda i: (0, 0))
    return pl.pallas_call(
        _norm_mod_kernel,
        grid=(m // tm,),
        in_specs=[pl.BlockSpec((tm, d), lambda i: (i, 0)), row, row, row],
        out_specs=pl.BlockSpec((tm, d), lambda i: (i, 0)),
        out_shape=jax.ShapeDtypeStruct((m, d), BF16),
        compiler_params=_params(("parallel",), 2 * tm * d * 4 + 2 * tm * d * 2 + 3 * tm * d * 4 + (2 << 20)),
        name="norm_mod",
    )(x, g, shift, scale)


def _post_norm_kernel(x_ref, y_ref, g_ref, gt_ref, g2_ref, sh_ref, sc_ref, o_ref, h_ref):
    y = y_ref[...]
    yn = y * lax.rsqrt(jnp.mean(y * y, axis=-1, keepdims=True) + NORM_EPS)
    x = x_ref[...] + gt_ref[...] * (yn * g_ref[...])
    o_ref[...] = x
    h_ref[...] = _norm_mod(x, g2_ref[...], sh_ref[...], sc_ref[...]).astype(h_ref.dtype)


def post_norm(x, y, g, gate, g_next, shift, scale, tm=256):
    m, d = x.shape
    row = pl.BlockSpec((1, d), lambda i: (0, 0))
    blk = pl.BlockSpec((tm, d), lambda i: (i, 0))
    return pl.pallas_call(
        _post_norm_kernel,
        grid=(m // tm,),
        in_specs=[blk, blk, row, row, row, row, row],
        out_specs=[blk, blk],
        out_shape=[jax.ShapeDtypeStruct((m, d), F32), jax.ShapeDtypeStruct((m, d), BF16)],
        compiler_params=_params(("parallel",), 7 * tm * d * 4 + 5 * tm * d * 4 + (2 << 20)),
        name="post_norm",
    )(x, y, g, gate, g_next, shift, scale)


def _side_cast(src_ref, dst_ref, step, n_chunks):
    @pl.when(step < n_chunks)
    def _():
        dst_ref[...] = src_ref[...].astype(dst_ref.dtype)


def _side_cast_plumbing(side, n_chunks, step_of):
    rows, cols = side.shape
    spec = pl.BlockSpec((rows // n_chunks, cols), lambda *g: (jnp.minimum(step_of(*g), n_chunks - 1), 0))
    vmem = 2 * (rows // n_chunks) * cols * (4 + 2)
    return spec, jax.ShapeDtypeStruct((rows, cols), BF16), vmem


def _mm_kernel(*refs, has_bias, side_chunks):
    a_ref, w_ref = refs[0], refs[1]
    n_in = 2 + has_bias + (side_chunks > 0)
    o_ref = refs[n_in]
    acc = jnp.dot(a_ref[...], w_ref[...], preferred_element_type=F32)
    if has_bias:
        acc = acc + refs[2][...]
    o_ref[...] = acc.astype(o_ref.dtype)
    if side_chunks:
        step = pl.program_id(0) * pl.num_programs(1) + pl.program_id(1)
        _side_cast(refs[n_in - 1], refs[n_in + 1], step, side_chunks)


def matmul(a, w, bias=None, out_dtype=F32, tm=1024, tn=512, side=None, side_chunks=0):
    m, k = a.shape
    n = w.shape[1]
    tm, tn = min(tm, m), min(tn, n)
    nj = n // tn
    in_specs = [pl.BlockSpec((tm, k), lambda i, j: (i, 0)), pl.BlockSpec((k, tn), lambda i, j: (0, j))]
    args = [a, w]
    if bias is not None:
        in_specs.append(pl.BlockSpec((1, tn), lambda i, j: (0, j)))
        args.append(bias)
    osz = jnp.dtype(out_dtype).itemsize
    vmem = 2 * tm * k * 2 + 2 * k * tn * 2 + 2 * tm * tn * osz + 2 * tm * tn * 4 + (2 << 20)
    out_specs = pl.BlockSpec((tm, tn), lambda i, j: (i, j))
    out_shape = jax.ShapeDtypeStruct((m, n), out_dtype)
    if side is not None:
        assert side_chunks <= (m // tm) * nj
        spec, shape, side_vmem = _side_cast_plumbing(side, side_chunks, lambda i, j: i * nj + j)
        in_specs.append(spec)
        args.append(side)
        out_specs, out_shape, vmem = [out_specs, spec], [out_shape, shape], vmem + side_vmem
    return pl.pallas_call(
        functools.partial(_mm_kernel, has_bias=bias is not None, side_chunks=side_chunks),
        grid=(m // tm, nj),
        in_specs=in_specs,
        out_specs=out_specs,
        out_shape=out_shape,
        compiler_params=_params(("arbitrary", "arbitrary") if side is not None else ("parallel", "arbitrary"), vmem),
        name="matmul",
    )(*args)


def _swiglu_up_kernel(a_ref, wg_ref, wu_ref, side_ref, o_ref, side_o_ref, *, side_chunks):
    a = a_ref[...]
    g = jnp.dot(a, wg_ref[...], preferred_element_type=F32)
    u = jnp.dot(a, wu_ref[...], preferred_element_type=F32)
    o_ref[...] = (g * _sigmoid(g) * u).astype(o_ref.dtype)
    _side_cast(side_ref, side_o_ref, pl.program_id(0) * pl.num_programs(1) + pl.program_id(1), side_chunks)


def swiglu_up(a, w_gu, tm, tn, side, side_chunks):
    m, k = a.shape
    f = w_gu.shape[1] // 2
    nj = f // tn
    assert side_chunks <= (m // tm) * nj
    spec, shape, side_vmem = _side_cast_plumbing(side, side_chunks, lambda i, j: i * nj + j)
    vmem = 2 * tm * k * 2 + 4 * k * tn * 2 + 2 * tm * tn * 2 + 4 * tm * tn * 4 + side_vmem + (2 << 20)
    return pl.pallas_call(
        functools.partial(_swiglu_up_kernel, side_chunks=side_chunks),
        grid=(m // tm, nj),
        in_specs=[
            pl.BlockSpec((tm, k), lambda i, j: (i, 0)),
            pl.BlockSpec((k, tn), lambda i, j: (0, j)),
            pl.BlockSpec((k, tn), lambda i, j: (0, j + nj)),
            spec,
        ],
        out_specs=[pl.BlockSpec((tm, tn), lambda i, j: (i, j)), spec],
        out_shape=[jax.ShapeDtypeStruct((m, f), BF16), shape],
        compiler_params=_params(("arbitrary", "arbitrary"), vmem),
        name="swiglu_up",
    )(a, w_gu, w_gu, side)


def _post_router_kernel(x_ref, y_ref, gp_ref, gt_ref, g_ref, sh_ref, sc_ref, w_ref, b_ref, o_ref, sel_ref, wts_ref):
    y = y_ref[...]
    yn = y * lax.rsqrt(jnp.mean(y * y, axis=-1, keepdims=True) + NORM_EPS)
    x = x_ref[...] + gt_ref[...] * (yn * gp_ref[...])
    o_ref[...] = x
    h = _norm_mod(x, g_ref[...], sh_ref[...], sc_ref[...])
    logits = jnp.dot(h, w_ref[...], preferred_element_type=F32, precision=lax.Precision.HIGHEST) + b_ref[...]
    lane = lax.broadcasted_iota(jnp.int32, logits.shape, 1)
    neg = jnp.float32(-jnp.inf)
    logits = jnp.where(lane < N_EXPERTS, logits, neg)
    v1 = jnp.max(logits, axis=1, keepdims=True)
    i1 = jnp.min(jnp.where(logits == v1, lane, LANES), axis=1, keepdims=True)
    rest = jnp.where(lane == i1, neg, logits)
    v2 = jnp.max(rest, axis=1, keepdims=True)
    i2 = jnp.min(jnp.where(rest == v2, lane, LANES), axis=1, keepdims=True)
    e2 = jnp.exp(v2 - v1)
    w1 = 1.0 / (1.0 + e2)
    w2 = e2 / (1.0 + e2)
    sel_ref[...] = jnp.where(lane == 0, i1, jnp.where(lane == 1, i2, 0))
    wts_ref[...] = jnp.where(lane == 0, w1, jnp.where(lane == 1, w2, 0.0))


def post_router_top2(x, y, g_post, gate, g, shift, scale, router_w, router_b, tm=256):
    m, d = x.shape
    wp = jnp.zeros((d, LANES), F32).at[:, :N_EXPERTS].set(router_w)
    bp = jnp.zeros((1, LANES), F32).at[0, :N_EXPERTS].set(router_b)
    row = pl.BlockSpec((1, d), lambda i: (0, 0))
    blk = pl.BlockSpec((tm, d), lambda i: (i, 0))
    out = pl.BlockSpec((tm, LANES), lambda i: (i, 0))
    return pl.pallas_call(
        _post_router_kernel,
        grid=(m // tm,),
        in_specs=[blk, blk, row, row, row, row, row,
                  pl.BlockSpec((d, LANES), lambda i: (0, 0)), pl.BlockSpec((1, LANES), lambda i: (0, 0))],
        out_specs=[blk, out, out],
        out_shape=[jax.ShapeDtypeStruct((m, d), F32), jax.ShapeDtypeStruct((m, LANES), jnp.int32),
                   jax.ShapeDtypeStruct((m, LANES), F32)],
        compiler_params=_params(("parallel",), 6 * tm * d * 4 + 6 * tm * d * 4 + 2 * d * LANES * 4 + (4 << 20)),
        name="post_router",
    )(x, y, g_post, gate, g, shift, scale, wp, bp)


MOE_TILE = 512


def moe_plan(sel):
    t = sel.shape[0]
    npairs = 2 * t
    rows = npairs + N_EXPERTS * MOE_TILE
    e_flat = sel[:, :2].reshape(npairs)
    order = jnp.argsort(e_flat, stable=True).astype(jnp.int32)
    rank = jnp.argsort(order).astype(jnp.int32)
    counts = jnp.sum((e_flat[:, None] == jnp.arange(N_EXPERTS, dtype=jnp.int32)[None, :]).astype(jnp.int32), axis=0)
    padded = (counts + MOE_TILE - 1) // MOE_TILE * MOE_TILE
    ends_p = jnp.cumsum(padded)
    starts_p = ends_p - padded
    starts_u = jnp.cumsum(counts) - counts
    pos = starts_p[e_flat] + rank - starts_u[e_flat]
    tile_start = jnp.arange(rows // MOE_TILE, dtype=jnp.int32) * MOE_TILE
    tile_expert = jnp.minimum(jnp.searchsorted(ends_p, tile_start, side="right"), N_EXPERTS - 1).astype(jnp.int32)
    n_valid = (ends_p[-1] // MOE_TILE).astype(jnp.int32).reshape(1)
    r = jnp.arange(rows, dtype=jnp.int32)
    e_row = jnp.repeat(tile_expert, MOE_TILE)
    in_group = r - starts_p[e_row]
    valid = (in_group < counts[e_row]) & (r < ends_p[-1])
    src_tok = jnp.where(valid, order[jnp.clip(starts_u[e_row] + in_group, 0, npairs - 1)] // 2, 0)
    return pos.astype(jnp.int32), src_tok.astype(jnp.int32), tile_expert, n_valid


MOE_GATHER_CHUNKS = 4


def _gather_rows(idx_ref, base, stride, src_hbm, dst_ref, sem, r0, n, wait):
    def body(r, carry):
        copy = pltpu.make_async_copy(src_hbm.at[pl.ds(idx_ref[base + stride * r], 1)], dst_ref.at[pl.ds(r, 1)], sem)
        if wait:
            copy.wait()
        else:
            copy.start()
        return carry

    lax.fori_loop(r0, r0 + n, body, 0, unroll=8)


def _gswiglu_kernel(te_ref, nv_ref, src_ref, x_hbm, g_ref, sh_ref, sc_ref, wg_ref, wu_ref, o_ref,
                    xbuf_ref, a_ref, sems):
    i, j = pl.program_id(0), pl.program_id(1)
    tm = a_ref.shape[0]
    n_valid = nv_ref[0]
    slot = i % 2

    def gather(tile, to_slot, wait):
        @pl.when(tile < n_valid)
        def _():
            _gather_rows(src_ref, tile * tm, 1, x_hbm, xbuf_ref.at[to_slot], sems.at[to_slot], 0, tm, wait=wait)

    @pl.when(j == 0)
    def _():
        @pl.when(i == 0)
        def _():
            gather(i, slot, False)

        gather(i, slot, True)
        gather(i + 1, 1 - slot, False)

        @pl.when(i < n_valid)
        def _():
            a_ref[...] = _norm_mod(xbuf_ref[slot], g_ref[...], sh_ref[...], sc_ref[...]).astype(a_ref.dtype)

    @pl.when(i < n_valid)
    def _():
        a = a_ref[...]
        g = jnp.dot(a, wg_ref[0], preferred_element_type=F32)
        u = jnp.dot(a, wu_ref[0], preferred_element_type=F32)
        o_ref[...] = (g * _sigmoid(g) * u).astype(o_ref.dtype)

    @pl.when(i >= n_valid)
    def _():
        o_ref[...] = jnp.zeros_like(o_ref)


def grouped_swiglu_up(x, src_tok, g, shift, scale, w_gu, tile_expert, n_valid, tn=512):
    k = x.shape[1]
    rows = src_tok.shape[0]
    f = w_gu.shape[2] // 2
    nj = f // tn
    tm = MOE_TILE
    row = pl.BlockSpec((1, k), lambda i, j, te, nv, src: (0, 0))
    vmem = 2 * tm * k * 4 + tm * k * 2 + 3 * tm * k * 4 + 4 * k * tn * 2 + 2 * tm * tn * 2 + 4 * tm * tn * 4 + (2 << 20)
    return pl.pallas_call(
        _gswiglu_kernel,
        grid_spec=pltpu.PrefetchScalarGridSpec(
            num_scalar_prefetch=3,
            grid=(rows // tm, nj),
            in_specs=[
                pl.BlockSpec(memory_space=pl.ANY), row, row, row,
                pl.BlockSpec((1, k, tn), lambda i, j, te, nv, src: (te[i], 0, j)),
                pl.BlockSpec((1, k, tn), lambda i, j, te, nv, src: (te[i], 0, j + nj)),
            ],
            out_specs=pl.BlockSpec((tm, tn), lambda i, j, te, nv, src: (i, j)),
            scratch_shapes=[pltpu.VMEM((2, tm, k), F32), pltpu.VMEM((tm, k), BF16), pltpu.SemaphoreType.DMA((2,))],
        ),
        out_shape=jax.ShapeDtypeStruct((rows, f), BF16),
        compiler_params=_params(("arbitrary", "arbitrary"), vmem),
        name="grouped_swiglu_up",
    )(tile_expert, n_valid, src_tok, x, g, shift, scale, w_gu, w_gu)


def _gdown_kernel(te_ref, nv_ref, a_ref, w_ref, o_ref):
    valid = pl.program_id(0) < nv_ref[0]

    @pl.when(valid)
    def _():
        o_ref[...] = jnp.dot(a_ref[...], w_ref[0], preferred_element_type=F32)

    @pl.when(jnp.logical_not(valid))
    def _():
        o_ref[...] = jnp.zeros_like(o_ref)


def grouped_down(act, w_down, tile_expert, n_valid, tn=1024):
    rows, f = act.shape
    n = w_down.shape[2]
    tm = MOE_TILE
    vmem = 2 * tm * f * 2 + 2 * f * tn * 2 + 2 * tm * tn * 4 + 2 * tm * tn * 4 + (2 << 20)
    return pl.pallas_call(
        _gdown_kernel,
        grid_spec=pltpu.PrefetchScalarGridSpec(
            num_scalar_prefetch=2,
            grid=(rows // tm, n // tn),
            in_specs=[
                pl.BlockSpec((tm, f), lambda i, j, te, nv: (i, 0)),
                pl.BlockSpec((1, f, tn), lambda i, j, te, nv: (te[i], 0, j)),
            ],
            out_specs=pl.BlockSpec((tm, tn), lambda i, j, te, nv: (i, j)),
        ),
        out_shape=jax.ShapeDtypeStruct((rows, n), F32),
        compiler_params=_params(("parallel", "arbitrary"), vmem),
        name="grouped_down",
    )(tile_expert, n_valid, act, w_down)


def _moe_combine_kernel(pos_ref, x_ref, w_ref, ys_hbm, g_ref, gt_ref, o_ref, buf_ref, sems):
    tm = x_ref.shape[0]
    rc = tm // MOE_GATHER_CHUNKS
    base = 2 * pl.program_id(0) * tm
    for wait in (False, True):
        for c in range(MOE_GATHER_CHUNKS):
            for slot in range(2):
                _gather_rows(pos_ref, base + slot, 2, ys_hbm, buf_ref.at[slot], sems.at[c], c * rc, rc, wait=wait)
            if wait:
                rows = pl.ds(c * rc, rc)
                w = w_ref[rows, :]
                y = w[:, 0:1] * buf_ref[0, rows, :] + w[:, 1:2] * buf_ref[1, rows, :]
                yn = y * lax.rsqrt(jnp.mean(y * y, axis=-1, keepdims=True) + NORM_EPS)
                o_ref[rows, :] = x_ref[rows, :] + gt_ref[...] * (yn * g_ref[...])


def moe_combine_residual(x, wts, ys, pos, g, gate, tm=128):
    t, d = x.shape
    row = pl.BlockSpec((1, d), lambda i, p: (0, 0))
    return pl.pallas_call(
        _moe_combine_kernel,
        grid_spec=pltpu.PrefetchScalarGridSpec(
            num_scalar_prefetch=1,
            grid=(t // tm,),
            in_specs=[pl.BlockSpec((tm, d), lambda i, p: (i, 0)), pl.BlockSpec((tm, LANES), lambda i, p: (i, 0)),
                      pl.BlockSpec(memory_space=pl.ANY), row, row],
            out_specs=pl.BlockSpec((tm, d), lambda i, p: (i, 0)),
            scratch_shapes=[pltpu.VMEM((2, tm, d), F32), pltpu.SemaphoreType.DMA((MOE_GATHER_CHUNKS,))],
        ),
        out_shape=jax.ShapeDtypeStruct((t, d), F32),
        compiler_params=_params(("arbitrary",), 10 * tm * d * 4 + (2 << 20)),
        name="moe_combine_residual",
    )(pos, x, wts, ys, g, gate)


def _rope(x, cs, sn):
    lane = lax.broadcasted_iota(jnp.int32, x.shape, 1)
    first = (lane % (2 * ROPE_PAIRS)) < ROPE_PAIRS
    partner = jnp.where(first, pltpu.roll(x, HEAD_DIM - ROPE_PAIRS, 1), pltpu.roll(x, ROPE_PAIRS, 1))
    return x * cs + partner * sn


def _shift_rows(u, halo_prev, halo_next):
    n = u.shape[0]
    row = lax.broadcasted_iota(jnp.int32, u.shape, 0)
    prev = jnp.where(row == 0, halo_prev, pltpu.roll(u, 1, 0))
    nxt = jnp.where(row == n - 1, halo_next, pltpu.roll(u, n - 1, 0))
    return prev, nxt


def _mixer_kernel(sink_ref, p_ref, kvp_ref, kvn_ref, cp_ref, hp_ref, cn_ref, hn_ref, ckv_ref,
                  cs_ref, sn_ref, csp_ref, snp_ref, csn_ref, snn_ref, cw_ref, o_ref):
    i = pl.program_id(0)
    nb = pl.num_programs(0)
    cs, sn = cs_ref[...], sn_ref[...]

    last = BF16_SUBLANES - 1
    for c0 in range(0, CONV_WIDTH, LANES):
        cl = slice(c0, c0 + LANES)
        cc = slice(CONV_WIDTH + c0, CONV_WIDTH + c0 + LANES)
        ch = slice(2 * CONV_WIDTH + c0, 2 * CONV_WIDTH + c0 + LANES)
        cu = p_ref[:, cc].astype(F32) * p_ref[:, ch].astype(F32)
        halo_p = cp_ref[last:last + 1, cl].astype(F32) * hp_ref[last:last + 1, cl].astype(F32)
        halo_n = cn_ref[0:1, cl].astype(F32) * hn_ref[0:1, cl].astype(F32)
        halo_p = jnp.where(i > 0, halo_p, 0.0)
        halo_n = jnp.where(i < nb - 1, halo_n, 0.0)
        cu_prev, cu_next = _shift_rows(cu, halo_p, halo_n)
        y_conv = p_ref[:, cl].astype(F32) * (cu_prev * cw_ref[0:1, cl] + cu * cw_ref[1:2, cl] + cu_next * cw_ref[2:3, cl])
        o_ref[:, cl] = y_conv.astype(o_ref.dtype)

    n_ctx = ckv_ref.shape[0]
    n_keys = 3 * BLOCK + n_ctx
    qi = lax.broadcasted_iota(jnp.int32, (BLOCK, n_keys), 0)
    kj = lax.broadcasted_iota(jnp.int32, (BLOCK, n_keys), 1)
    key_pos = i * BLOCK + kj - BLOCK
    visible = (kj >= 3 * BLOCK) | ((jnp.abs(qi + BLOCK - kj) <= BLOCK) & (key_pos >= 0) & (key_pos < nb * BLOCK))
    nt = (((1,), (1,)), ((), ()))

    for kh in range(N_KV_HEADS):
        ko = KV_START + kh * HEAD_DIM
        vo = KV_START + KV_WIDTH + kh * HEAD_DIM
        kb = kh * HEAD_DIM
        vb = KV_WIDTH + kh * HEAD_DIM
        keys = jnp.concatenate([
            _rope(kvp_ref[:, kb:kb + HEAD_DIM].astype(F32), csp_ref[...], snp_ref[...]).astype(BF16),
            _rope(p_ref[:, ko:ko + HEAD_DIM].astype(F32), cs, sn).astype(BF16),
            _rope(kvn_ref[:, kb:kb + HEAD_DIM].astype(F32), csn_ref[...], snn_ref[...]).astype(BF16),
            ckv_ref[:, kb:kb + HEAD_DIM],
        ], axis=0)
        vals = jnp.concatenate([kvp_ref[:, vb:vb + HEAD_DIM], p_ref[:, vo:vo + HEAD_DIM],
                                kvn_ref[:, vb:vb + HEAD_DIM], ckv_ref[:, vb:vb + HEAD_DIM]], axis=0)
        for g in range(GQA_GROUP):
            h = kh * GQA_GROUP + g
            qo = Q_START + h * HEAD_DIM
            q = (_rope(p_ref[:, qo:qo + HEAD_DIM].astype(F32), cs, sn) * ATTN_SCALE).astype(BF16)
            sc = jnp.where(visible, lax.dot_general(q, keys, nt, preferred_element_type=F32), -jnp.inf)
            sink = sink_ref[h]
            m = jnp.maximum(jnp.max(sc, axis=1, keepdims=True), sink)
            e = jnp.exp(sc - m)
            denom = jnp.exp(sink - m) + jnp.sum(e, axis=1, keepdims=True)
            o = jnp.dot(e.astype(BF16), vals, preferred_element_type=F32) / denom
            oo = CONV_WIDTH + h * HEAD_DIM
            o_ref[:, oo:oo + HEAD_DIM] = o.astype(o_ref.dtype)


def even_mixer_core(p, ckv, cs, sn, conv_w, sink):
    s = p.shape[0]
    nb = s // BLOCK
    hb = BLOCK // BF16_SUBLANES
    kvc = KV_START // (2 * KV_WIDTH)
    prev = lambda i: jnp.maximum(i - 1, 0)
    nxt = lambda i: jnp.minimum(i + 1, nb - 1)
    tab = lambda f: pl.BlockSpec((BLOCK, HEAD_DIM), lambda i: (f(i), 0))
    same = lambda i: i
    in_specs = [
        pl.BlockSpec(memory_space=pltpu.SMEM),
        pl.BlockSpec((BLOCK, IN_WIDTH), lambda i: (i, 0)),
        pl.BlockSpec((BLOCK, 2 * KV_WIDTH), lambda i: (prev(i), kvc)),
        pl.BlockSpec((BLOCK, 2 * KV_WIDTH), lambda i: (nxt(i), kvc)),
        pl.BlockSpec((BF16_SUBLANES, CONV_WIDTH), lambda i: (jnp.maximum(i * hb - 1, 0), 1)),
        pl.BlockSpec((BF16_SUBLANES, CONV_WIDTH), lambda i: (jnp.maximum(i * hb - 1, 0), 2)),
        pl.BlockSpec((BF16_SUBLANES, CONV_WIDTH), lambda i: (jnp.minimum((i + 1) * hb, nb * hb - 1), 1)),
        pl.BlockSpec((BF16_SUBLANES, CONV_WIDTH), lambda i: (jnp.minimum((i + 1) * hb, nb * hb - 1), 2)),
        pl.BlockSpec(ckv.shape, lambda i: (0, 0)),
        tab(same), tab(same), tab(prev), tab(prev), tab(nxt), tab(nxt),
        pl.BlockSpec(conv_w.shape, lambda i: (0, 0)),
    ]
    return pl.pallas_call(
        _mixer_kernel,
        grid=(nb,),
        in_specs=in_specs,
        out_specs=pl.BlockSpec((BLOCK, CONV_WIDTH + Q_WIDTH), lambda i: (i, 0)),
        out_shape=jax.ShapeDtypeStruct((s, CONV_WIDTH + Q_WIDTH), BF16),
        compiler_params=_params(("parallel",), 32 << 20),
        name="even_mixer_core",
    )(sink, p, p, p, p, p, p, p, ckv, cs, sn, cs, sn, cs, sn, conv_w)


def rope_tables(s):
    t = jnp.arange(s, dtype=jnp.int32)
    row = (t // GRID_W).astype(F32)
    col = (t % GRID_W).astype(F32)
    inv = ROPE_BASE ** (-jnp.arange(ROPE_PAIRS, dtype=F32) / ROPE_PAIRS)
    ang_r = row[:, None] * inv[None, :]
    ang_c = col[:, None] * inv[None, :]
    cs = jnp.concatenate([jnp.cos(ang_r)] * 2 + [jnp.cos(ang_c)] * 2, axis=1)
    sn = jnp.concatenate([-jnp.sin(ang_r), jnp.sin(ang_r), -jnp.sin(ang_c), jnp.sin(ang_c)], axis=1)
    return cs, sn


def _sconv_kernel(u_ref, up_ref, un_ref, w_ref, b_ref, o_ref):
    i = pl.program_id(0)
    u = u_ref[...].astype(F32)
    last = up_ref.shape[0] - 1
    halo_p = jnp.where(i > 0, up_ref[last:last + 1, :].astype(F32), 0.0)
    halo_n = jnp.where(i < pl.num_programs(0) - 1, un_ref[0:1, :].astype(F32), 0.0)
    prev, nxt = _shift_rows(u, halo_p, halo_n)
    o_ref[...] = (prev * w_ref[0:1, :] + u * w_ref[1:2, :] + nxt * w_ref[2:3, :] + b_ref[...]).astype(o_ref.dtype)


def short_conv(u, w, b, col0, width, out_dtype, tm=256, tn=2048):
    s = u.shape[0]
    hr = BF16_SUBLANES
    hb = tm // hr
    nh = s // hr
    j0 = col0 // tn
    return pl.pallas_call(
        _sconv_kernel,
        grid=(s // tm, width // tn),
        in_specs=[
            pl.BlockSpec((tm, tn), lambda i, j: (i, j + j0)),
            pl.BlockSpec((hr, tn), lambda i, j: (jnp.maximum(i * hb - 1, 0), j + j0)),
            pl.BlockSpec((hr, tn), lambda i, j: (jnp.minimum((i + 1) * hb, nh - 1), j + j0)),
            pl.BlockSpec((3, tn), lambda i, j: (0, j + j0)),
            pl.BlockSpec((1, tn), lambda i, j: (0, j + j0)),
        ],
        out_specs=pl.BlockSpec((tm, tn), lambda i, j: (i, j)),
        out_shape=jax.ShapeDtypeStruct((s, width), out_dtype),
        compiler_params=_params(("parallel", "parallel"), 10 * tm * tn * 4 + (2 << 20)),
        name="short_conv",
    )(u, u, u, w, b)


def _hid_kernel(z_ref, w1_ref, b1_ref, w2_ref, b2_ref, w3_ref, b3_ref, fr_ref, o_ref):
    hp = lax.Precision.HIGHEST
    fr = fr_ref[...]
    h = jnp.sin(fr * (jnp.dot(z_ref[...], w1_ref[...], preferred_element_type=F32, precision=hp) + b1_ref[...]))
    h = jnp.sin(fr * (jnp.dot(h, w2_ref[...], preferred_element_type=F32, precision=hp) + b2_ref[...]))
    o_ref[...] = jnp.sin(fr * (jnp.dot(h, w3_ref[...], preferred_element_type=F32, precision=hp) + b3_ref[...]))


def _pad2(a, r, c):
    return jnp.zeros((r, c), F32).at[:a.shape[0], :a.shape[1]].set(a.astype(F32))


def filter_hidden(length, w1, b1, freq, w2, b2, w3, b3, tl=2048):
    t = jnp.linspace(0.0, 1.0, length, dtype=F32)[:, None]
    w = (2.0 * math.pi / length) * jnp.arange(length, dtype=F32)[:, None]
    bands = jnp.linspace(1e-4, FILTER_BANDS - 1, FILTER_BANDS, dtype=F32)[None]
    z = jnp.concatenate([t, jnp.cos(bands * w), -jnp.sin(bands * w)], axis=-1)
    zp = _pad2(z, length, LANES)
    full = pl.BlockSpec((LANES, LANES), lambda i: (0, 0))
    row = pl.BlockSpec((1, LANES), lambda i: (0, 0))
    return pl.pallas_call(
        _hid_kernel,
        grid=(length // tl,),
        in_specs=[pl.BlockSpec((tl, LANES), lambda i: (i, 0)), full, row, full, row, full, row, row],
        out_specs=pl.BlockSpec((tl, LANES), lambda i: (i, 0)),
        out_shape=jax.ShapeDtypeStruct((length, LANES), F32),
        compiler_params=_params(("parallel",), 16 << 20),
        name="filter_hidden",
    )(zp, _pad2(w1, LANES, LANES), _pad2(b1[None], 1, LANES), _pad2(w2, LANES, LANES), _pad2(b2[None], 1, LANES),
      _pad2(w3, LANES, LANES), _pad2(b3[None], 1, LANES), _pad2(freq[None], 1, LANES))


FFT_N1 = 256
FFT_N2 = 128
FFT_K1 = FFT_N1 // 2
FFT_GROUP = BF16_SUBLANES
FFT_PITCH = 3 * F32_SUBLANES


def _phase_tables(length):
    n1h, n2, k1n = FFT_K1, FFT_N2, FFT_K1
    assert length == n1h * n2
    n = 2 * length
    two_pi = 2.0 * math.pi
    ia = jnp.arange(n1h, dtype=jnp.int32)
    pa = (ia[None, :] * (2 * ia[:, None] + 1)) % (2 * FFT_N1)
    tha = pa.astype(F32) * (two_pi / (2 * FFT_N1))
    fa = jnp.concatenate([jnp.cos(tha), -jnp.sin(tha)], axis=0).astype(BF16)
    ca = (jnp.concatenate([jnp.cos(tha).T, -jnp.sin(tha).T], axis=1) * (2.0 / n)).astype(BF16)
    k1 = jnp.arange(k1n, dtype=jnp.int32)[:, None, None]
    k2 = jnp.arange(n2, dtype=jnp.int32)[None, :, None]
    m2 = jnp.arange(n2, dtype=jnp.int32)[None, None, :]
    pb = (m2 * (2 * (k1 + FFT_N1 * k2) + 1)) % (2 * n)
    phb = pb.astype(F32) * (two_pi / (2 * n))
    gr, gi = jnp.cos(phb), -jnp.sin(phb)
    gb = jnp.concatenate([jnp.concatenate([gr, -gi], axis=2), jnp.concatenate([gi, gr], axis=2)], axis=1)
    hr, hi = jnp.swapaxes(gr, 1, 2), -jnp.swapaxes(gi, 1, 2)
    gbi = jnp.concatenate([jnp.concatenate([hr, -hi], axis=2), jnp.concatenate([hi, hr], axis=2)], axis=1)
    return fa, ca, gb.astype(BF16), gbi.astype(BF16)


def _regroup_rows(k):
    return pl.ds(pl.multiple_of(k * FFT_PITCH, F32_SUBLANES), FFT_GROUP)


def _fft_a_core(rows_of, fa_ref, o_ref, s_refs, ns):
    fa = fa_ref[...]
    for j in range(FFT_GROUP):
        res = jnp.dot(fa, rows_of(j).astype(BF16), preferred_element_type=F32)
        for p in range(2):
            for s in range(ns):
                s_refs[p * ns + s][pl.ds(j, FFT_K1, stride=FFT_PITCH), :] = (
                    res[p * FFT_K1:(p + 1) * FFT_K1, s * LANES:(s + 1) * LANES])

    def emit(k, carry):
        for p in range(2):
            for s in range(ns):
                o_ref[0, k, p, :, s * LANES:(s + 1) * LANES] = s_refs[p * ns + s][_regroup_rows(k), :].astype(o_ref.dtype)
        return carry

    lax.fori_loop(0, FFT_K1, emit, 0, unroll=8)


def _fft_a_kernel(*refs, ns):
    x_refs, fa_ref, o_ref = refs[:ns], refs[ns], refs[ns + 1]
    xs_refs = refs[ns + 2:2 * ns + 2]
    for s in range(ns):
        xs_refs[s][...] = x_refs[s][...].reshape(FFT_K1 * FFT_GROUP, LANES)
    rows_of = lambda j: jnp.concatenate([r[pl.ds(j, FFT_K1, stride=FFT_GROUP), :] for r in xs_refs], axis=1)
    _fft_a_core(rows_of, fa_ref, o_ref, refs[2 * ns + 2:], ns)


def _fft_a_filter_kernel(hid_ref, w_ref, delta_ref, fa_ref, o_ref, n_ref, hs_ref, *s_refs, ns, length):
    g = pl.program_id(2)

    @pl.when(g == 0)
    def _():
        n_ref[...] = jnp.zeros_like(n_ref)

    hs_ref[...] = hid_ref[...].reshape(FFT_K1 * FFT_GROUP, LANES)
    w = w_ref[0].astype(BF16)
    delta = delta_ref[...]
    n1 = lax.broadcasted_iota(jnp.int32, (FFT_K1, ns * LANES), 0)
    decay_slab = jnp.exp(-((n1 * FFT_N2).astype(F32) / float(length - 1)) * delta)

    def rows_of(j):
        hid = hs_ref[pl.ds(j, FFT_K1, stride=FFT_GROUP), :]
        decay_row = jnp.exp(-((g * FFT_GROUP + j).astype(F32) / float(length - 1)) * delta)
        f = jnp.dot(hid.astype(BF16), w, preferred_element_type=F32) * decay_slab * decay_row
        n_ref[0, 0:1, :] += jnp.sum(jnp.abs(f), axis=0, keepdims=True)
        return f

    _fft_a_core(rows_of, fa_ref, o_ref, s_refs, ns)


def _fft_a_scratch(ns):
    return [pltpu.VMEM((FFT_K1 * FFT_PITCH, LANES), F32)] * (2 * ns)


def _fft_a_vmem(ns):
    blk = FFT_K1 * FFT_GROUP * LANES
    return 2 * ns * blk * 4 + 2 * 2 * ns * blk * 2 + ns * blk * 4 + 2 * ns * FFT_K1 * FFT_PITCH * LANES * 4 + (8 << 20)


def fft_filter_pass_a(hid, w_o, length, dc=256):
    q, _, d = w_o.shape
    ns = dc // LANES
    delta = jnp.abs(jnp.linspace(MIN_DECAY, MAX_DECAY, d, dtype=F32))[None]
    hid3 = hid.reshape(FFT_K1, FFT_N2, LANES)
    fa = _phase_tables(length)[0]
    return pl.pallas_call(
        functools.partial(_fft_a_filter_kernel, ns=ns, length=length),
        grid=(q, d // dc, FFT_N2 // FFT_GROUP),
        in_specs=[pl.BlockSpec((FFT_K1, FFT_GROUP, LANES), lambda w, ci, g: (0, g, 0)),
                  pl.BlockSpec((1, LANES, dc), lambda w, ci, g: (w, 0, ci)),
                  pl.BlockSpec((1, dc), lambda w, ci, g: (0, ci)),
                  pl.BlockSpec(fa.shape, lambda w, ci, g: (0, 0))],
        out_specs=[pl.BlockSpec((1, FFT_K1, 2, FFT_GROUP, dc), lambda w, ci, g: (w, 0, 0, g, ci)),
                   pl.BlockSpec((1, F32_SUBLANES, dc), lambda w, ci, g: (w, 0, ci))],
        out_shape=[jax.ShapeDtypeStruct((q, FFT_K1, 2, FFT_N2, d), BF16),
                   jax.ShapeDtypeStruct((q, F32_SUBLANES, d), F32)],
        scratch_shapes=[pltpu.VMEM((FFT_K1 * FFT_GROUP, LANES), F32)] + _fft_a_scratch(ns),
        compiler_params=_params(("parallel", "parallel", "arbitrary"), _fft_a_vmem(ns)),
        name="fft_filter_pass_a",
    )(hid3, w_o, delta, fa)


def fft_pass_a(x, col0, d, dc=256):
    q, length, c = x.shape
    ns = dc // LANES
    x4 = x.reshape(q, FFT_K1, FFT_N2, c)
    slab0 = col0 // LANES
    in_specs = [pl.BlockSpec((None, FFT_K1, FFT_GROUP, LANES),
                             functools.partial(lambda w, ci, g, s: (w, 0, g, slab0 + ci * ns + s), s=s))
                for s in range(ns)]
    fa = _phase_tables(length)[0]
    in_specs.append(pl.BlockSpec(fa.shape, lambda w, ci, g: (0, 0)))
    return pl.pallas_call(
        functools.partial(_fft_a_kernel, ns=ns),
        grid=(q, d // dc, FFT_N2 // FFT_GROUP),
        in_specs=in_specs,
        out_specs=pl.BlockSpec((1, FFT_K1, 2, FFT_GROUP, dc), lambda w, ci, g: (w, 0, 0, g, ci)),
        out_shape=jax.ShapeDtypeStruct((q, FFT_K1, 2, FFT_N2, d), BF16),
        scratch_shapes=[pltpu.VMEM((FFT_K1 * FFT_GROUP, LANES), F32)] * ns + _fft_a_scratch(ns),
        compiler_params=_params(("parallel", "parallel", "parallel"), _fft_a_vmem(ns)),
        name="fft_pass_a",
    )(*([x4] * ns), fa)


def _fft_bk_kernel(af_ref, ab_ref, gb_ref, o_ref):
    for i in range(af_ref.shape[1]):
        g = gb_ref[i]
        uf = jnp.dot(g, af_ref[0, i], preferred_element_type=F32)
        ub = jnp.dot(g, ab_ref[0, i], preferred_element_type=F32)
        n2 = FFT_N2
        o_ref[0, i, 0:n2, :] = (uf[0:n2] + ub[0:n2]).astype(o_ref.dtype)
        o_ref[0, i, n2:2 * n2, :] = (uf[n2:2 * n2] - ub[n2:2 * n2]).astype(o_ref.dtype)


def _fft_b_vmem(kb, r, dc, n_blocks, n_tables):
    return 2 * n_blocks * kb * r * dc * 2 + 2 * n_tables * kb * r * r * 2 + 6 * r * dc * 4 + (4 << 20)


def fft_filter_spectrum(a, gb, kb=8, dc=512):
    q, k1, r, d = a.shape
    blk = lambda f: pl.BlockSpec((1, kb, r, dc), f)
    return pl.pallas_call(
        _fft_bk_kernel,
        grid=(q // 2, k1 // kb, d // dc),
        in_specs=[blk(lambda o, k, c: (2 * o, k, 0, c)), blk(lambda o, k, c: (2 * o + 1, k, 0, c)),
                  pl.BlockSpec((kb, r, r), lambda o, k, c: (k, 0, 0))],
        out_specs=blk(lambda o, k, c: (o, k, 0, c)),
        out_shape=jax.ShapeDtypeStruct((q // 2, k1, r, d), BF16),
        compiler_params=_params(("parallel", "parallel", "parallel"), _fft_b_vmem(kb, r, dc, 3, 1)),
        name="fft_filter_spectrum",
    )(a, a, gb)


def _fft_b_kernel(a_ref, k_ref, gb_ref, gbi_ref, o_ref):
    n2 = FFT_N2
    for i in range(a_ref.shape[0]):
        u = jnp.dot(gb_ref[i], a_ref[i], preferred_element_type=F32)
        ur, ui = u[0:n2], u[n2:2 * n2]
        kr, ki = k_ref[0, i, 0:n2, :].astype(F32), k_ref[0, i, n2:2 * n2, :].astype(F32)
        v = jnp.concatenate([ur * kr - ui * ki, ur * ki + ui * kr], axis=0).astype(BF16)
        o_ref[i] = jnp.dot(gbi_ref[i], v, preferred_element_type=F32).astype(o_ref.dtype)


def fft_pass_b(a, kspec, order, gb, gbi, kb=8, dc=512):
    k1, r, d = a.shape
    return pl.pallas_call(
        _fft_b_kernel,
        grid=(k1 // kb, d // dc),
        in_specs=[pl.BlockSpec((kb, r, dc), lambda k, c: (k, 0, c)),
                  pl.BlockSpec((1, kb, r, dc), lambda k, c: (order, k, 0, c)),
                  pl.BlockSpec((kb, r, r), lambda k, c: (k, 0, 0)),
                  pl.BlockSpec((kb, r, r), lambda k, c: (k, 0, 0))],
        out_specs=pl.BlockSpec((kb, r, dc), lambda k, c: (k, 0, c)),
        out_shape=jax.ShapeDtypeStruct((k1, r, d), BF16),
        compiler_params=_params(("parallel", "parallel"), _fft_b_vmem(kb, r, dc, 3, 2)),
        name="fft_pass_b",
    )(a, kspec, gb, gbi)


def _fft_c_kernel(*refs, ns):
    b_ref, ca_ref, gate_refs, z_refs = refs[0], refs[1], refs[2:2 + ns], refs[2 + ns:2 + 2 * ns]
    n_ref, bias_ref, o_ref = refs[2 + 2 * ns:5 + 2 * ns]
    s_refs = refs[5 + 2 * ns:5 + 4 * ns]
    t_refs = refs[5 + 4 * ns:]
    ca = ca_ref[...]

    def spread(k, carry):
        for p in range(2):
            for s in range(ns):
                s_refs[p * ns + s][_regroup_rows(k), :] = b_ref[k, p, :, s * LANES:(s + 1) * LANES].astype(F32)
        return carry

    lax.fori_loop(0, FFT_K1, spread, 0, unroll=8)
    for j in range(FFT_GROUP):
        b = jnp.concatenate([
            jnp.concatenate([s_refs[p * ns + s][pl.ds(j, FFT_K1, stride=FFT_PITCH), :] for s in range(ns)], axis=1)
            for p in range(2)], axis=0)
        y = jnp.dot(ca, b.astype(BF16), preferred_element_type=F32)
        for s in range(ns):
            t_refs[s][pl.ds(j, FFT_K1, stride=FFT_PITCH), :] = y[:, s * LANES:(s + 1) * LANES]

    def emit(k, carry):
        for s in range(ns):
            sl = slice(s * LANES, (s + 1) * LANES)
            y = t_refs[s][_regroup_rows(k), :] / (n_ref[0:1, sl] + n_ref[1:2, sl] + 1e-6)
            o_ref[k, :, sl] = (gate_refs[s][k].astype(F32) * (y + bias_ref[:, sl] * z_refs[s][k])).astype(o_ref.dtype)
        return carry

    lax.fori_loop(0, FFT_K1, emit, 0, unroll=8)


def fft_pass_c(bp, ca, gate_arr, gate_col0, z_arr, z_col0, nsum, bias, out_dtype, dc=256):
    k1, r, d = bp.shape
    ns = dc // LANES
    length = FFT_K1 * FFT_N2
    b4 = bp.reshape(k1, 2, FFT_N2, d)
    g3 = gate_arr.reshape(FFT_K1, FFT_N2, gate_arr.shape[1])
    z3 = z_arr.reshape(FFT_K1, FFT_N2, z_arr.shape[1])
    sig = lambda slab0: [pl.BlockSpec((FFT_K1, FFT_GROUP, LANES),
                                      functools.partial(lambda ci, g, s: (0, g, slab0 + ci * ns + s), s=s))
                         for s in range(ns)]
    row = pl.BlockSpec((1, dc), lambda ci, g: (0, ci))
    blk = FFT_K1 * FFT_GROUP * dc
    vmem = 2 * 2 * blk * 2 + 2 * 3 * blk * 4 + 3 * ns * FFT_K1 * FFT_PITCH * LANES * 4 + (8 << 20)
    out = pl.pallas_call(
        functools.partial(_fft_c_kernel, ns=ns),
        grid=(d // dc, FFT_N2 // FFT_GROUP),
        in_specs=[pl.BlockSpec((k1, 2, FFT_GROUP, dc), lambda ci, g: (0, 0, g, ci)),
                  pl.BlockSpec(ca.shape, lambda ci, g: (0, 0)),
                  *sig(gate_col0 // LANES), *sig(z_col0 // LANES),
                  pl.BlockSpec((2, dc), lambda ci, g: (0, ci)), row],
        out_specs=pl.BlockSpec((FFT_K1, FFT_GROUP, dc), lambda ci, g: (0, g, ci)),
        out_shape=jax.ShapeDtypeStruct((FFT_K1, FFT_N2, d), out_dtype),
        scratch_shapes=[pltpu.VMEM((FFT_K1 * FFT_PITCH, LANES), F32)] * (3 * ns),
        compiler_params=_params(("parallel", "parallel"), vmem),
        name="fft_pass_c",
    )(b4, ca, *([g3] * ns), *([z3] * ns), nsum, bias)
    return out.reshape(length, d)


def kernel(x, c, ctx, c_ctx, ada_w, ada_b, norm_g, mix_w_in, mix_conv_w, mix_sink, mix_w_out, ffn_w_gu, ffn_w_down, hy_w_in, hy_b_in, hy_conv_w, hy_conv_b, hf_w1, hf_b1, hf_freq, hf_w2, hf_b2, hf_w3, hf_b3, hf_w_out, hf_bias, hy_w_out, hy_b_out, router_w, router_b, moe_w_gu, moe_w_down):
    assert x.shape[0] == 1 and ada_w.shape[0] == 2
    s, d = x.shape[1], x.shape[2]
    xs = x[0]
    ctxs = ctx[0]

    acts = jnp.zeros((F32_SUBLANES, d), F32).at[0].set(c[0]).at[1].set(c_ctx)
    mods = adaln_all(acts, ada_w, ada_b)
    row = lambda layer, r, k: mods[layer, r:r + 1, k * d:(k + 1) * d]

    g = norm_g[0]
    h = norm_mod(xs, g[0:1], row(0, 0, 0), row(0, 0, 1))
    hc = norm_mod(ctxs, g[0:1], row(0, 1, 0), row(0, 1, 1))
    w_in = mix_w_in[0].astype(BF16)
    p, ffn_gu = matmul(h, w_in, out_dtype=BF16, side=ffn_w_gu[0], side_chunks=256)
    ckv = matmul(hc, w_in[:, KV_START:], out_dtype=BF16)
    cs, sn = rope_tables(s)
    y = even_mixer_core(p, ckv, cs, sn, mix_conv_w[0], mix_sink[0])
    out = matmul(y, mix_w_out[0].astype(BF16))
    xs, h = post_norm(xs, out, g[1:2], row(0, 0, 2), g[2:3], row(0, 0, 3), row(0, 0, 4))

    act, ffn_down = swiglu_up(h, ffn_gu, tm=2048, tn=256, side=ffn_w_down[0], side_chunks=344)
    out, hy_in = matmul(act, ffn_down, tm=512, tn=256, side=hy_w_in[0], side_chunks=256)
    g1 = norm_g[1]
    xs, h = post_norm(xs, out, g[3:4], row(0, 0, 5), g1[0:1], row(1, 0, 0), row(1, 0, 1))

    g = g1
    moe_gu_f32 = moe_w_gu[0].reshape(-1, moe_w_gu.shape[-1])
    u0, moe_gu = matmul(h, hy_in, bias=hy_b_in[0][None], out_dtype=BF16,
                        side=moe_gu_f32, side_chunks=256)
    x12 = short_conv(u0, hy_conv_w[0], hy_conv_b[0][None], 0, 2 * d, BF16)
    v = short_conv(u0, hy_conv_w[0], hy_conv_b[0][None], 2 * d, d, F32)
    hid = filter_hidden(s, hf_w1[0], hf_b1[0], hf_freq[0], hf_w2[0], hf_b2[0], hf_w3[0], hf_b3[0])
    w_o = hf_w_out[0].astype(F32).reshape(FILTER_HIDDEN, HYENA_ORDER * 2, d).transpose(1, 0, 2)
    w_o = jnp.zeros((HYENA_ORDER * 2, LANES, d), F32).at[:, :FILTER_HIDDEN].set(w_o)
    filt_a, fsum = fft_filter_pass_a(hid, w_o, s)
    fsum = fsum[:, 0, :]
    _, ca, gb, gbi = _phase_tables(s)
    spec = lambda a: a.reshape(a.shape[0], FFT_K1, 2 * FFT_N2, d)
    kspec = fft_filter_spectrum(spec(filt_a), gb)
    a = spec(fft_pass_a(v[None], 0, d))[0]
    z1 = fft_pass_c(fft_pass_b(a, kspec, 0, gb, gbi), ca, x12, 0, v, 0, fsum[0:2], hf_bias[0, 0][None], F32)
    a = spec(fft_pass_a(z1[None], 0, d))[0]
    z2 = fft_pass_c(fft_pass_b(a, kspec, 1, gb, gbi), ca, x12, d, z1, 0, fsum[2:4], hf_bias[0, 1][None], BF16)
    moe_down_f32 = moe_w_down[0].reshape(-1, moe_w_down.shape[-1])
    out, moe_down = matmul(z2, hy_w_out[0].astype(BF16), bias=hy_b_out[0][None], side=moe_down_f32, side_chunks=128)
    xs, sel, wts = post_router_top2(xs, out, g[1:2], row(1, 0, 2), g[2:3], row(1, 0, 3), row(1, 0, 4),
                                    router_w[0], router_b[0])
    pos, src_tok, tile_expert, n_valid = moe_plan(sel)
    act = grouped_swiglu_up(xs, src_tok, g[2:3], row(1, 0, 3), row(1, 0, 4), moe_gu.reshape(moe_w_gu.shape[1:]),
                            tile_expert, n_valid)
    ys = grouped_down(act, moe_down.reshape(moe_w_down.shape[1:]), tile_expert, n_valid)
    xs = moe_combine_residual(xs, wts, ys, pos, g[3:4], row(1, 0, 5))
    return xs[None]
```

```python
import functools
import math

import jax
import jax.numpy as jnp
import numpy as np
from jax import lax
from jax.experimental import pallas as pl
from jax.experimental.pallas import tpu as pltpu

F32 = jnp.float32
BF16 = jnp.bfloat16

D_MODEL = 4096
GRID_W = 64
HEAD_DIM = 128
CONV_WIDTH = D_MODEL // 2
N_HEADS = (D_MODEL // 2) // HEAD_DIM
N_KV_HEADS = N_HEADS // 4
GQA_GROUP = N_HEADS // N_KV_HEADS
Q_WIDTH = N_HEADS * HEAD_DIM
KV_WIDTH = N_KV_HEADS * HEAD_DIM
Q_START = 3 * CONV_WIDTH
KV_START = Q_START + Q_WIDTH
IN_WIDTH = KV_START + 2 * KV_WIDTH
BLOCK = 128
ATTN_SCALE = HEAD_DIM ** -0.5
ROPE_BASE = 10000.0
ROPE_PAIRS = HEAD_DIM // 4
HYENA_ORDER = 2
FILTER_EMB = 33
FILTER_BANDS = (FILTER_EMB - 1) // 2
FILTER_HIDDEN = 64
MIN_DECAY = math.log(1e-2) / 0.3
MAX_DECAY = math.log(1e-2) / 1.5
N_EXPERTS = 8
NORM_EPS = 1e-6

V7X_VMEM_BYTES = 64 * 1024 * 1024
LANES = 128
F32_SUBLANES = 8
BF16_SUBLANES = 16


def _params(semantics, vmem_bytes):
    limit = min(int(vmem_bytes), V7X_VMEM_BYTES - 4 * 1024 * 1024)
    return pltpu.CompilerParams(dimension_semantics=semantics, vmem_limit_bytes=limit)


def _sigmoid(v):
    return 1.0 / (1.0 + jnp.exp(-v))


def _adaln_kernel(a_ref, w_ref, b_ref, o_ref, acc_ref):
    k = pl.program_id(2)

    @pl.when(k == 0)
    def _():
        acc_ref[...] = jnp.zeros_like(acc_ref)

    a = a_ref[...]
    a = a * _sigmoid(a)
    acc_ref[...] += jnp.dot(a.astype(BF16), w_ref[0].astype(BF16), preferred_element_type=F32)

    @pl.when(k == pl.num_programs(2) - 1)
    def _():
        o_ref[0] = acc_ref[...] + b_ref[0]


def adaln_all(acts, ada_w, ada_b):
    depth, d, n = ada_w.shape
    tn, tk = 2048, 1024
    return pl.pallas_call(
        _adaln_kernel,
        grid=(depth, n // tn, d // tk),
        in_specs=[
            pl.BlockSpec((F32_SUBLANES, tk), lambda l, j, k: (0, k)),
            pl.BlockSpec((1, tk, tn), lambda l, j, k: (l, k, j)),
            pl.BlockSpec((1, 1, tn), lambda l, j, k: (l, 0, j)),
        ],
        out_specs=pl.BlockSpec((1, F32_SUBLANES, tn), lambda l, j, k: (l, 0, j)),
        out_shape=jax.ShapeDtypeStruct((depth, F32_SUBLANES, n), F32),
        scratch_shapes=[pltpu.VMEM((F32_SUBLANES, tn), F32)],
        compiler_params=_params(("parallel", "parallel", "arbitrary"), 2 * tk * tn * 4 + tk * tn * 2 + (4 << 20)),
        name="adaln",
    )(acts, ada_w, ada_b.reshape(depth, 1, n))


def _norm_mod(x, g, sh, sc):
    y = x * lax.rsqrt(jnp.mean(x * x, axis=-1, keepdims=True) + NORM_EPS)
    return (y * g) * (1.0 + sc) + sh


def _norm_mod_kernel(x_ref, g_ref, sh_ref, sc_ref, o_ref):
    o_ref[...] = _norm_mod(x_ref[...], g_ref[...], sh_ref[...], sc_ref[...]).astype(o_ref.dtype)


def norm_mod(x, g, shift, scale, tm=512):
    m, d = x.shape
    tm = min(tm, m)
    row = pl.BlockSpec((1, d), lambda i: (0, 0))
    return pl.pallas_call(
        _norm_mod_kernel,
        grid=(m // tm,),
        in_specs=[pl.BlockSpec((tm, d), lambda i: (i, 0)), row, row, row],
        out_specs=pl.BlockSpec((tm, d), lambda i: (i, 0)),
        out_shape=jax.ShapeDtypeStruct((m, d), BF16),
        compiler_params=_params(("parallel",), 2 * tm * d * 4 + 2 * tm * d * 2 + 3 * tm * d * 4 + (2 << 20)),
        name="norm_mod",
    )(x, g, shift, scale)


def _post_norm_kernel(x_ref, y_ref, g_ref, gt_ref, g2_ref, sh_ref, sc_ref, o_ref, h_ref):
    y = y_ref[...]
    yn = y * lax.rsqrt(jnp.mean(y * y, axis=-1, keepdims=True) + NORM_EPS)
    x = x_ref[...] + gt_ref[...] * (yn * g_ref[...])
    o_ref[...] = x
    h_ref[...] = _norm_mod(x, g2_ref[...], sh_ref[...], sc_ref[...]).astype(h_ref.dtype)


def post_norm(x, y, g, gate, g_next, shift, scale, tm=256):
    m, d = x.shape
    row = pl.BlockSpec((1, d), lambda i: (0, 0))
    blk = pl.BlockSpec((tm, d), lambda i: (i, 0))
    return pl.pallas_call(
        _post_norm_kernel,
        grid=(m // tm,),
        in_specs=[blk, blk, row, row, row, row, row],
        out_specs=[blk, blk],
        out_shape=[jax.ShapeDtypeStruct((m, d), F32), jax.ShapeDtypeStruct((m, d), BF16)],
        compiler_params=_params(("parallel",), 7 * tm * d * 4 + 5 * tm * d * 4 + (2 << 20)),
        name="post_norm",
    )(x, y, g, gate, g_next, shift, scale)


def _side_cast(src_ref, dst_ref, step, n_chunks):
    @pl.when(step < n_chunks)
    def _():
        dst_ref[...] = src_ref[...].astype(dst_ref.dtype)


def _side_cast_plumbing(side, n_chunks, step_of):
    rows, cols = side.shape
    spec = pl.BlockSpec((rows // n_chunks, cols), lambda *g: (jnp.minimum(step_of(*g), n_chunks - 1), 0))
    vmem = 2 * (rows // n_chunks) * cols * (4 + 2)
    return spec, jax.ShapeDtypeStruct((rows, cols), BF16), vmem


def _mm_kernel(*refs, has_bias, side_chunks):
    a_ref, w_ref = refs[0], refs[1]
    n_in = 2 + has_bias + (side_chunks > 0)
    o_ref = refs[n_in]
    acc = jnp.dot(a_ref[...], w_ref[...], preferred_element_type=F32)
    if has_bias:
        acc = acc + refs[2][...]
    o_ref[...] = acc.astype(o_ref.dtype)
    if side_chunks:
        step = pl.program_id(0) * pl.num_programs(1) + pl.program_id(1)
        _side_cast(refs[n_in - 1], refs[n_in + 1], step, side_chunks)


def matmul(a, w, bias=None, out_dtype=F32, tm=1024, tn=512, side=None, side_chunks=0):
    m, k = a.shape
    n = w.shape[1]
    tm, tn = min(tm, m), min(tn, n)
    nj = n // tn
    in_specs = [pl.BlockSpec((tm, k), lambda i, j: (i, 0)), pl.BlockSpec((k, tn), lambda i, j: (0, j))]
    args = [a, w]
    if bias is not None:
        in_specs.append(pl.BlockSpec((1, tn), lambda i, j: (0, j)))
        args.append(bias)
    osz = jnp.dtype(out_dtype).itemsize
    vmem = 2 * tm * k * 2 + 2 * k * tn * 2 + 2 * tm * tn * osz + 2 * tm * tn * 4 + (2 << 20)
    out_specs = pl.BlockSpec((tm, tn), lambda i, j: (i, j))
    out_shape = jax.ShapeDtypeStruct((m, n), out_dtype)
    if side is not None:
        assert side_chunks <= (m // tm) * nj
        spec, shape, side_vmem = _side_cast_plumbing(side, side_chunks, lambda i, j: i * nj + j)
        in_specs.append(spec)
        args.append(side)
        out_specs, out_shape, vmem = [out_specs, spec], [out_shape, shape], vmem + side_vmem
    return pl.pallas_call(
        functools.partial(_mm_kernel, has_bias=bias is not None, side_chunks=side_chunks),
        grid=(m // tm, nj),
        in_specs=in_specs,
        out_specs=out_specs,
        out_shape=out_shape,
        compiler_params=_params(("arbitrary", "arbitrary") if side is not None else ("parallel", "arbitrary"), vmem),
        name="matmul",
    )(*args)


def _swiglu_up_kernel(a_ref, wg_ref, wu_ref, side_ref, o_ref, side_o_ref, *, side_chunks):
    a = a_ref[...]
    g = jnp.dot(a, wg_ref[...], preferred_element_type=F32)
    u = jnp.dot(a, wu_ref[...], preferred_element_type=F32)
    o_ref[...] = (g * _sigmoid(g) * u).astype(o_ref.dtype)
    _side_cast(side_ref, side_o_ref, pl.program_id(0) * pl.num_programs(1) + pl.program_id(1), side_chunks)


def swiglu_up(a, w_gu, tm, tn, side, side_chunks):
    m, k = a.shape
    f = w_gu.shape[1] // 2
    nj = f // tn
    assert side_chunks <= (m // tm) * nj
    spec, shape, side_vmem = _side_cast_plumbing(side, side_chunks, lambda i, j: i * nj + j)
    vmem = 2 * tm * k * 2 + 4 * k * tn * 2 + 2 * tm * tn * 2 + 4 * tm * tn * 4 + side_vmem + (2 << 20)
    return pl.pallas_call(
        functools.partial(_swiglu_up_kernel, side_chunks=side_chunks),
        grid=(m // tm, nj),
        in_specs=[
            pl.BlockSpec((tm, k), lambda i, j: (i, 0)),
            pl.BlockSpec((k, tn), lambda i, j: (0, j)),
            pl.BlockSpec((k, tn), lambda i, j: (0, j + nj)),
            spec,
        ],
        out_specs=[pl.BlockSpec((tm, tn), lambda i, j: (i, j)), spec],
        out_shape=[jax.ShapeDtypeStruct((m, f), BF16), shape],
        compiler_params=_params(("arbitrary", "arbitrary"), vmem),
        name="swiglu_up",
    )(a, w_gu, w_gu, side)


def _post_router_kernel(x_ref, y_ref, gp_ref, gt_ref, g_ref, sh_ref, sc_ref, w_ref, b_ref, o_ref, sel_ref, wts_ref):
    y = y_ref[...]
    yn = y * lax.rsqrt(jnp.mean(y * y, axis=-1, keepdims=True) + NORM_EPS)
    x = x_ref[...] + gt_ref[...] * (yn * gp_ref[...])
    o_ref[...] = x
    h = _norm_mod(x, g_ref[...], sh_ref[...], sc_ref[...])
    logits = jnp.dot(h, w_ref[...], preferred_element_type=F32, precision=lax.Precision.HIGHEST) + b_ref[...]
    lane = lax.broadcasted_iota(jnp.int32, logits.shape, 1)
    neg = jnp.float32(-jnp.inf)
    logits = jnp.where(lane < N_EXPERTS, logits, neg)
    v1 = jnp.max(logits, axis=1, keepdims=True)
    i1 = jnp.min(jnp.where(logits == v1, lane, LANES), axis=1, keepdims=True)
    rest = jnp.where(lane == i1, neg, logits)
    v2 = jnp.max(rest, axis=1, keepdims=True)
    i2 = jnp.min(jnp.where(rest == v2, lane, LANES), axis=1, keepdims=True)
    e2 = jnp.exp(v2 - v1)
    w1 = 1.0 / (1.0 + e2)
    w2 = e2 / (1.0 + e2)
    sel_ref[...] = jnp.where(lane == 0, i1, jnp.where(lane == 1, i2, 0))
    wts_ref[...] = jnp.where(lane == 0, w1, jnp.where(lane == 1, w2, 0.0))


def post_router_top2(x, y, g_post, gate, g, shift, scale, router_w, router_b, tm=256):
    m, d = x.shape
    wp = jnp.zeros((d, LANES), F32).at[:, :N_EXPERTS].set(router_w)
    bp = jnp.zeros((1, LANES), F32).at[0, :N_EXPERTS].set(router_b)
    row = pl.BlockSpec((1, d), lambda i: (0, 0))
    blk = pl.BlockSpec((tm, d), lambda i: (i, 0))
    out = pl.BlockSpec((tm, LANES), lambda i: (i, 0))
    return pl.pallas_call(
        _post_router_kernel,
        grid=(m // tm,),
        in_specs=[blk, blk, row, row, row, row, row,
                  pl.BlockSpec((d, LANES), lambda i: (0, 0)), pl.BlockSpec((1, LANES), lambda i: (0, 0))],
        out_specs=[blk, out, out],
        out_shape=[jax.ShapeDtypeStruct((m, d), F32), jax.ShapeDtypeStruct((m, LANES), jnp.int32),
                   jax.ShapeDtypeStruct((m, LANES), F32)],
        compiler_params=_params(("parallel",), 6 * tm * d * 4 + 6 * tm * d * 4 + 2 * d * LANES * 4 + (4 << 20)),
        name="post_router",
    )(x, y, g_post, gate, g, shift, scale, wp, bp)


MOE_TILE = 512


def moe_plan(sel):
    t = sel.shape[0]
    npairs = 2 * t
    rows = npairs + N_EXPERTS * MOE_TILE
    e_flat = sel[:, :2].reshape(npairs)
    order = jnp.argsort(e_flat, stable=True).astype(jnp.int32)
    rank = jnp.argsort(order).astype(jnp.int32)
    counts = jnp.sum((e_flat[:, None] == jnp.arange(N_EXPERTS, dtype=jnp.int32)[None, :]).astype(jnp.int32), axis=0)
    padded = (counts + MOE_TILE - 1) // MOE_TILE * MOE_TILE
    ends_p = jnp.cumsum(padded)
    starts_p = ends_p - padded
    starts_u = jnp.cumsum(counts) - counts
    pos = starts_p[e_flat] + rank - starts_u[e_flat]
    tile_start = jnp.arange(rows // MOE_TILE, dtype=jnp.int32) * MOE_TILE
    tile_expert = jnp.minimum(jnp.searchsorted(ends_p, tile_start, side="right"), N_EXPERTS - 1).astype(jnp.int32)
    n_valid = (ends_p[-1] // MOE_TILE).astype(jnp.int32).reshape(1)
    r = jnp.arange(rows, dtype=jnp.int32)
    e_row = jnp.repeat(tile_expert, MOE_TILE)
    in_group = r - starts_p[e_row]
    valid = (in_group < counts[e_row]) & (r < ends_p[-1])
    src_tok = jnp.where(valid, order[jnp.clip(starts_u[e_row] + in_group, 0, npairs - 1)] // 2, 0)
    return pos.astype(jnp.int32), src_tok.astype(jnp.int32), tile_expert, n_valid


MOE_GATHER_CHUNKS = 4


def _gather_rows(idx_ref, base, stride, src_hbm, dst_ref, sem, r0, n, wait):
    def body(r, carry):
        copy = pltpu.make_async_copy(src_hbm.at[pl.ds(idx_ref[base + stride * r], 1)], dst_ref.at[pl.ds(r, 1)], sem)
        if wait:
            copy.wait()
        else:
            copy.start()
        return carry

    lax.fori_loop(r0, r0 + n, body, 0, unroll=8)


def _gswiglu_kernel(te_ref, nv_ref, src_ref, x_hbm, g_ref, sh_ref, sc_ref, wg_ref, wu_ref, o_ref,
                    xbuf_ref, a_ref, sems):
    i, j = pl.program_id(0), pl.program_id(1)
    tm = a_ref.shape[0]
    n_valid = nv_ref[0]
    slot = i % 2

    def gather(tile, to_slot, wait):
        @pl.when(tile < n_valid)
        def _():
            _gather_rows(src_ref, tile * tm, 1, x_hbm, xbuf_ref.at[to_slot], sems.at[to_slot], 0, tm, wait=wait)

    @pl.when(j == 0)
    def _():
        @pl.when(i == 0)
        def _():
            gather(i, slot, False)

        gather(i, slot, True)
        gather(i + 1, 1 - slot, False)

        @pl.when(i < n_valid)
        def _():
            a_ref[...] = _norm_mod(xbuf_ref[slot], g_ref[...], sh_ref[...], sc_ref[...]).astype(a_ref.dtype)

    @pl.when(i < n_valid)
    def _():
        a = a_ref[...]
        g = jnp.dot(a, wg_ref[0], preferred_element_type=F32)
        u = jnp.dot(a, wu_ref[0], preferred_element_type=F32)
        o_ref[...] = (g * _sigmoid(g) * u).astype(o_ref.dtype)

    @pl.when(i >= n_valid)
    def _():
        o_ref[...] = jnp.zeros_like(o_ref)


def grouped_swiglu_up(x, src_tok, g, shift, scale, w_gu, tile_expert, n_valid, tn=512):
    k = x.shape[1]
    rows = src_tok.shape[0]
    f = w_gu.shape[2] // 2
    nj = f // tn
    tm = MOE_TILE
    row = pl.BlockSpec((1, k), lambda i, j, te, nv, src: (0, 0))
    vmem = 2 * tm * k * 4 + tm * k * 2 + 3 * tm * k * 4 + 4 * k * tn * 2 + 2 * tm * tn * 2 + 4 * tm * tn * 4 + (2 << 20)
    return pl.pallas_call(
        _gswiglu_kernel,
        grid_spec=pltpu.PrefetchScalarGridSpec(
            num_scalar_prefetch=3,
            grid=(rows // tm, nj),
            in_specs=[
                pl.BlockSpec(memory_space=pl.ANY), row, row, row,
                pl.BlockSpec((1, k, tn), lambda i, j, te, nv, src: (te[i], 0, j)),
                pl.BlockSpec((1, k, tn), lambda i, j, te, nv, src: (te[i], 0, j + nj)),
            ],
            out_specs=pl.BlockSpec((tm, tn), lambda i, j, te, nv, src: (i, j)),
            scratch_shapes=[pltpu.VMEM((2, tm, k), F32), pltpu.VMEM((tm, k), BF16), pltpu.SemaphoreType.DMA((2,))],
        ),
        out_shape=jax.ShapeDtypeStruct((rows, f), BF16),
        compiler_params=_params(("arbitrary", "arbitrary"), vmem),
        name="grouped_swiglu_up",
    )(tile_expert, n_valid, src_tok, x, g, shift, scale, w_gu, w_gu)


def _gdown_kernel(te_ref, nv_ref, a_ref, w_ref, o_ref):
    valid = pl.program_id(0) < nv_ref[0]

    @pl.when(valid)
    def _():
        o_ref[...] = jnp.dot(a_ref[...], w_ref[0], preferred_element_type=F32)

    @pl.when(jnp.logical_not(valid))
    def _():
        o_ref[...] = jnp.zeros_like(o_ref)


def grouped_down(act, w_down, tile_expert, n_valid, tn=1024):
    rows, f = act.shape
    n = w_down.shape[2]
    tm = MOE_TILE
    vmem = 2 * tm * f * 2 + 2 * f * tn * 2 + 2 * tm * tn * 4 + 2 * tm * tn * 4 + (2 << 20)
    return pl.pallas_call(
        _gdown_kernel,
        grid_spec=pltpu.PrefetchScalarGridSpec(
            num_scalar_prefetch=2,
            grid=(rows // tm, n // tn),
            in_specs=[
                pl.BlockSpec((tm, f), lambda i, j, te, nv: (i, 0)),
                pl.BlockSpec((1, f, tn), lambda i, j, te, nv: (te[i], 0, j)),
            ],
            out_specs=pl.BlockSpec((tm, tn), lambda i, j, te, nv: (i, j)),
        ),
        out_shape=jax.ShapeDtypeStruct((rows, n), F32),
        compiler_params=_params(("parallel", "arbitrary"), vmem),
        name="grouped_down",
    )(tile_expert, n_valid, act, w_down)


def _moe_combine_kernel(pos_ref, x_ref, w_ref, ys_hbm, g_ref, gt_ref, o_ref, buf_ref, sems):
    tm = x_ref.shape[0]
    rc = tm // MOE_GATHER_CHUNKS
    base = 2 * pl.program_id(0) * tm
    for wait in (False, True):
        for c in range(MOE_GATHER_CHUNKS):
            for slot in range(2):
                _gather_rows(pos_ref, base + slot, 2, ys_hbm, buf_ref.at[slot], sems.at[c], c * rc, rc, wait=wait)
            if wait:
                rows = pl.ds(c * rc, rc)
                w = w_ref[rows, :]
                y = w[:, 0:1] * buf_ref[0, rows, :] + w[:, 1:2] * buf_ref[1, rows, :]
                yn = y * lax.rsqrt(jnp.mean(y * y, axis=-1, keepdims=True) + NORM_EPS)
                o_ref[rows, :] = x_ref[rows, :] + gt_ref[...] * (yn * g_ref[...])


def moe_combine_residual(x, wts, ys, pos, g, gate, tm=128):
    t, d = x.shape
    row = pl.BlockSpec((1, d), lambda i, p: (0, 0))
    return pl.pallas_call(
        _moe_combine_kernel,
        grid_spec=pltpu.PrefetchScalarGridSpec(
            num_scalar_prefetch=1,
            grid=(t // tm,),
            in_specs=[pl.BlockSpec((tm, d), lambda i, p: (i, 0)), pl.BlockSpec((tm, LANES), lambda i, p: (i, 0)),
                      pl.BlockSpec(memory_space=pl.ANY), row, row],
            out_specs=pl.BlockSpec((tm, d), lambda i, p: (i, 0)),
            scratch_shapes=[pltpu.VMEM((2, tm, d), F32), pltpu.SemaphoreType.DMA((MOE_GATHER_CHUNKS,))],
        ),
        out_shape=jax.ShapeDtypeStruct((t, d), F32),
        compiler_params=_params(("arbitrary",), 10 * tm * d * 4 + (2 << 20)),
        name="moe_combine_residual",
    )(pos, x, wts, ys, g, gate)


def _rope(x, cs, sn):
    lane = lax.broadcasted_iota(jnp.int32, x.shape, 1)
    first = (lane % (2 * ROPE_PAIRS)) < ROPE_PAIRS
    partner = jnp.where(first, pltpu.roll(x, HEAD_DIM - ROPE_PAIRS, 1), pltpu.roll(x, ROPE_PAIRS, 1))
    return x * cs + partner * sn


def _shift_rows(u, halo_prev, halo_next):
    n = u.shape[0]
    row = lax.broadcasted_iota(jnp.int32, u.shape, 0)
    prev = jnp.where(row == 0, halo_prev, pltpu.roll(u, 1, 0))
    nxt = jnp.where(row == n - 1, halo_next, pltpu.roll(u, n - 1, 0))
    return prev, nxt


def _mixer_kernel(sink_ref, p_ref, kvp_ref, kvn_ref, cp_ref, hp_ref, cn_ref, hn_ref, ckv_ref,
                  cs_ref, sn_ref, csp_ref, snp_ref, csn_ref, snn_ref, cw_ref, o_ref):
    i = pl.program_id(0)
    nb = pl.num_programs(0)
    cs, sn = cs_ref[...], sn_ref[...]

    last = BF16_SUBLANES - 1
    for c0 in range(0, CONV_WIDTH, LANES):
        cl = slice(c0, c0 + LANES)
        cc = slice(CONV_WIDTH + c0, CONV_WIDTH + c0 + LANES)
        ch = slice(2 * CONV_WIDTH + c0, 2 * CONV_WIDTH + c0 + LANES)
        cu = p_ref[:, cc].astype(F32) * p_ref[:, ch].astype(F32)
        halo_p = cp_ref[last:last + 1, cl].astype(F32) * hp_ref[last:last + 1, cl].astype(F32)
        halo_n = cn_ref[0:1, cl].astype(F32) * hn_ref[0:1, cl].astype(F32)
        halo_p = jnp.where(i > 0, halo_p, 0.0)
        halo_n = jnp.where(i < nb - 1, halo_n, 0.0)
        cu_prev, cu_next = _shift_rows(cu, halo_p, halo_n)
        y_conv = p_ref[:, cl].astype(F32) * (cu_prev * cw_ref[0:1, cl] + cu * cw_ref[1:2, cl] + cu_next * cw_ref[2:3, cl])
        o_ref[:, cl] = y_conv.astype(o_ref.dtype)

    n_ctx = ckv_ref.shape[0]
    n_keys = 3 * BLOCK + n_ctx
    qi = lax.broadcasted_iota(jnp.int32, (BLOCK, n_keys), 0)
    kj = lax.broadcasted_iota(jnp.int32, (BLOCK, n_keys), 1)
    key_pos = i * BLOCK + kj - BLOCK
    visible = (kj >= 3 * BLOCK) | ((jnp.abs(qi + BLOCK - kj) <= BLOCK) & (key_pos >= 0) & (key_pos < nb * BLOCK))
    nt = (((1,), (1,)), ((), ()))

    for kh in range(N_KV_HEADS):
        ko = KV_START + kh * HEAD_DIM
        vo = KV_START + KV_WIDTH + kh * HEAD_DIM
        kb = kh * HEAD_DIM
        vb = KV_WIDTH + kh * HEAD_DIM
        keys = jnp.concatenate([
            _rope(kvp_ref[:, kb:kb + HEAD_DIM].astype(F32), csp_ref[...], snp_ref[...]).astype(BF16),
            _rope(p_ref[:, ko:ko + HEAD_DIM].astype(F32), cs, sn).astype(BF16),
            _rope(kvn_ref[:, kb:kb + HEAD_DIM].astype(F32), csn_ref[...], snn_ref[...]).astype(BF16),
            ckv_ref[:, kb:kb + HEAD_DIM],
        ], axis=0)
        vals = jnp.concatenate([kvp_ref[:, vb:vb + HEAD_DIM], p_ref[:, vo:vo + HEAD_DIM],
                                kvn_ref[:, vb:vb + HEAD_DIM], ckv_ref[:, vb:vb + HEAD_DIM]], axis=0)
        for g in range(GQA_GROUP):
            h = kh * GQA_GROUP + g
            qo = Q_START + h * HEAD_DIM
            q = (_rope(p_ref[:, qo:qo + HEAD_DIM].astype(F32), cs, sn) * ATTN_SCALE).astype(BF16)
            sc = jnp.where(visible, lax.dot_general(q, keys, nt, preferred_element_type=F32), -jnp.inf)
            sink = sink_ref[h]
            m = jnp.maximum(jnp.max(sc, axis=1, keepdims=True), sink)
            e = jnp.exp(sc - m)
            denom = jnp.exp(sink - m) + jnp.sum(e, axis=1, keepdims=True)
            o = jnp.dot(e.astype(BF16), vals, preferred_element_type=F32) / denom
            oo = CONV_WIDTH + h * HEAD_DIM
            o_ref[:, oo:oo + HEAD_DIM] = o.astype(o_ref.dtype)


def even_mixer_core(p, ckv, cs, sn, conv_w, sink):
    s = p.shape[0]
    nb = s // BLOCK
    hb = BLOCK // BF16_SUBLANES
    kvc = KV_START // (2 * KV_WIDTH)
    prev = lambda i: jnp.maximum(i - 1, 0)
    nxt = lambda i: jnp.minimum(i + 1, nb - 1)
    tab = lambda f: pl.BlockSpec((BLOCK, HEAD_DIM), lambda i: (f(i), 0))
    same = lambda i: i
    in_specs = [
        pl.BlockSpec(memory_space=pltpu.SMEM),
        pl.BlockSpec((BLOCK, IN_WIDTH), lambda i: (i, 0)),
        pl.BlockSpec((BLOCK, 2 * KV_WIDTH), lambda i: (prev(i), kvc)),
        pl.BlockSpec((BLOCK, 2 * KV_WIDTH), lambda i: (nxt(i), kvc)),
        pl.BlockSpec((BF16_SUBLANES, CONV_WIDTH), lambda i: (jnp.maximum(i * hb - 1, 0), 1)),
        pl.BlockSpec((BF16_SUBLANES, CONV_WIDTH), lambda i: (jnp.maximum(i * hb - 1, 0), 2)),
        pl.BlockSpec((BF16_SUBLANES, CONV_WIDTH), lambda i: (jnp.minimum((i + 1) * hb, nb * hb - 1), 1)),
        pl.BlockSpec((BF16_SUBLANES, CONV_WIDTH), lambda i: (jnp.minimum((i + 1) * hb, nb * hb - 1), 2)),
        pl.BlockSpec(ckv.shape, lambda i: (0, 0)),
        tab(same), tab(same), tab(prev), tab(prev), tab(nxt), tab(nxt),
        pl.BlockSpec(conv_w.shape, lambda i: (0, 0)),
    ]
    return pl.pallas_call(
        _mixer_kernel,
        grid=(nb,),
        in_specs=in_specs,
        out_specs=pl.BlockSpec((BLOCK, CONV_WIDTH + Q_WIDTH), lambda i: (i, 0)),
        out_shape=jax.ShapeDtypeStruct((s, CONV_WIDTH + Q_WIDTH), BF16),
        compiler_params=_params(("parallel",), 32 << 20),
        name="even_mixer_core",
    )(sink, p, p, p, p, p, p, p, ckv, cs, sn, cs, sn, cs, sn, conv_w)


def rope_tables(s):
    t = np.arange(s)
    inv = ROPE_BASE ** (-np.arange(ROPE_PAIRS, dtype=np.float64) / ROPE_PAIRS)
    ang_r = (t // GRID_W)[:, None] * inv[None, :]
    ang_c = (t % GRID_W)[:, None] * inv[None, :]
    cs = np.concatenate([np.cos(ang_r)] * 2 + [np.cos(ang_c)] * 2, axis=1)
    sn = np.concatenate([-np.sin(ang_r), np.sin(ang_r), -np.sin(ang_c), np.sin(ang_c)], axis=1)
    return cs.astype(np.float32), sn.astype(np.float32)


def _sconv_kernel(u_ref, up_ref, un_ref, w_ref, b_ref, o_ref):
    i = pl.program_id(0)
    u = u_ref[...].astype(F32)
    last = up_ref.shape[0] - 1
    halo_p = jnp.where(i > 0, up_ref[last:last + 1, :].astype(F32), 0.0)
    halo_n = jnp.where(i < pl.num_programs(0) - 1, un_ref[0:1, :].astype(F32), 0.0)
    prev, nxt = _shift_rows(u, halo_p, halo_n)
    o_ref[...] = (prev * w_ref[0:1, :] + u * w_ref[1:2, :] + nxt * w_ref[2:3, :] + b_ref[...]).astype(o_ref.dtype)


def short_conv(u, w, b, col0, width, out_dtype, tm=256, tn=2048):
    s = u.shape[0]
    hr = BF16_SUBLANES
    hb = tm // hr
    nh = s // hr
    j0 = col0 // tn
    return pl.pallas_call(
        _sconv_kernel,
        grid=(s // tm, width // tn),
        in_specs=[
            pl.BlockSpec((tm, tn), lambda i, j: (i, j + j0)),
            pl.BlockSpec((hr, tn), lambda i, j: (jnp.maximum(i * hb - 1, 0), j + j0)),
            pl.BlockSpec((hr, tn), lambda i, j: (jnp.minimum((i + 1) * hb, nh - 1), j + j0)),
            pl.BlockSpec((3, tn), lambda i, j: (0, j + j0)),
            pl.BlockSpec((1, tn), lambda i, j: (0, j + j0)),
        ],
        out_specs=pl.BlockSpec((tm, tn), lambda i, j: (i, j)),
        out_shape=jax.ShapeDtypeStruct((s, width), out_dtype),
        compiler_params=_params(("parallel", "parallel"), 10 * tm * tn * 4 + (2 << 20)),
        name="short_conv",
    )(u, u, u, w, b)


def _hid_kernel(z_ref, w1_ref, b1_ref, w2_ref, b2_ref, w3_ref, b3_ref, fr_ref, o_ref):
    hp = lax.Precision.HIGHEST
    fr = fr_ref[...]
    h = jnp.sin(fr * (jnp.dot(z_ref[...], w1_ref[...], preferred_element_type=F32, precision=hp) + b1_ref[...]))
    h = jnp.sin(fr * (jnp.dot(h, w2_ref[...], preferred_element_type=F32, precision=hp) + b2_ref[...]))
    o_ref[...] = jnp.sin(fr * (jnp.dot(h, w3_ref[...], preferred_element_type=F32, precision=hp) + b3_ref[...]))


def _pad2(a, r, c):
    return jnp.zeros((r, c), F32).at[:a.shape[0], :a.shape[1]].set(a.astype(F32))


def filter_hidden(length, w1, b1, freq, w2, b2, w3, b3, tl=2048):
    t = np.linspace(0.0, 1.0, length)[:, None]
    w = (2.0 * math.pi / length) * np.arange(length)[:, None]
    bands = np.linspace(1e-4, FILTER_BANDS - 1, FILTER_BANDS)[None]
    z = np.concatenate([t, np.cos(bands * w), -np.sin(bands * w)], axis=-1)
    zp = np.zeros((length, LANES), np.float32)
    zp[:, :FILTER_EMB] = z
    full = pl.BlockSpec((LANES, LANES), lambda i: (0, 0))
    row = pl.BlockSpec((1, LANES), lambda i: (0, 0))
    return pl.pallas_call(
        _hid_kernel,
        grid=(length // tl,),
        in_specs=[pl.BlockSpec((tl, LANES), lambda i: (i, 0)), full, row, full, row, full, row, row],
        out_specs=pl.BlockSpec((tl, LANES), lambda i: (i, 0)),
        out_shape=jax.ShapeDtypeStruct((length, LANES), F32),
        compiler_params=_params(("parallel",), 16 << 20),
        name="filter_hidden",
    )(zp, _pad2(w1, LANES, LANES), _pad2(b1[None], 1, LANES), _pad2(w2, LANES, LANES), _pad2(b2[None], 1, LANES),
      _pad2(w3, LANES, LANES), _pad2(b3[None], 1, LANES), _pad2(freq[None], 1, LANES))


FFT_N1 = 256
FFT_N2 = 128
FFT_K1 = FFT_N1 // 2
FFT_GROUP = BF16_SUBLANES
FFT_PITCH = 3 * F32_SUBLANES


@functools.lru_cache(maxsize=None)
def _phase_tables(length):
    n1h, n2 = FFT_K1, FFT_N2
    assert length == n1h * n2
    n = 2 * length
    ia = np.arange(n1h, dtype=np.int64)
    tha = ((ia[None, :] * (2 * ia[:, None] + 1)) % (2 * FFT_N1)) * (2.0 * np.pi / (2 * FFT_N1))
    fa = np.concatenate([np.cos(tha), -np.sin(tha)], axis=0)
    ca = np.concatenate([np.cos(tha).T, -np.sin(tha).T], axis=1) * (2.0 / n)
    k1 = ia[:, None, None]
    k2 = np.arange(n2, dtype=np.int64)[None, :, None]
    m2 = np.arange(n2, dtype=np.int64)[None, None, :]
    phb = ((m2 * (2 * (k1 + FFT_N1 * k2) + 1)) % (2 * n)) * (2.0 * np.pi / (2 * n))
    gr, gi = np.cos(phb), -np.sin(phb)
    gb = np.concatenate([np.concatenate([gr, -gi], axis=2), np.concatenate([gi, gr], axis=2)], axis=1)
    hr, hi = np.swapaxes(gr, 1, 2), -np.swapaxes(gi, 1, 2)
    gbi = np.concatenate([np.concatenate([hr, -hi], axis=2), np.concatenate([hi, hr], axis=2)], axis=1)
    return tuple(t.astype(np.float32).astype(BF16) for t in (fa, ca, gb, gbi))


def _regroup_rows(k):
    return pl.ds(pl.multiple_of(k * FFT_PITCH, F32_SUBLANES), FFT_GROUP)


def _fft_a_core(rows_of, fa_ref, o_ref, s_refs, ns):
    fa = fa_ref[...]
    for j in range(FFT_GROUP):
        res = jnp.dot(fa, rows_of(j).astype(BF16), preferred_element_type=F32)
        for p in range(2):
            for s in range(ns):
                s_refs[p * ns + s][pl.ds(j, FFT_K1, stride=FFT_PITCH), :] = (
                    res[p * FFT_K1:(p + 1) * FFT_K1, s * LANES:(s + 1) * LANES])

    def emit(k, carry):
        for p in range(2):
            for s in range(ns):
                o_ref[0, k, p, :, s * LANES:(s + 1) * LANES] = s_refs[p * ns + s][_regroup_rows(k), :].astype(o_ref.dtype)
        return carry

    lax.fori_loop(0, FFT_K1, emit, 0, unroll=8)


def _fft_a_kernel(*refs, ns):
    x_refs, fa_ref, o_ref = refs[:ns], refs[ns], refs[ns + 1]
    xs_refs = refs[ns + 2:2 * ns + 2]
    for s in range(ns):
        xs_refs[s][...] = x_refs[s][...].reshape(FFT_K1 * FFT_GROUP, LANES)
    rows_of = lambda j: jnp.concatenate([r[pl.ds(j, FFT_K1, stride=FFT_GROUP), :] for r in xs_refs], axis=1)
    _fft_a_core(rows_of, fa_ref, o_ref, refs[2 * ns + 2:], ns)


def _fft_a_filter_kernel(hid_ref, w_ref, delta_ref, fa_ref, o_ref, n_ref, hs_ref, *s_refs, ns, length):
    g = pl.program_id(2)

    @pl.when(g == 0)
    def _():
        n_ref[...] = jnp.zeros_like(n_ref)

    hs_ref[...] = hid_ref[...].reshape(FFT_K1 * FFT_GROUP, LANES)
    w = w_ref[0].astype(BF16)
    delta = delta_ref[...]
    n1 = lax.broadcasted_iota(jnp.int32, (FFT_K1, ns * LANES), 0)
    decay_slab = jnp.exp(-((n1 * FFT_N2).astype(F32) / float(length - 1)) * delta)

    def rows_of(j):
        hid = hs_ref[pl.ds(j, FFT_K1, stride=FFT_GROUP), :]
        decay_row = jnp.exp(-((g * FFT_GROUP + j).astype(F32) / float(length - 1)) * delta)
        f = jnp.dot(hid.astype(BF16), w, preferred_element_type=F32) * decay_slab * decay_row
        n_ref[0, 0:1, :] += jnp.sum(jnp.abs(f), axis=0, keepdims=True)
        return f

    _fft_a_core(rows_of, fa_ref, o_ref, s_refs, ns)


def _fft_a_scratch(ns):
    return [pltpu.VMEM((FFT_K1 * FFT_PITCH, LANES), F32)] * (2 * ns)


def _fft_a_vmem(ns):
    blk = FFT_K1 * FFT_GROUP * LANES
    return 2 * ns * blk * 4 + 2 * 2 * ns * blk * 2 + ns * blk * 4 + 2 * ns * FFT_K1 * FFT_PITCH * LANES * 4 + (8 << 20)


def fft_filter_pass_a(hid, w_o, length, dc=512):
    q, _, d = w_o.shape
    ns = dc // LANES
    delta = jnp.abs(jnp.linspace(MIN_DECAY, MAX_DECAY, d, dtype=F32))[None]
    hid3 = hid.reshape(FFT_K1, FFT_N2, LANES)
    fa = _phase_tables(length)[0]
    return pl.pallas_call(
        functools.partial(_fft_a_filter_kernel, ns=ns, length=length),
        grid=(q, d // dc, FFT_N2 // FFT_GROUP),
        in_specs=[pl.BlockSpec((FFT_K1, FFT_GROUP, LANES), lambda w, ci, g: (0, g, 0)),
                  pl.BlockSpec((1, LANES, dc), lambda w, ci, g: (w, 0, ci)),
                  pl.BlockSpec((1, dc), lambda w, ci, g: (0, ci)),
                  pl.BlockSpec(fa.shape, lambda w, ci, g: (0, 0))],
        out_specs=[pl.BlockSpec((1, FFT_K1, 2, FFT_GROUP, dc), lambda w, ci, g: (w, 0, 0, g, ci)),
                   pl.BlockSpec((1, F32_SUBLANES, dc), lambda w, ci, g: (w, 0, ci))],
        out_shape=[jax.ShapeDtypeStruct((q, FFT_K1, 2, FFT_N2, d), BF16),
                   jax.ShapeDtypeStruct((q, F32_SUBLANES, d), F32)],
        scratch_shapes=[pltpu.VMEM((FFT_K1 * FFT_GROUP, LANES), F32)] + _fft_a_scratch(ns),
        compiler_params=_params(("parallel", "parallel", "arbitrary"), _fft_a_vmem(ns)),
        name="fft_filter_pass_a",
    )(hid3, w_o, delta, fa)


def fft_pass_a(x, col0, d, dc=256):
    q, length, c = x.shape
    ns = dc // LANES
    x4 = x.reshape(q, FFT_K1, FFT_N2, c)
    slab0 = col0 // LANES
    in_specs = [pl.BlockSpec((None, FFT_K1, FFT_GROUP, LANES),
                             functools.partial(lambda w, ci, g, s: (w, 0, g, slab0 + ci * ns + s), s=s))
                for s in range(ns)]
    fa = _phase_tables(length)[0]
    in_specs.append(pl.BlockSpec(fa.shape, lambda w, ci, g: (0, 0)))
    return pl.pallas_call(
        functools.partial(_fft_a_kernel, ns=ns),
        grid=(q, d // dc, FFT_N2 // FFT_GROUP),
        in_specs=in_specs,
        out_specs=pl.BlockSpec((1, FFT_K1, 2, FFT_GROUP, dc), lambda w, ci, g: (w, 0, 0, g, ci)),
        out_shape=jax.ShapeDtypeStruct((q, FFT_K1, 2, FFT_N2, d), BF16),
        scratch_shapes=[pltpu.VMEM((FFT_K1 * FFT_GROUP, LANES), F32)] * ns + _fft_a_scratch(ns),
        compiler_params=_params(("parallel", "parallel", "parallel"), _fft_a_vmem(ns)),
        name="fft_pass_a",
    )(*([x4] * ns), fa)


def _fft_b_kernel(a_ref, af_ref, ab_ref, gb_ref, gbi_ref, o_ref):
    n2 = FFT_N2
    dc = a_ref.shape[2]
    for i in range(a_ref.shape[0]):
        rhs = jnp.concatenate([a_ref[i], af_ref[0, i], ab_ref[0, i]], axis=1)
        res = jnp.dot(gb_ref[i], rhs, preferred_element_type=F32)
        u, uf, ub = res[:, 0:dc], res[:, dc:2 * dc], res[:, 2 * dc:3 * dc]
        ur, ui = u[0:n2], u[n2:2 * n2]
        kr, ki = uf[0:n2] + ub[0:n2], uf[n2:2 * n2] - ub[n2:2 * n2]
        v = jnp.concatenate([ur * kr - ui * ki, ur * ki + ui * kr], axis=0).astype(BF16)
        o_ref[i] = jnp.dot(gbi_ref[i], v, preferred_element_type=F32).astype(o_ref.dtype)


def fft_pass_b(a, filt_a, order, gb, gbi, kb=8, dc=512):
    k1, r, d = a.shape
    vmem = 2 * 4 * kb * r * dc * 2 + 2 * 2 * kb * r * r * 2 + 10 * r * dc * 4 + (4 << 20)
    return pl.pallas_call(
        _fft_b_kernel,
        grid=(k1 // kb, d // dc),
        in_specs=[pl.BlockSpec((kb, r, dc), lambda k, c: (k, 0, c)),
                  pl.BlockSpec((1, kb, r, dc), lambda k, c: (2 * order, k, 0, c)),
                  pl.BlockSpec((1, kb, r, dc), lambda k, c: (2 * order + 1, k, 0, c)),
                  pl.BlockSpec((kb, r, r), lambda k, c: (k, 0, 0)),
                  pl.BlockSpec((kb, r, r), lambda k, c: (k, 0, 0))],
        out_specs=pl.BlockSpec((kb, r, dc), lambda k, c: (k, 0, c)),
        out_shape=jax.ShapeDtypeStruct((k1, r, d), BF16),
        compiler_params=_params(("parallel", "parallel"), vmem),
        name="fft_pass_b",
    )(a, filt_a, filt_a, gb, gbi)


def _fft_c_kernel(*refs, ns):
    b_ref, ca_ref, gate_refs, z_refs = refs[0], refs[1], refs[2:2 + ns], refs[2 + ns:2 + 2 * ns]
    n_ref, bias_ref, o_ref = refs[2 + 2 * ns:5 + 2 * ns]
    s_refs = refs[5 + 2 * ns:5 + 4 * ns]
    t_refs = refs[5 + 4 * ns:]
    ca = ca_ref[...]

    def spread(k, carry):
        for p in range(2):
            for s in range(ns):
                s_refs[p * ns + s][_regroup_rows(k), :] = b_ref[k, p, :, s * LANES:(s + 1) * LANES].astype(F32)
        return carry

    lax.fori_loop(0, FFT_K1, spread, 0, unroll=8)
    for j in range(FFT_GROUP):
        b = jnp.concatenate([
            jnp.concatenate([s_refs[p * ns + s][pl.ds(j, FFT_K1, stride=FFT_PITCH), :] for s in range(ns)], axis=1)
            for p in range(2)], axis=0)
        y = jnp.dot(ca, b.astype(BF16), preferred_element_type=F32)
        for s in range(ns):
            t_refs[s][pl.ds(j, FFT_K1, stride=FFT_PITCH), :] = y[:, s * LANES:(s + 1) * LANES]

    def emit(k, carry):
        for s in range(ns):
            sl = slice(s * LANES, (s + 1) * LANES)
            y = t_refs[s][_regroup_rows(k), :] / (n_ref[0:1, sl] + n_ref[1:2, sl] + 1e-6)
            o_ref[k, :, sl] = (gate_refs[s][k].astype(F32) * (y + bias_ref[:, sl] * z_refs[s][k])).astype(o_ref.dtype)
        return carry

    lax.fori_loop(0, FFT_K1, emit, 0, unroll=8)


def fft_pass_c(bp, ca, gate_arr, gate_col0, z_arr, z_col0, nsum, bias, out_dtype, dc=256):
    k1, r, d = bp.shape
    ns = dc // LANES
    length = FFT_K1 * FFT_N2
    b4 = bp.reshape(k1, 2, FFT_N2, d)
    g3 = gate_arr.reshape(FFT_K1, FFT_N2, gate_arr.shape[1])
    z3 = z_arr.reshape(FFT_K1, FFT_N2, z_arr.shape[1])
    sig = lambda slab0: [pl.BlockSpec((FFT_K1, FFT_GROUP, LANES),
                                      functools.partial(lambda ci, g, s: (0, g, slab0 + ci * ns + s), s=s))
                         for s in range(ns)]
    row = pl.BlockSpec((1, dc), lambda ci, g: (0, ci))
    blk = FFT_K1 * FFT_GROUP * dc
    vmem = 2 * 2 * blk * 2 + 2 * 3 * blk * 4 + 3 * ns * FFT_K1 * FFT_PITCH * LANES * 4 + (8 << 20)
    out = pl.pallas_call(
        functools.partial(_fft_c_kernel, ns=ns),
        grid=(d // dc, FFT_N2 // FFT_GROUP),
        in_specs=[pl.BlockSpec((k1, 2, FFT_GROUP, dc), lambda ci, g: (0, 0, g, ci)),
                  pl.BlockSpec(ca.shape, lambda ci, g: (0, 0)),
                  *sig(gate_col0 // LANES), *sig(z_col0 // LANES),
                  pl.BlockSpec((2, dc), lambda ci, g: (0, ci)), row],
        out_specs=pl.BlockSpec((FFT_K1, FFT_GROUP, dc), lambda ci, g: (0, g, ci)),
        out_shape=jax.ShapeDtypeStruct((FFT_K1, FFT_N2, d), out_dtype),
        scratch_shapes=[pltpu.VMEM((FFT_K1 * FFT_PITCH, LANES), F32)] * (3 * ns),
        compiler_params=_params(("parallel", "parallel"), vmem),
        name="fft_pass_c",
    )(b4, ca, *([g3] * ns), *([z3] * ns), nsum, bias)
    return out.reshape(length, d)


def kernel(x, c, ctx, c_ctx, ada_w, ada_b, norm_g, mix_w_in, mix_conv_w, mix_sink, mix_w_out, ffn_w_gu, ffn_w_down, hy_w_in, hy_b_in, hy_conv_w, hy_conv_b, hf_w1, hf_b1, hf_freq, hf_w2, hf_b2, hf_w3, hf_b3, hf_w_out, hf_bias, hy_w_out, hy_b_out, router_w, router_b, moe_w_gu, moe_w_down):
    assert x.shape[0] == 1 and ada_w.shape[0] == 2
    s, d = x.shape[1], x.shape[2]
    xs = x[0]
    ctxs = ctx[0]

    acts = jnp.zeros((F32_SUBLANES, d), F32).at[0].set(c[0]).at[1].set(c_ctx)
    mods = adaln_all(acts, ada_w, ada_b)
    row = lambda layer, r, k: mods[layer, r:r + 1, k * d:(k + 1) * d]

    g = norm_g[0]
    h = norm_mod(xs, g[0:1], row(0, 0, 0), row(0, 0, 1))
    hc = norm_mod(ctxs, g[0:1], row(0, 1, 0), row(0, 1, 1))
    w_in = mix_w_in[0].astype(BF16)
    p, ffn_gu = matmul(h, w_in, out_dtype=BF16, side=ffn_w_gu[0], side_chunks=256)
    ckv = matmul(hc, w_in[:, KV_START:], out_dtype=BF16)
    cs, sn = rope_tables(s)
    y = even_mixer_core(p, ckv, cs, sn, mix_conv_w[0], mix_sink[0])
    out = matmul(y, mix_w_out[0].astype(BF16))
    xs, h = post_norm(xs, out, g[1:2], row(0, 0, 2), g[2:3], row(0, 0, 3), row(0, 0, 4))

    act, ffn_down = swiglu_up(h, ffn_gu, tm=2048, tn=256, side=ffn_w_down[0], side_chunks=344)
    out, hy_in = matmul(act, ffn_down, tm=512, tn=256, side=hy_w_in[0], side_chunks=256)
    g1 = norm_g[1]
    xs, h = post_norm(xs, out, g[3:4], row(0, 0, 5), g1[0:1], row(1, 0, 0), row(1, 0, 1))

    g = g1
    moe_gu_f32 = moe_w_gu[0].reshape(-1, moe_w_gu.shape[-1])
    u0, moe_gu = matmul(h, hy_in, bias=hy_b_in[0][None], out_dtype=BF16,
                        side=moe_gu_f32, side_chunks=256)
    x12 = short_conv(u0, hy_conv_w[0], hy_conv_b[0][None], 0, 2 * d, BF16)
    v = short_conv(u0, hy_conv_w[0], hy_conv_b[0][None], 2 * d, d, F32)
    hid = filter_hidden(s, hf_w1[0], hf_b1[0], hf_freq[0], hf_w2[0], hf_b2[0], hf_w3[0], hf_b3[0])
    w_o = hf_w_out[0].astype(F32).reshape(FILTER_HIDDEN, HYENA_ORDER * 2, d).transpose(1, 0, 2)
    w_o = jnp.zeros((HYENA_ORDER * 2, LANES, d), F32).at[:, :FILTER_HIDDEN].set(w_o)
    filt_a, fsum = fft_filter_pass_a(hid, w_o, s)
    fsum = fsum[:, 0, :]
    _, ca, gb, gbi = _phase_tables(s)
    spec = lambda a: a.reshape(a.shape[0], FFT_K1, 2 * FFT_N2, d)
    filt_a = spec(filt_a)
    a = spec(fft_pass_a(v[None], 0, d))[0]
    z1 = fft_pass_c(fft_pass_b(a, filt_a, 0, gb, gbi), ca, x12, 0, v, 0, fsum[0:2], hf_bias[0, 0][None], F32)
    a = spec(fft_pass_a(z1[None], 0, d))[0]
    z2 = fft_pass_c(fft_pass_b(a, filt_a, 1, gb, gbi), ca, x12, d, z1, 0, fsum[2:4], hf_bias[0, 1][None], BF16)
    moe_down_f32 = moe_w_down[0].reshape(-1, moe_w_down.shape[-1])
    out, moe_down = matmul(z2, hy_w_out[0].astype(BF16), bias=hy_b_out[0][None], side=moe_down_f32, side_chunks=128)
    xs, sel, wts = post_router_top2(xs, out, g[1:2], row(1, 0, 2), g[2:3], row(1, 0, 3), row(1, 0, 4),
                                    router_w[0], router_b[0])
    pos, src_tok, tile_expert, n_valid = moe_plan(sel)
    act = grouped_swiglu_up(xs, src_tok, g[2:3], row(1, 0, 3), row(1, 0, 4), moe_gu.reshape(moe_w_gu.shape[1:]),
                            tile_expert, n_valid)
    ys = grouped_down(act, moe_down.reshape(moe_w_down.shape[1:]), tile_expert, n_valid)
    xs = moe_combine_residual(xs, wts, ys, pos, g[3:4], row(1, 0, 5))
    return xs[None]
```

```python
import functools
import math

import jax
import jax.numpy as jnp
import numpy as np
from jax import lax
from jax.experimental import pallas as pl
from jax.experimental.pallas import tpu as pltpu

F32 = jnp.float32
BF16 = jnp.bfloat16

D_MODEL = 4096
GRID_W = 64
HEAD_DIM = 128
CONV_WIDTH = D_MODEL // 2
N_HEADS = (D_MODEL // 2) // HEAD_DIM
N_KV_HEADS = N_HEADS // 4
GQA_GROUP = N_HEADS // N_KV_HEADS
Q_WIDTH = N_HEADS * HEAD_DIM
KV_WIDTH = N_KV_HEADS * HEAD_DIM
Q_START = 3 * CONV_WIDTH
KV_START = Q_START + Q_WIDTH
IN_WIDTH = KV_START + 2 * KV_WIDTH
BLOCK = 128
ATTN_SCALE = HEAD_DIM ** -0.5
ROPE_BASE = 10000.0
ROPE_PAIRS = HEAD_DIM // 4
HYENA_ORDER = 2
FILTER_EMB = 33
FILTER_BANDS = (FILTER_EMB - 1) // 2
FILTER_HIDDEN = 64
MIN_DECAY = math.log(1e-2) / 0.3
MAX_DECAY = math.log(1e-2) / 1.5
N_EXPERTS = 8
NORM_EPS = 1e-6

V7X_VMEM_BYTES = 64 * 1024 * 1024
LANES = 128
F32_SUBLANES = 8
BF16_SUBLANES = 16


def _params(semantics, vmem_bytes):
    limit = min(int(vmem_bytes), V7X_VMEM_BYTES - 4 * 1024 * 1024)
    return pltpu.CompilerParams(dimension_semantics=semantics, vmem_limit_bytes=limit)


def _sigmoid(v):
    return 1.0 / (1.0 + jnp.exp(-v))


def _adaln_kernel(a_ref, w_ref, b_ref, o_ref, acc_ref):
    k = pl.program_id(2)

    @pl.when(k == 0)
    def _():
        acc_ref[...] = jnp.zeros_like(acc_ref)

    a = a_ref[...]
    a = a * _sigmoid(a)
    acc_ref[...] += jnp.dot(a.astype(BF16), w_ref[0].astype(BF16), preferred_element_type=F32)

    @pl.when(k == pl.num_programs(2) - 1)
    def _():
        o_ref[0] = acc_ref[...] + b_ref[0]


def adaln_all(acts, ada_w, ada_b):
    depth, d, n = ada_w.shape
    tn, tk = 2048, 1024
    return pl.pallas_call(
        _adaln_kernel,
        grid=(depth, n // tn, d // tk),
        in_specs=[
            pl.BlockSpec((F32_SUBLANES, tk), lambda l, j, k: (0, k)),
            pl.BlockSpec((1, tk, tn), lambda l, j, k: (l, k, j)),
            pl.BlockSpec((1, 1, tn), lambda l, j, k: (l, 0, j)),
        ],
        out_specs=pl.BlockSpec((1, F32_SUBLANES, tn), lambda l, j, k: (l, 0, j)),
        out_shape=jax.ShapeDtypeStruct((depth, F32_SUBLANES, n), F32),
        scratch_shapes=[pltpu.VMEM((F32_SUBLANES, tn), F32)],
        compiler_params=_params(("parallel", "parallel", "arbitrary"), 2 * tk * tn * 4 + tk * tn * 2 + (4 << 20)),
        name="adaln",
    )(acts, ada_w, ada_b.reshape(depth, 1, n))


def _norm_mod(x, g, sh, sc):
    y = x * lax.rsqrt(jnp.mean(x * x, axis=-1, keepdims=True) + NORM_EPS)
    return (y * g) * (1.0 + sc) + sh


def _norm_mod_kernel(x_ref, g_ref, sh_ref, sc_ref, o_ref):
    o_ref[...] = _norm_mod(x_ref[...], g_ref[...], sh_ref[...], sc_ref[...]).astype(o_ref.dtype)


def norm_mod(x, g, shift, scale, tm=512):
    m, d = x.shape
    tm = min(tm, m)
    row = pl.BlockSpec((1, d), lambda i: (0, 0))
    return pl.pallas_call(
        _norm_mod_kernel,
        grid=(m // tm,),
        in_specs=[pl.BlockSpec((tm, d), lambda i: (i, 0)), row, row, row],
        out_specs=pl.BlockSpec((tm, d), lambda i: (i, 0)),
        out_shape=jax.ShapeDtypeStruct((m, d), BF16),
        compiler_params=_params(("parallel",), 2 * tm * d * 4 + 2 * tm * d * 2 + 3 * tm * d * 4 + (2 << 20)),
        name="norm_mod",
    )(x, g, shift, scale)


def _post_norm_kernel(x_ref, y_ref, g_ref, gt_ref, g2_ref, sh_ref, sc_ref, o_ref, h_ref):
    y = y_ref[...].astype(F32)
    yn = y * lax.rsqrt(jnp.mean(y * y, axis=-1, keepdims=True) + NORM_EPS)
    x = x_ref[...] + gt_ref[...] * (yn * g_ref[...])
    o_ref[...] = x
    h_ref[...] = _norm_mod(x, g2_ref[...], sh_ref[...], sc_ref[...]).astype(h_ref.dtype)


def post_norm(x, y, g, gate, g_next, shift, scale, tm=256):
    m, d = x.shape
    row = pl.BlockSpec((1, d), lambda i: (0, 0))
    blk = pl.BlockSpec((tm, d), lambda i: (i, 0))
    return pl.pallas_call(
        _post_norm_kernel,
        grid=(m // tm,),
        in_specs=[blk, blk, row, row, row, row, row],
        out_specs=[blk, blk],
        out_shape=[jax.ShapeDtypeStruct((m, d), F32), jax.ShapeDtypeStruct((m, d), BF16)],
        compiler_params=_params(("parallel",), 7 * tm * d * 4 + 5 * tm * d * 4 + (2 << 20)),
        name="post_norm",
    )(x, y, g, gate, g_next, shift, scale)


def _side_cast(src_ref, dst_ref, step, n_chunks):
    @pl.when(step < n_chunks)
    def _():
        dst_ref[...] = src_ref[...].astype(dst_ref.dtype)


def _side_cast_plumbing(side, n_chunks, step_of):
    rows, cols = side.shape
    spec = pl.BlockSpec((rows // n_chunks, cols), lambda *g: (jnp.minimum(step_of(*g), n_chunks - 1), 0))
    vmem = 2 * (rows // n_chunks) * cols * (4 + 2)
    return spec, jax.ShapeDtypeStruct((rows, cols), BF16), vmem


def _mm_kernel(*refs, has_bias, side_chunks):
    a_ref, w_ref = refs[0], refs[1]
    n_in = 2 + has_bias + (side_chunks > 0)
    o_ref = refs[n_in]
    acc = jnp.dot(a_ref[...], w_ref[...], preferred_element_type=F32)
    if has_bias:
        acc = acc + refs[2][...]
    o_ref[...] = acc.astype(o_ref.dtype)
    if side_chunks:
        step = pl.program_id(0) * pl.num_programs(1) + pl.program_id(1)
        _side_cast(refs[n_in - 1], refs[n_in + 1], step, side_chunks)


def matmul(a, w, bias=None, out_dtype=F32, tm=1024, tn=512, side=None, side_chunks=0, a_buffers=2):
    m, k = a.shape
    n = w.shape[1]
    tm, tn = min(tm, m), min(tn, n)
    nj = n // tn
    a_mode = {} if a_buffers == 2 else {"pipeline_mode": pl.Buffered(a_buffers)}
    in_specs = [pl.BlockSpec((tm, k), lambda i, j: (i, 0), **a_mode), pl.BlockSpec((k, tn), lambda i, j: (0, j))]
    args = [a, w]
    if bias is not None:
        in_specs.append(pl.BlockSpec((1, tn), lambda i, j: (0, j)))
        args.append(bias)
    osz = jnp.dtype(out_dtype).itemsize
    vmem = a_buffers * tm * k * 2 + 2 * k * tn * 2 + 2 * tm * tn * osz + 2 * tm * tn * 4 + (2 << 20)
    out_specs = pl.BlockSpec((tm, tn), lambda i, j: (i, j))
    out_shape = jax.ShapeDtypeStruct((m, n), out_dtype)
    if side is not None:
        assert side_chunks <= (m // tm) * nj
        spec, shape, side_vmem = _side_cast_plumbing(side, side_chunks, lambda i, j: i * nj + j)
        in_specs.append(spec)
        args.append(side)
        out_specs, out_shape, vmem = [out_specs, spec], [out_shape, shape], vmem + side_vmem
    return pl.pallas_call(
        functools.partial(_mm_kernel, has_bias=bias is not None, side_chunks=side_chunks),
        grid=(m // tm, nj),
        in_specs=in_specs,
        out_specs=out_specs,
        out_shape=out_shape,
        compiler_params=_params(("arbitrary", "arbitrary") if side is not None else ("parallel", "arbitrary"), vmem),
        name="matmul",
    )(*args)


def _swiglu_up_kernel(a_ref, wg_ref, wu_ref, side_ref, o_ref, side_o_ref, *, side_chunks):
    a = a_ref[...]
    g = jnp.dot(a, wg_ref[...], preferred_element_type=F32)
    u = jnp.dot(a, wu_ref[...], preferred_element_type=F32)
    o_ref[...] = (g * _sigmoid(g) * u).astype(o_ref.dtype)
    _side_cast(side_ref, side_o_ref, pl.program_id(0) * pl.num_programs(1) + pl.program_id(1), side_chunks)


def swiglu_up(a, w_gu, tm, tn, side, side_chunks):
    m, k = a.shape
    f = w_gu.shape[1] // 2
    nj = f // tn
    assert side_chunks <= (m // tm) * nj
    spec, shape, side_vmem = _side_cast_plumbing(side, side_chunks, lambda i, j: i * nj + j)
    vmem = 2 * tm * k * 2 + 4 * k * tn * 2 + 2 * tm * tn * 2 + 4 * tm * tn * 4 + side_vmem + (2 << 20)
    return pl.pallas_call(
        functools.partial(_swiglu_up_kernel, side_chunks=side_chunks),
        grid=(m // tm, nj),
        in_specs=[
            pl.BlockSpec((tm, k), lambda i, j: (i, 0)),
            pl.BlockSpec((k, tn), lambda i, j: (0, j)),
            pl.BlockSpec((k, tn), lambda i, j: (0, j + nj)),
            spec,
        ],
        out_specs=[pl.BlockSpec((tm, tn), lambda i, j: (i, j)), spec],
        out_shape=[jax.ShapeDtypeStruct((m, f), BF16), shape],
        compiler_params=_params(("arbitrary", "arbitrary"), vmem),
        name="swiglu_up",
    )(a, w_gu, w_gu, side)


def _post_router_kernel(x_ref, y_ref, gp_ref, gt_ref, g_ref, sh_ref, sc_ref, w_ref, b_ref, o_ref, sel_ref, wts_ref):
    y = y_ref[...].astype(F32)
    yn = y * lax.rsqrt(jnp.mean(y * y, axis=-1, keepdims=True) + NORM_EPS)
    x = x_ref[...] + gt_ref[...] * (yn * gp_ref[...])
    o_ref[...] = x
    h = _norm_mod(x, g_ref[...], sh_ref[...], sc_ref[...])
    logits = jnp.dot(h, w_ref[...], preferred_element_type=F32, precision=lax.Precision.HIGHEST) + b_ref[...]
    lane = lax.broadcasted_iota(jnp.int32, logits.shape, 1)
    neg = jnp.float32(-jnp.inf)
    logits = jnp.where(lane < N_EXPERTS, logits, neg)
    v1 = jnp.max(logits, axis=1, keepdims=True)
    i1 = jnp.min(jnp.where(logits == v1, lane, LANES), axis=1, keepdims=True)
    rest = jnp.where(lane == i1, neg, logits)
    v2 = jnp.max(rest, axis=1, keepdims=True)
    i2 = jnp.min(jnp.where(rest == v2, lane, LANES), axis=1, keepdims=True)
    e2 = jnp.exp(v2 - v1)
    w1 = 1.0 / (1.0 + e2)
    w2 = e2 / (1.0 + e2)
    sel_ref[...] = jnp.where(lane == 0, i1, jnp.where(lane == 1, i2, 0))
    wts_ref[...] = jnp.where(lane == 0, w1, jnp.where(lane == 1, w2, 0.0))


def post_router_top2(x, y, g_post, gate, g, shift, scale, router_w, router_b, tm=256):
    m, d = x.shape
    wp = jnp.zeros((d, LANES), F32).at[:, :N_EXPERTS].set(router_w)
    bp = jnp.zeros((1, LANES), F32).at[0, :N_EXPERTS].set(router_b)
    row = pl.BlockSpec((1, d), lambda i: (0, 0))
    blk = pl.BlockSpec((tm, d), lambda i: (i, 0))
    out = pl.BlockSpec((tm, LANES), lambda i: (i, 0))
    return pl.pallas_call(
        _post_router_kernel,
        grid=(m // tm,),
        in_specs=[blk, blk, row, row, row, row, row,
                  pl.BlockSpec((d, LANES), lambda i: (0, 0)), pl.BlockSpec((1, LANES), lambda i: (0, 0))],
        out_specs=[blk, out, out],
        out_shape=[jax.ShapeDtypeStruct((m, d), F32), jax.ShapeDtypeStruct((m, LANES), jnp.int32),
                   jax.ShapeDtypeStruct((m, LANES), F32)],
        compiler_params=_params(("parallel",), 6 * tm * d * 4 + 6 * tm * d * 4 + 2 * d * LANES * 4 + (4 << 20)),
        name="post_router",
    )(x, y, g_post, gate, g, shift, scale, wp, bp)


MOE_TILE = 512


def moe_plan(sel):
    t = sel.shape[0]
    npairs = 2 * t
    rows = npairs + N_EXPERTS * MOE_TILE
    e_flat = sel[:, :2].reshape(npairs)
    order = jnp.argsort(e_flat, stable=True).astype(jnp.int32)
    rank = jnp.argsort(order).astype(jnp.int32)
    counts = jnp.sum((e_flat[:, None] == jnp.arange(N_EXPERTS, dtype=jnp.int32)[None, :]).astype(jnp.int32), axis=0)
    padded = (counts + MOE_TILE - 1) // MOE_TILE * MOE_TILE
    ends_p = jnp.cumsum(padded)
    starts_p = ends_p - padded
    starts_u = jnp.cumsum(counts) - counts
    pos = starts_p[e_flat] + rank - starts_u[e_flat]
    tile_start = jnp.arange(rows // MOE_TILE, dtype=jnp.int32) * MOE_TILE
    tile_expert = jnp.minimum(jnp.searchsorted(ends_p, tile_start, side="right"), N_EXPERTS - 1).astype(jnp.int32)
    n_valid = (ends_p[-1] // MOE_TILE).astype(jnp.int32).reshape(1)
    r = jnp.arange(rows, dtype=jnp.int32)
    e_row = jnp.repeat(tile_expert, MOE_TILE)
    in_group = r - starts_p[e_row]
    valid = (in_group < counts[e_row]) & (r < ends_p[-1])
    src_tok = jnp.where(valid, order[jnp.clip(starts_u[e_row] + in_group, 0, npairs - 1)] // 2, 0)
    return pos.astype(jnp.int32), src_tok.astype(jnp.int32), tile_expert, n_valid


def _gather_rows(idx_ref, base, stride, src_hbm, dst_ref, sem, r0, n, wait):
    def body(r, carry):
        copy = pltpu.make_async_copy(src_hbm.at[pl.ds(idx_ref[base + stride * r], 1)], dst_ref.at[pl.ds(r, 1)], sem)
        if wait:
            copy.wait()
        else:
            copy.start()
        return carry

    lax.fori_loop(r0, r0 + n, body, 0, unroll=8)


def _gswiglu_kernel(te_ref, nv_ref, src_ref, x_hbm, g_ref, sh_ref, sc_ref, wg_ref, wu_ref, o_ref,
                    xbuf_ref, a_ref, sems):
    i, j = pl.program_id(0), pl.program_id(1)
    tm = a_ref.shape[0]
    n_valid = nv_ref[0]
    slot = i % 2

    def gather(tile, to_slot, wait):
        @pl.when(tile < n_valid)
        def _():
            _gather_rows(src_ref, tile * tm, 1, x_hbm, xbuf_ref.at[to_slot], sems.at[to_slot], 0, tm, wait=wait)

    @pl.when(j == 0)
    def _():
        @pl.when(i == 0)
        def _():
            gather(i, slot, False)

        gather(i, slot, True)
        gather(i + 1, 1 - slot, False)

        @pl.when(i < n_valid)
        def _():
            a_ref[...] = _norm_mod(xbuf_ref[slot], g_ref[...], sh_ref[...], sc_ref[...]).astype(a_ref.dtype)

    @pl.when(i < n_valid)
    def _():
        a = a_ref[...]
        g = jnp.dot(a, wg_ref[0], preferred_element_type=F32)
        u = jnp.dot(a, wu_ref[0], preferred_element_type=F32)
        o_ref[...] = (g * _sigmoid(g) * u).astype(o_ref.dtype)

    @pl.when(i >= n_valid)
    def _():
        o_ref[...] = jnp.zeros_like(o_ref)


def grouped_swiglu_up(x, src_tok, g, shift, scale, w_gu, tile_expert, n_valid, tn=512):
    k = x.shape[1]
    rows = src_tok.shape[0]
    f = w_gu.shape[2] // 2
    nj = f // tn
    tm = MOE_TILE
    row = pl.BlockSpec((1, k), lambda i, j, te, nv, src: (0, 0))
    vmem = 2 * tm * k * 4 + tm * k * 2 + 3 * tm * k * 4 + 4 * k * tn * 2 + 2 * tm * tn * 2 + 4 * tm * tn * 4 + (2 << 20)
    return pl.pallas_call(
        _gswiglu_kernel,
        grid_spec=pltpu.PrefetchScalarGridSpec(
            num_scalar_prefetch=3,
            grid=(rows // tm, nj),
            in_specs=[
                pl.BlockSpec(memory_space=pl.ANY), row, row, row,
                pl.BlockSpec((1, k, tn), lambda i, j, te, nv, src: (te[i], 0, j)),
                pl.BlockSpec((1, k, tn), lambda i, j, te, nv, src: (te[i], 0, j + nj)),
            ],
            out_specs=pl.BlockSpec((tm, tn), lambda i, j, te, nv, src: (i, j)),
            scratch_shapes=[pltpu.VMEM((2, tm, k), F32), pltpu.VMEM((tm, k), BF16), pltpu.SemaphoreType.DMA((2,))],
        ),
        out_shape=jax.ShapeDtypeStruct((rows, f), BF16),
        compiler_params=_params(("arbitrary", "arbitrary"), vmem),
        name="grouped_swiglu_up",
    )(tile_expert, n_valid, src_tok, x, g, shift, scale, w_gu, w_gu)


def _gdown_kernel(te_ref, nv_ref, a_ref, w_ref, o_ref):
    valid = pl.program_id(0) < nv_ref[0]

    @pl.when(valid)
    def _():
        o_ref[...] = jnp.dot(a_ref[...], w_ref[0], preferred_element_type=F32)

    @pl.when(jnp.logical_not(valid))
    def _():
        o_ref[...] = jnp.zeros_like(o_ref)


def grouped_down(act, w_down, tile_expert, n_valid, tn=1024):
    rows, f = act.shape
    n = w_down.shape[2]
    tm = MOE_TILE
    vmem = 2 * tm * f * 2 + 2 * f * tn * 2 + 2 * tm * tn * 4 + 2 * tm * tn * 4 + (2 << 20)
    return pl.pallas_call(
        _gdown_kernel,
        grid_spec=pltpu.PrefetchScalarGridSpec(
            num_scalar_prefetch=2,
            grid=(rows // tm, n // tn),
            in_specs=[
                pl.BlockSpec((tm, f), lambda i, j, te, nv: (i, 0)),
                pl.BlockSpec((1, f, tn), lambda i, j, te, nv: (te[i], 0, j)),
            ],
            out_specs=pl.BlockSpec((tm, tn), lambda i, j, te, nv: (i, j)),
        ),
        out_shape=jax.ShapeDtypeStruct((rows, n), F32),
        compiler_params=_params(("parallel", "arbitrary"), vmem),
        name="grouped_down",
    )(tile_expert, n_valid, act, w_down)


def _moe_combine_kernel(pos_ref, x_ref, w_ref, ys_hbm, g_ref, gt_ref, o_ref, buf_ref, sems):
    i = pl.program_id(0)
    tm = x_ref.shape[0]
    slot = i % 2

    def gather(step, to_slot, wait):
        for e in range(2):
            _gather_rows(pos_ref, 2 * step * tm + e, 2, ys_hbm, buf_ref.at[to_slot, e], sems.at[to_slot], 0, tm,
                         wait=wait)

    @pl.when(i == 0)
    def _():
        gather(i, slot, False)

    @pl.when(i + 1 < pl.num_programs(0))
    def _():
        gather(i + 1, 1 - slot, False)

    gather(i, slot, True)
    w = w_ref[...]
    y = w[:, 0:1] * buf_ref[slot, 0] + w[:, 1:2] * buf_ref[slot, 1]
    yn = y * lax.rsqrt(jnp.mean(y * y, axis=-1, keepdims=True) + NORM_EPS)
    o_ref[...] = x_ref[...] + gt_ref[...] * (yn * g_ref[...])


def moe_combine_residual(x, wts, ys, pos, g, gate, tm=128):
    t, d = x.shape
    row = pl.BlockSpec((1, d), lambda i, p: (0, 0))
    return pl.pallas_call(
        _moe_combine_kernel,
        grid_spec=pltpu.PrefetchScalarGridSpec(
            num_scalar_prefetch=1,
            grid=(t // tm,),
            in_specs=[pl.BlockSpec((tm, d), lambda i, p: (i, 0)), pl.BlockSpec((tm, LANES), lambda i, p: (i, 0)),
                      pl.BlockSpec(memory_space=pl.ANY), row, row],
            out_specs=pl.BlockSpec((tm, d), lambda i, p: (i, 0)),
            scratch_shapes=[pltpu.VMEM((2, 2, tm, d), F32), pltpu.SemaphoreType.DMA((2,))],
        ),
        out_shape=jax.ShapeDtypeStruct((t, d), F32),
        compiler_params=_params(("arbitrary",), 12 * tm * d * 4 + (2 << 20)),
        name="moe_combine_residual",
    )(pos, x, wts, ys, g, gate)


def _rope(x, cs, sn):
    lane = lax.broadcasted_iota(jnp.int32, x.shape, 1)
    first = (lane % (2 * ROPE_PAIRS)) < ROPE_PAIRS
    partner = jnp.where(first, pltpu.roll(x, HEAD_DIM - ROPE_PAIRS, 1), pltpu.roll(x, ROPE_PAIRS, 1))
    return x * cs + partner * sn


def _shift_rows(u, halo_prev, halo_next):
    n = u.shape[0]
    row = lax.broadcasted_iota(jnp.int32, u.shape, 0)
    prev = jnp.where(row == 0, halo_prev, pltpu.roll(u, 1, 0))
    nxt = jnp.where(row == n - 1, halo_next, pltpu.roll(u, n - 1, 0))
    return prev, nxt


def _mixer_kernel(sink_ref, p_ref, kvp_ref, kvn_ref, cp_ref, hp_ref, cn_ref, hn_ref, ckv_ref,
                  cs_ref, sn_ref, csp_ref, snp_ref, csn_ref, snn_ref, cw_ref, o_ref):
    i = pl.program_id(0)
    nb = pl.num_programs(0)
    cs, sn = cs_ref[...], sn_ref[...]

    last = BF16_SUBLANES - 1
    for c0 in range(0, CONV_WIDTH, LANES):
        cl = slice(c0, c0 + LANES)
        cc = slice(CONV_WIDTH + c0, CONV_WIDTH + c0 + LANES)
        ch = slice(2 * CONV_WIDTH + c0, 2 * CONV_WIDTH + c0 + LANES)
        cu = p_ref[:, cc].astype(F32) * p_ref[:, ch].astype(F32)
        halo_p = cp_ref[last:last + 1, cl].astype(F32) * hp_ref[last:last + 1, cl].astype(F32)
        halo_n = cn_ref[0:1, cl].astype(F32) * hn_ref[0:1, cl].astype(F32)
        halo_p = jnp.where(i > 0, halo_p, 0.0)
        halo_n = jnp.where(i < nb - 1, halo_n, 0.0)
        cu_prev, cu_next = _shift_rows(cu, halo_p, halo_n)
        y_conv = p_ref[:, cl].astype(F32) * (cu_prev * cw_ref[0:1, cl] + cu * cw_ref[1:2, cl] + cu_next * cw_ref[2:3, cl])
        o_ref[:, cl] = y_conv.astype(o_ref.dtype)

    n_ctx = ckv_ref.shape[0]
    n_keys = 3 * BLOCK + n_ctx
    qi = lax.broadcasted_iota(jnp.int32, (BLOCK, n_keys), 0)
    kj = lax.broadcasted_iota(jnp.int32, (BLOCK, n_keys), 1)
    key_pos = i * BLOCK + kj - BLOCK
    visible = (kj >= 3 * BLOCK) | ((jnp.abs(qi + BLOCK - kj) <= BLOCK) & (key_pos >= 0) & (key_pos < nb * BLOCK))
    nt = (((1,), (1,)), ((), ()))

    for kh in range(N_KV_HEADS):
        ko = KV_START + kh * HEAD_DIM
        vo = KV_START + KV_WIDTH + kh * HEAD_DIM
        kb = kh * HEAD_DIM
        vb = KV_WIDTH + kh * HEAD_DIM
        keys = jnp.concatenate([
            _rope(kvp_ref[:, kb:kb + HEAD_DIM].astype(F32), csp_ref[...], snp_ref[...]).astype(BF16),
            _rope(p_ref[:, ko:ko + HEAD_DIM].astype(F32), cs, sn).astype(BF16),
            _rope(kvn_ref[:, kb:kb + HEAD_DIM].astype(F32), csn_ref[...], snn_ref[...]).astype(BF16),
            ckv_ref[:, kb:kb + HEAD_DIM],
        ], axis=0)
        vals = jnp.concatenate([kvp_ref[:, vb:vb + HEAD_DIM], p_ref[:, vo:vo + HEAD_DIM],
                                kvn_ref[:, vb:vb + HEAD_DIM], ckv_ref[:, vb:vb + HEAD_DIM]], axis=0)
        for g in range(GQA_GROUP):
            h = kh * GQA_GROUP + g
            qo = Q_START + h * HEAD_DIM
            q = (_rope(p_ref[:, qo:qo + HEAD_DIM].astype(F32), cs, sn) * ATTN_SCALE).astype(BF16)
            sc = jnp.where(visible, lax.dot_general(q, keys, nt, preferred_element_type=F32), -jnp.inf)
            sink = sink_ref[h]
            m = jnp.maximum(jnp.max(sc, axis=1, keepdims=True), sink)
            e = jnp.exp(sc - m)
            denom = jnp.exp(sink - m) + jnp.sum(e, axis=1, keepdims=True)
            o = jnp.dot(e.astype(BF16), vals, preferred_element_type=F32) / denom
            oo = CONV_WIDTH + h * HEAD_DIM
            o_ref[:, oo:oo + HEAD_DIM] = o.astype(o_ref.dtype)


def even_mixer_core(p, ckv, cs, sn, conv_w, sink):
    s = p.shape[0]
    nb = s // BLOCK
    hb = BLOCK // BF16_SUBLANES
    kvc = KV_START // (2 * KV_WIDTH)
    prev = lambda i: jnp.maximum(i - 1, 0)
    nxt = lambda i: jnp.minimum(i + 1, nb - 1)
    tab = lambda f: pl.BlockSpec((BLOCK, HEAD_DIM), lambda i: (f(i), 0))
    same = lambda i: i
    in_specs = [
        pl.BlockSpec(memory_space=pltpu.SMEM),
        pl.BlockSpec((BLOCK, IN_WIDTH), lambda i: (i, 0)),
        pl.BlockSpec((BLOCK, 2 * KV_WIDTH), lambda i: (prev(i), kvc)),
        pl.BlockSpec((BLOCK, 2 * KV_WIDTH), lambda i: (nxt(i), kvc)),
        pl.BlockSpec((BF16_SUBLANES, CONV_WIDTH), lambda i: (jnp.maximum(i * hb - 1, 0), 1)),
        pl.BlockSpec((BF16_SUBLANES, CONV_WIDTH), lambda i: (jnp.maximum(i * hb - 1, 0), 2)),
        pl.BlockSpec((BF16_SUBLANES, CONV_WIDTH), lambda i: (jnp.minimum((i + 1) * hb, nb * hb - 1), 1)),
        pl.BlockSpec((BF16_SUBLANES, CONV_WIDTH), lambda i: (jnp.minimum((i + 1) * hb, nb * hb - 1), 2)),
        pl.BlockSpec(ckv.shape, lambda i: (0, 0)),
        tab(same), tab(same), tab(prev), tab(prev), tab(nxt), tab(nxt),
        pl.BlockSpec(conv_w.shape, lambda i: (0, 0)),
    ]
    return pl.pallas_call(
        _mixer_kernel,
        grid=(nb,),
        in_specs=in_specs,
        out_specs=pl.BlockSpec((BLOCK, CONV_WIDTH + Q_WIDTH), lambda i: (i, 0)),
        out_shape=jax.ShapeDtypeStruct((s, CONV_WIDTH + Q_WIDTH), BF16),
        compiler_params=_params(("parallel",), 32 << 20),
        name="even_mixer_core",
    )(sink, p, p, p, p, p, p, p, ckv, cs, sn, cs, sn, cs, sn, conv_w)


def rope_tables(s):
    t = np.arange(s)
    inv = ROPE_BASE ** (-np.arange(ROPE_PAIRS, dtype=np.float64) / ROPE_PAIRS)
    ang_r = (t // GRID_W)[:, None] * inv[None, :]
    ang_c = (t % GRID_W)[:, None] * inv[None, :]
    cs = np.concatenate([np.cos(ang_r)] * 2 + [np.cos(ang_c)] * 2, axis=1)
    sn = np.concatenate([-np.sin(ang_r), np.sin(ang_r), -np.sin(ang_c), np.sin(ang_c)], axis=1)
    return cs.astype(np.float32), sn.astype(np.float32)


def _sconv_kernel(u_ref, up_ref, un_ref, w_ref, b_ref, o_ref):
    i = pl.program_id(0)
    u = u_ref[...].astype(F32)
    last = up_ref.shape[0] - 1
    halo_p = jnp.where(i > 0, up_ref[last:last + 1, :].astype(F32), 0.0)
    halo_n = jnp.where(i < pl.num_programs(0) - 1, un_ref[0:1, :].astype(F32), 0.0)
    prev, nxt = _shift_rows(u, halo_p, halo_n)
    o_ref[...] = (prev * w_ref[0:1, :] + u * w_ref[1:2, :] + nxt * w_ref[2:3, :] + b_ref[...]).astype(o_ref.dtype)


def short_conv(u, w, b, col0, width, out_dtype, tm=256, tn=2048):
    s = u.shape[0]
    hr = BF16_SUBLANES
    hb = tm // hr
    nh = s // hr
    j0 = col0 // tn
    return pl.pallas_call(
        _sconv_kernel,
        grid=(s // tm, width // tn),
        in_specs=[
            pl.BlockSpec((tm, tn), lambda i, j: (i, j + j0)),
            pl.BlockSpec((hr, tn), lambda i, j: (jnp.maximum(i * hb - 1, 0), j + j0)),
            pl.BlockSpec((hr, tn), lambda i, j: (jnp.minimum((i + 1) * hb, nh - 1), j + j0)),
            pl.BlockSpec((3, tn), lambda i, j: (0, j + j0)),
            pl.BlockSpec((1, tn), lambda i, j: (0, j + j0)),
        ],
        out_specs=pl.BlockSpec((tm, tn), lambda i, j: (i, j)),
        out_shape=jax.ShapeDtypeStruct((s, width), out_dtype),
        compiler_params=_params(("parallel", "parallel"), 10 * tm * tn * 4 + (2 << 20)),
        name="short_conv",
    )(u, u, u, w, b)


def _hid_kernel(z_ref, w1_ref, b1_ref, w2_ref, b2_ref, w3_ref, b3_ref, fr_ref, o_ref):
    hp = lax.Precision.HIGHEST
    fr = fr_ref[...]
    h = jnp.sin(fr * (jnp.dot(z_ref[...], w1_ref[...], preferred_element_type=F32, precision=hp) + b1_ref[...]))
    h = jnp.sin(fr * (jnp.dot(h, w2_ref[...], preferred_element_type=F32, precision=hp) + b2_ref[...]))
    o_ref[...] = jnp.sin(fr * (jnp.dot(h, w3_ref[...], preferred_element_type=F32, precision=hp) + b3_ref[...]))


def _pad2(a, r, c):
    return jnp.zeros((r, c), F32).at[:a.shape[0], :a.shape[1]].set(a.astype(F32))


def filter_hidden(length, w1, b1, freq, w2, b2, w3, b3, tl=2048):
    t = np.linspace(0.0, 1.0, length)[:, None]
    w = (2.0 * math.pi / length) * np.arange(length)[:, None]
    bands = np.linspace(1e-4, FILTER_BANDS - 1, FILTER_BANDS)[None]
    z = np.concatenate([t, np.cos(bands * w), -np.sin(bands * w)], axis=-1)
    zp = np.zeros((length, LANES), np.float32)
    zp[:, :FILTER_EMB] = z
    full = pl.BlockSpec((LANES, LANES), lambda i: (0, 0))
    row = pl.BlockSpec((1, LANES), lambda i: (0, 0))
    return pl.pallas_call(
        _hid_kernel,
        grid=(length // tl,),
        in_specs=[pl.BlockSpec((tl, LANES), lambda i: (i, 0)), full, row, full, row, full, row, row],
        out_specs=pl.BlockSpec((tl, LANES), lambda i: (i, 0)),
        out_shape=jax.ShapeDtypeStruct((length, LANES), F32),
        compiler_params=_params(("parallel",), 16 << 20),
        name="filter_hidden",
    )(zp, _pad2(w1, LANES, LANES), _pad2(b1[None], 1, LANES), _pad2(w2, LANES, LANES), _pad2(b2[None], 1, LANES),
      _pad2(w3, LANES, LANES), _pad2(b3[None], 1, LANES), _pad2(freq[None], 1, LANES))


FFT_N1 = 256
FFT_N2 = 128
FFT_K1 = FFT_N1 // 2
FFT_GROUP = BF16_SUBLANES
FFT_PITCH = 3 * F32_SUBLANES


@functools.lru_cache(maxsize=None)
def _phase_tables(length):
    n1h, n2 = FFT_K1, FFT_N2
    assert length == n1h * n2
    n = 2 * length
    ia = np.arange(n1h, dtype=np.int64)
    tha = ((ia[None, :] * (2 * ia[:, None] + 1)) % (2 * FFT_N1)) * (2.0 * np.pi / (2 * FFT_N1))
    fa = np.concatenate([np.cos(tha), -np.sin(tha)], axis=0)
    ca = np.concatenate([np.cos(tha).T, -np.sin(tha).T], axis=1) * (2.0 / n)
    k1 = ia[:, None, None]
    k2 = np.arange(n2, dtype=np.int64)[None, :, None]
    m2 = np.arange(n2, dtype=np.int64)[None, None, :]
    phb = ((m2 * (2 * (k1 + FFT_N1 * k2) + 1)) % (2 * n)) * (2.0 * np.pi / (2 * n))
    gr, gi = np.cos(phb), -np.sin(phb)
    gb = np.concatenate([np.concatenate([gr, -gi], axis=2), np.concatenate([gi, gr], axis=2)], axis=1)
    hr, hi = np.swapaxes(gr, 1, 2), -np.swapaxes(gi, 1, 2)
    gbi = np.concatenate([np.concatenate([hr, -hi], axis=2), np.concatenate([hi, hr], axis=2)], axis=1)
    return tuple(t.astype(np.float32).astype(BF16) for t in (fa, ca, gb, gbi))


def _regroup_rows(k):
    return pl.ds(pl.multiple_of(k * FFT_PITCH, F32_SUBLANES), FFT_GROUP)


def _fft_a_core(rows_of, fa_ref, o_ref, s_refs, ns):
    fa = fa_ref[...]
    for j in range(FFT_GROUP):
        res = jnp.dot(fa, rows_of(j).astype(BF16), preferred_element_type=F32)
        for p in range(2):
            for s in range(ns):
                s_refs[p * ns + s][pl.ds(j, FFT_K1, stride=FFT_PITCH), :] = (
                    res[p * FFT_K1:(p + 1) * FFT_K1, s * LANES:(s + 1) * LANES])

    def emit(k, carry):
        for p in range(2):
            for s in range(ns):
                o_ref[0, k, p, :, s * LANES:(s + 1) * LANES] = s_refs[p * ns + s][_regroup_rows(k), :].astype(o_ref.dtype)
        return carry

    lax.fori_loop(0, FFT_K1, emit, 0, unroll=8)


def _fft_a_kernel(*refs, ns):
    x_refs, fa_ref, o_ref = refs[:ns], refs[ns], refs[ns + 1]
    xs_refs = refs[ns + 2:2 * ns + 2]
    for s in range(ns):
        xs_refs[s][...] = x_refs[s][...].reshape(FFT_K1 * FFT_GROUP, LANES)
    rows_of = lambda j: jnp.concatenate([r[pl.ds(j, FFT_K1, stride=FFT_GROUP), :] for r in xs_refs], axis=1)
    _fft_a_core(rows_of, fa_ref, o_ref, refs[2 * ns + 2:], ns)


def _fft_a_filter_kernel(hid_ref, w_ref, delta_ref, fa_ref, o_ref, n_ref, hs_ref, *s_refs, ns, length):
    g = pl.program_id(2)

    @pl.when(g == 0)
    def _():
        n_ref[...] = jnp.zeros_like(n_ref)

    hs_ref[...] = hid_ref[...].reshape(FFT_K1 * FFT_GROUP, LANES)
    w = w_ref[0].astype(BF16)
    delta = delta_ref[...]
    n1 = lax.broadcasted_iota(jnp.int32, (FFT_K1, ns * LANES), 0)
    decay_slab = jnp.exp(-((n1 * FFT_N2).astype(F32) / float(length - 1)) * delta)

    def rows_of(j):
        hid = hs_ref[pl.ds(j, FFT_K1, stride=FFT_GROUP), :]
        decay_row = jnp.exp(-((g * FFT_GROUP + j).astype(F32) / float(length - 1)) * delta)
        f = jnp.dot(hid.astype(BF16), w, preferred_element_type=F32) * decay_slab * decay_row
        n_ref[0, 0:1, :] += jnp.sum(jnp.abs(f), axis=0, keepdims=True)
        return f

    _fft_a_core(rows_of, fa_ref, o_ref, s_refs, ns)


def _fft_a_scratch(ns):
    return [pltpu.VMEM((FFT_K1 * FFT_PITCH, LANES), F32)] * (2 * ns)


def _fft_a_vmem(ns):
    blk = FFT_K1 * FFT_GROUP * LANES
    return 2 * ns * blk * 4 + 2 * 2 * ns * blk * 2 + ns * blk * 4 + 2 * ns * FFT_K1 * FFT_PITCH * LANES * 4 + (8 << 20)


def fft_filter_pass_a(hid, w_o, length, dc=256):
    q, _, d = w_o.shape
    ns = dc // LANES
    delta = jnp.abs(jnp.linspace(MIN_DECAY, MAX_DECAY, d, dtype=F32))[None]
    hid3 = hid.reshape(FFT_K1, FFT_N2, LANES)
    fa = _phase_tables(length)[0]
    return pl.pallas_call(
        functools.partial(_fft_a_filter_kernel, ns=ns, length=length),
        grid=(q, d // dc, FFT_N2 // FFT_GROUP),
        in_specs=[pl.BlockSpec((FFT_K1, FFT_GROUP, LANES), lambda w, ci, g: (0, g, 0)),
                  pl.BlockSpec((1, LANES, dc), lambda w, ci, g: (w, 0, ci)),
                  pl.BlockSpec((1, dc), lambda w, ci, g: (0, ci)),
                  pl.BlockSpec(fa.shape, lambda w, ci, g: (0, 0))],
        out_specs=[pl.BlockSpec((1, FFT_K1, 2, FFT_GROUP, dc), lambda w, ci, g: (w, 0, 0, g, ci)),
                   pl.BlockSpec((1, F32_SUBLANES, dc), lambda w, ci, g: (w, 0, ci))],
        out_shape=[jax.ShapeDtypeStruct((q, FFT_K1, 2, FFT_N2, d), BF16),
                   jax.ShapeDtypeStruct((q, F32_SUBLANES, d), F32)],
        scratch_shapes=[pltpu.VMEM((FFT_K1 * FFT_GROUP, LANES), F32)] + _fft_a_scratch(ns),
        compiler_params=_params(("parallel", "parallel", "arbitrary"), _fft_a_vmem(ns)),
        name="fft_filter_pass_a",
    )(hid3, w_o, delta, fa)


def fft_pass_a(x, col0, d, dc=256):
    q, length, c = x.shape
    ns = dc // LANES
    x4 = x.reshape(q, FFT_K1, FFT_N2, c)
    slab0 = col0 // LANES
    in_specs = [pl.BlockSpec((None, FFT_K1, FFT_GROUP, LANES),
                             functools.partial(lambda w, ci, g, s: (w, 0, g, slab0 + ci * ns + s), s=s))
                for s in range(ns)]
    fa = _phase_tables(length)[0]
    in_specs.append(pl.BlockSpec(fa.shape, lambda w, ci, g: (0, 0)))
    return pl.pallas_call(
        functools.partial(_fft_a_kernel, ns=ns),
        grid=(q, d // dc, FFT_N2 // FFT_GROUP),
        in_specs=in_specs,
        out_specs=pl.BlockSpec((1, FFT_K1, 2, FFT_GROUP, dc), lambda w, ci, g: (w, 0, 0, g, ci)),
        out_shape=jax.ShapeDtypeStruct((q, FFT_K1, 2, FFT_N2, d), BF16),
        scratch_shapes=[pltpu.VMEM((FFT_K1 * FFT_GROUP, LANES), F32)] * ns + _fft_a_scratch(ns),
        compiler_params=_params(("parallel", "parallel", "parallel"), _fft_a_vmem(ns)),
        name="fft_pass_a",
    )(*([x4] * ns), fa)


def _fft_b_kernel(a_ref, af_ref, ab_ref, gb_ref, gbi_ref, o_ref):
    n2 = FFT_N2
    dc = a_ref.shape[2]
    for i in range(a_ref.shape[0]):
        rhs = jnp.concatenate([a_ref[i], af_ref[0, i], ab_ref[0, i]], axis=1)
        res = jnp.dot(gb_ref[i], rhs, preferred_element_type=F32)
        u, uf, ub = res[:, 0:dc], res[:, dc:2 * dc], res[:, 2 * dc:3 * dc]
        ur, ui = u[0:n2], u[n2:2 * n2]
        kr, ki = uf[0:n2] + ub[0:n2], uf[n2:2 * n2] - ub[n2:2 * n2]
        v = jnp.concatenate([ur * kr - ui * ki, ur * ki + ui * kr], axis=0).astype(BF16)
        o_ref[i] = jnp.dot(gbi_ref[i], v, preferred_element_type=F32).astype(o_ref.dtype)


def fft_pass_b(a, filt_a, order, gb, gbi, kb=8, dc=512):
    k1, r, d = a.shape
    vmem = 2 * 4 * kb * r * dc * 2 + 2 * 2 * kb * r * r * 2 + 10 * r * dc * 4 + (4 << 20)
    return pl.pallas_call(
        _fft_b_kernel,
        grid=(k1 // kb, d // dc),
        in_specs=[pl.BlockSpec((kb, r, dc), lambda k, c: (k, 0, c)),
                  pl.BlockSpec((1, kb, r, dc), lambda k, c: (2 * order, k, 0, c)),
                  pl.BlockSpec((1, kb, r, dc), lambda k, c: (2 * order + 1, k, 0, c)),
                  pl.BlockSpec((kb, r, r), lambda k, c: (k, 0, 0)),
                  pl.BlockSpec((kb, r, r), lambda k, c: (k, 0, 0))],
        out_specs=pl.BlockSpec((kb, r, dc), lambda k, c: (k, 0, c)),
        out_shape=jax.ShapeDtypeStruct((k1, r, d), BF16),
        compiler_params=_params(("parallel", "parallel"), vmem),
        name="fft_pass_b",
    )(a, filt_a, filt_a, gb, gbi)


def _fft_c_kernel(*refs, ns):
    b_ref, ca_ref, gate_refs, z_refs = refs[0], refs[1], refs[2:2 + ns], refs[2 + ns:2 + 2 * ns]
    n_ref, bias_ref, o_ref = refs[2 + 2 * ns:5 + 2 * ns]
    s_refs = refs[5 + 2 * ns:5 + 4 * ns]
    t_refs = refs[5 + 4 * ns:]
    ca = ca_ref[...]

    def spread(k, carry):
        for p in range(2):
            for s in range(ns):
                s_refs[p * ns + s][_regroup_rows(k), :] = b_ref[k, p, :, s * LANES:(s + 1) * LANES].astype(F32)
        return carry

    lax.fori_loop(0, FFT_K1, spread, 0, unroll=8)
    for j in range(FFT_GROUP):
        b = jnp.concatenate([
            jnp.concatenate([s_refs[p * ns + s][pl.ds(j, FFT_K1, stride=FFT_PITCH), :] for s in range(ns)], axis=1)
            for p in range(2)], axis=0)
        y = jnp.dot(ca, b.astype(BF16), preferred_element_type=F32)
        for s in range(ns):
            t_refs[s][pl.ds(j, FFT_K1, stride=FFT_PITCH), :] = y[:, s * LANES:(s + 1) * LANES]

    def emit(k, carry):
        for s in range(ns):
            sl = slice(s * LANES, (s + 1) * LANES)
            y = t_refs[s][_regroup_rows(k), :] / (n_ref[0:1, sl] + n_ref[1:2, sl] + 1e-6)
            o_ref[k, :, sl] = (gate_refs[s][k].astype(F32) * (y + bias_ref[:, sl] * z_refs[s][k])).astype(o_ref.dtype)
        return carry

    lax.fori_loop(0, FFT_K1, emit, 0, unroll=8)


def fft_pass_c(bp, ca, gate_arr, gate_col0, z_arr, z_col0, nsum, bias, out_dtype, dc=256):
    k1, r, d = bp.shape
    ns = dc // LANES
    length = FFT_K1 * FFT_N2
    b4 = bp.reshape(k1, 2, FFT_N2, d)
    g3 = gate_arr.reshape(FFT_K1, FFT_N2, gate_arr.shape[1])
    z3 = z_arr.reshape(FFT_K1, FFT_N2, z_arr.shape[1])
    sig = lambda slab0: [pl.BlockSpec((FFT_K1, FFT_GROUP, LANES),
                                      functools.partial(lambda ci, g, s: (0, g, slab0 + ci * ns + s), s=s))
                         for s in range(ns)]
    row = pl.BlockSpec((1, dc), lambda ci, g: (0, ci))
    blk = FFT_K1 * FFT_GROUP * dc
    vmem = 2 * 2 * blk * 2 + 2 * 3 * blk * 4 + 3 * ns * FFT_K1 * FFT_PITCH * LANES * 4 + (8 << 20)
    out = pl.pallas_call(
        functools.partial(_fft_c_kernel, ns=ns),
        grid=(d // dc, FFT_N2 // FFT_GROUP),
        in_specs=[pl.BlockSpec((k1, 2, FFT_GROUP, dc), lambda ci, g: (0, 0, g, ci)),
                  pl.BlockSpec(ca.shape, lambda ci, g: (0, 0)),
                  *sig(gate_col0 // LANES), *sig(z_col0 // LANES),
                  pl.BlockSpec((2, dc), lambda ci, g: (0, ci)), row],
        out_specs=pl.BlockSpec((FFT_K1, FFT_GROUP, dc), lambda ci, g: (0, g, ci)),
        out_shape=jax.ShapeDtypeStruct((FFT_K1, FFT_N2, d), out_dtype),
        scratch_shapes=[pltpu.VMEM((FFT_K1 * FFT_PITCH, LANES), F32)] * (3 * ns),
        compiler_params=_params(("parallel", "parallel"), vmem),
        name="fft_pass_c",
    )(b4, ca, *([g3] * ns), *([z3] * ns), nsum, bias)
    return out.reshape(length, d)


def kernel(x, c, ctx, c_ctx, ada_w, ada_b, norm_g, mix_w_in, mix_conv_w, mix_sink, mix_w_out, ffn_w_gu, ffn_w_down, hy_w_in, hy_b_in, hy_conv_w, hy_conv_b, hf_w1, hf_b1, hf_freq, hf_w2, hf_b2, hf_w3, hf_b3, hf_w_out, hf_bias, hy_w_out, hy_b_out, router_w, router_b, moe_w_gu, moe_w_down):
    assert x.shape[0] == 1 and ada_w.shape[0] == 2
    s, d = x.shape[1], x.shape[2]
    xs = x[0]
    ctxs = ctx[0]

    acts = jnp.zeros((F32_SUBLANES, d), F32).at[0].set(c[0]).at[1].set(c_ctx)
    mods = adaln_all(acts, ada_w, ada_b)
    row = lambda layer, r, k: mods[layer, r:r + 1, k * d:(k + 1) * d]

    g = norm_g[0]
    h = norm_mod(xs, g[0:1], row(0, 0, 0), row(0, 0, 1))
    hc = norm_mod(ctxs, g[0:1], row(0, 1, 0), row(0, 1, 1))
    w_in = mix_w_in[0].astype(BF16)
    p, ffn_gu = matmul(h, w_in, out_dtype=BF16, side=ffn_w_gu[0], side_chunks=256)
    ckv = matmul(hc, w_in[:, KV_START:], out_dtype=BF16)
    cs, sn = rope_tables(s)
    y = even_mixer_core(p, ckv, cs, sn, mix_conv_w[0], mix_sink[0])
    out = matmul(y, mix_w_out[0].astype(BF16), out_dtype=BF16)
    xs, h = post_norm(xs, out, g[1:2], row(0, 0, 2), g[2:3], row(0, 0, 3), row(0, 0, 4))

    act, ffn_down = swiglu_up(h, ffn_gu, tm=2048, tn=256, side=ffn_w_down[0], side_chunks=344)
    out, hy_in = matmul(act, ffn_down, out_dtype=BF16, tm=1024, tn=256, a_buffers=1, side=hy_w_in[0], side_chunks=256)
    g1 = norm_g[1]
    xs, h = post_norm(xs, out, g[3:4], row(0, 0, 5), g1[0:1], row(1, 0, 0), row(1, 0, 1))

    g = g1
    moe_gu_f32 = moe_w_gu[0].reshape(-1, moe_w_gu.shape[-1])
    u0, moe_gu = matmul(h, hy_in, bias=hy_b_in[0][None], out_dtype=BF16,
                        side=moe_gu_f32, side_chunks=256)
    x12 = short_conv(u0, hy_conv_w[0], hy_conv_b[0][None], 0, 2 * d, BF16)
    v = short_conv(u0, hy_conv_w[0], hy_conv_b[0][None], 2 * d, d, F32)
    hid = filter_hidden(s, hf_w1[0], hf_b1[0], hf_freq[0], hf_w2[0], hf_b2[0], hf_w3[0], hf_b3[0])
    w_o = hf_w_out[0].astype(F32).reshape(FILTER_HIDDEN, HYENA_ORDER * 2, d).transpose(1, 0, 2)
    w_o = jnp.zeros((HYENA_ORDER * 2, LANES, d), F32).at[:, :FILTER_HIDDEN].set(w_o)
    filt_a, fsum = fft_filter_pass_a(hid, w_o, s)
    fsum = fsum[:, 0, :]
    _, ca, gb, gbi = _phase_tables(s)
    spec = lambda a: a.reshape(a.shape[0], FFT_K1, 2 * FFT_N2, d)
    filt_a = spec(filt_a)
    a = spec(fft_pass_a(v[None], 0, d))[0]
    z1 = fft_pass_c(fft_pass_b(a, filt_a, 0, gb, gbi), ca, x12, 0, v, 0, fsum[0:2], hf_bias[0, 0][None], F32)
    a = spec(fft_pass_a(z1[None], 0, d))[0]
    z2 = fft_pass_c(fft_pass_b(a, filt_a, 1, gb, gbi), ca, x12, d, z1, 0, fsum[2:4], hf_bias[0, 1][None], BF16)
    moe_down_f32 = moe_w_down[0].reshape(-1, moe_w_down.shape[-1])
    out, moe_down = matmul(z2, hy_w_out[0].astype(BF16), bias=hy_b_out[0][None], out_dtype=BF16,
                           side=moe_down_f32, side_chunks=128)
    xs, sel, wts = post_router_top2(xs, out, g[1:2], row(1, 0, 2), g[2:3], row(1, 0, 3), row(1, 0, 4),
                                    router_w[0], router_b[0])
    pos, src_tok, tile_expert, n_valid = moe_plan(sel)
    act = grouped_swiglu_up(xs, src_tok, g[2:3], row(1, 0, 3), row(1, 0, 4), moe_gu.reshape(moe_w_gu.shape[1:]),
                            tile_expert, n_valid)
    ys = grouped_down(act, moe_down.reshape(moe_w_down.shape[1:]), tile_expert, n_valid)
    xs = moe_combine_residual(xs, wts, ys, pos, g[3:4], row(1, 0, 5))
    return xs[None]
```

```python
import functools
import math

import jax
import jax.numpy as jnp
import numpy as np
from jax import lax
from jax.experimental import pallas as pl
from jax.experimental.pallas import tpu as pltpu

F32 = jnp.float32
BF16 = jnp.bfloat16

D_MODEL = 4096
GRID_W = 64
HEAD_DIM = 128
CONV_WIDTH = D_MODEL // 2
N_HEADS = (D_MODEL // 2) // HEAD_DIM
N_KV_HEADS = N_HEADS // 4
GQA_GROUP = N_HEADS // N_KV_HEADS
Q_WIDTH = N_HEADS * HEAD_DIM
KV_WIDTH = N_KV_HEADS * HEAD_DIM
Q_START = 3 * CONV_WIDTH
KV_START = Q_START + Q_WIDTH
IN_WIDTH = KV_START + 2 * KV_WIDTH
BLOCK = 128
ATTN_SCALE = HEAD_DIM ** -0.5
ROPE_BASE = 10000.0
ROPE_PAIRS = HEAD_DIM // 4
HYENA_ORDER = 2
FILTER_EMB = 33
FILTER_BANDS = (FILTER_EMB - 1) // 2
FILTER_HIDDEN = 64
MIN_DECAY = math.log(1e-2) / 0.3
MAX_DECAY = math.log(1e-2) / 1.5
N_EXPERTS = 8
NORM_EPS = 1e-6

V7X_VMEM_BYTES = 64 * 1024 * 1024
V7X_MXU_COLS = 256
LANES = 128
F32_SUBLANES = 8
BF16_SUBLANES = 16


def _params(semantics, vmem_bytes):
    limit = min(int(vmem_bytes), V7X_VMEM_BYTES - 4 * 1024 * 1024)
    return pltpu.CompilerParams(dimension_semantics=semantics, vmem_limit_bytes=limit)


def _sigmoid(v):
    return 1.0 / (1.0 + jnp.exp(-v))


def _adaln_kernel(a_ref, w_ref, b_ref, o_ref, acc_ref):
    k = pl.program_id(2)

    @pl.when(k == 0)
    def _():
        acc_ref[...] = jnp.zeros_like(acc_ref)

    a = a_ref[...]
    a = a * _sigmoid(a)
    acc_ref[...] += jnp.dot(a.astype(BF16), w_ref[0].astype(BF16), preferred_element_type=F32)

    @pl.when(k == pl.num_programs(2) - 1)
    def _():
        o_ref[0] = acc_ref[...] + b_ref[0]


def adaln_all(acts, ada_w, ada_b):
    depth, d, n = ada_w.shape
    tn, tk = 2048, 1024
    return pl.pallas_call(
        _adaln_kernel,
        grid=(depth, n // tn, d // tk),
        in_specs=[
            pl.BlockSpec((F32_SUBLANES, tk), lambda l, j, k: (0, k)),
            pl.BlockSpec((1, tk, tn), lambda l, j, k: (l, k, j)),
            pl.BlockSpec((1, 1, tn), lambda l, j, k: (l, 0, j)),
        ],
        out_specs=pl.BlockSpec((1, F32_SUBLANES, tn), lambda l, j, k: (l, 0, j)),
        out_shape=jax.ShapeDtypeStruct((depth, F32_SUBLANES, n), F32),
        scratch_shapes=[pltpu.VMEM((F32_SUBLANES, tn), F32)],
        compiler_params=_params(("parallel", "parallel", "arbitrary"), 2 * tk * tn * 4 + tk * tn * 2 + (4 << 20)),
        name="adaln",
    )(acts, ada_w, ada_b.reshape(depth, 1, n))


def _norm_mod(x, g, sh, sc):
    y = x * lax.rsqrt(jnp.mean(x * x, axis=-1, keepdims=True) + NORM_EPS)
    return (y * g) * (1.0 + sc) + sh


def _norm_mod_kernel(x_ref, g_ref, sh_ref, sc_ref, o_ref):
    o_ref[...] = _norm_mod(x_ref[...], g_ref[...], sh_ref[...], sc_ref[...]).astype(o_ref.dtype)


def norm_mod(x, g, shift, scale, tm=512):
    m, d = x.shape
    tm = min(tm, m)
    row = pl.BlockSpec((1, d), lambda i: (0, 0))
    return pl.pallas_call(
        _norm_mod_kernel,
        grid=(m // tm,),
        in_specs=[pl.BlockSpec((tm, d), lambda i: (i, 0)), row, row, row],
        out_specs=pl.BlockSpec((tm, d), lambda i: (i, 0)),
        out_shape=jax.ShapeDtypeStruct((m, d), BF16),
        compiler_params=_params(("parallel",), 2 * tm * d * 4 + 2 * tm * d * 2 + 3 * tm * d * 4 + (2 << 20)),
        name="norm_mod",
    )(x, g, shift, scale)


def _post_norm_kernel(x_ref, y_ref, g_ref, gt_ref, g2_ref, sh_ref, sc_ref, o_ref, h_ref):
    y = y_ref[...].astype(F32)
    yn = y * lax.rsqrt(jnp.mean(y * y, axis=-1, keepdims=True) + NORM_EPS)
    x = x_ref[...] + gt_ref[...] * (yn * g_ref[...])
    o_ref[...] = x
    h_ref[...] = _norm_mod(x, g2_ref[...], sh_ref[...], sc_ref[...]).astype(h_ref.dtype)


def post_norm(x, y, g, gate, g_next, shift, scale, tm=256):
    m, d = x.shape
    row = pl.BlockSpec((1, d), lambda i: (0, 0))
    blk = pl.BlockSpec((tm, d), lambda i: (i, 0))
    return pl.pallas_call(
        _post_norm_kernel,
        grid=(m // tm,),
        in_specs=[blk, blk, row, row, row, row, row],
        out_specs=[blk, blk],
        out_shape=[jax.ShapeDtypeStruct((m, d), F32), jax.ShapeDtypeStruct((m, d), BF16)],
        compiler_params=_params(("parallel",), 7 * tm * d * 4 + 5 * tm * d * 4 + (2 << 20)),
        name="post_norm",
    )(x, y, g, gate, g_next, shift, scale)


def _side_cast(src_ref, dst_ref, step, n_chunks):
    @pl.when(step < n_chunks)
    def _():
        dst_ref[...] = src_ref[...].astype(dst_ref.dtype)


def _side_cast_plumbing(side, n_chunks, step_of):
    rows, cols = side.shape
    spec = pl.BlockSpec((rows // n_chunks, cols), lambda *g: (jnp.minimum(step_of(*g), n_chunks - 1), 0))
    vmem = 2 * (rows // n_chunks) * cols * (4 + 2)
    return spec, jax.ShapeDtypeStruct((rows, cols), BF16), vmem


def _mm_kernel(*refs, has_bias, side_chunks):
    a_ref, w_ref = refs[0], refs[1]
    n_in = 2 + has_bias + (side_chunks > 0)
    o_ref = refs[n_in]
    acc = jnp.dot(a_ref[...], w_ref[...], preferred_element_type=F32)
    if has_bias:
        acc = acc + refs[2][...]
    o_ref[...] = acc.astype(o_ref.dtype)
    if side_chunks:
        step = pl.program_id(0) * pl.num_programs(1) + pl.program_id(1)
        _side_cast(refs[n_in - 1], refs[n_in + 1], step, side_chunks)


def matmul(a, w, bias=None, out_dtype=F32, tm=1024, tn=512, side=None, side_chunks=0, a_buffers=2):
    m, k = a.shape
    n = w.shape[1]
    tm, tn = min(tm, m), min(tn, n)
    nj = n // tn
    a_mode = {} if a_buffers == 2 else {"pipeline_mode": pl.Buffered(a_buffers)}
    in_specs = [pl.BlockSpec((tm, k), lambda i, j: (i, 0), **a_mode), pl.BlockSpec((k, tn), lambda i, j: (0, j))]
    args = [a, w]
    if bias is not None:
        in_specs.append(pl.BlockSpec((1, tn), lambda i, j: (0, j)))
        args.append(bias)
    osz = jnp.dtype(out_dtype).itemsize
    vmem = a_buffers * tm * k * 2 + 2 * k * tn * 2 + 2 * tm * tn * osz + 2 * tm * tn * 4 + (2 << 20)
    out_specs = pl.BlockSpec((tm, tn), lambda i, j: (i, j))
    out_shape = jax.ShapeDtypeStruct((m, n), out_dtype)
    if side is not None:
        assert side_chunks <= (m // tm) * nj
        spec, shape, side_vmem = _side_cast_plumbing(side, side_chunks, lambda i, j: i * nj + j)
        in_specs.append(spec)
        args.append(side)
        out_specs, out_shape, vmem = [out_specs, spec], [out_shape, shape], vmem + side_vmem
    return pl.pallas_call(
        functools.partial(_mm_kernel, has_bias=bias is not None, side_chunks=side_chunks),
        grid=(m // tm, nj),
        in_specs=in_specs,
        out_specs=out_specs,
        out_shape=out_shape,
        compiler_params=_params(("arbitrary", "arbitrary") if side is not None else ("parallel", "arbitrary"), vmem),
        name="matmul",
    )(*args)


def _swiglu_up_kernel(a_ref, wg_ref, wu_ref, side_ref, o_ref, side_o_ref, *, side_chunks):
    a = a_ref[...]
    g = jnp.dot(a, wg_ref[...], preferred_element_type=F32)
    u = jnp.dot(a, wu_ref[...], preferred_element_type=F32)
    o_ref[...] = (g * _sigmoid(g) * u).astype(o_ref.dtype)
    _side_cast(side_ref, side_o_ref, pl.program_id(0) * pl.num_programs(1) + pl.program_id(1), side_chunks)


def swiglu_up(a, w_gu, tm, tn, side, side_chunks):
    m, k = a.shape
    f = w_gu.shape[1] // 2
    nj = f // tn
    assert side_chunks <= (m // tm) * nj
    spec, shape, side_vmem = _side_cast_plumbing(side, side_chunks, lambda i, j: i * nj + j)
    vmem = 2 * tm * k * 2 + 4 * k * tn * 2 + 2 * tm * tn * 2 + 4 * tm * tn * 4 + side_vmem + (2 << 20)
    return pl.pallas_call(
        functools.partial(_swiglu_up_kernel, side_chunks=side_chunks),
        grid=(m // tm, nj),
        in_specs=[
            pl.BlockSpec((tm, k), lambda i, j: (i, 0)),
            pl.BlockSpec((k, tn), lambda i, j: (0, j)),
            pl.BlockSpec((k, tn), lambda i, j: (0, j + nj)),
            spec,
        ],
        out_specs=[pl.BlockSpec((tm, tn), lambda i, j: (i, j)), spec],
        out_shape=[jax.ShapeDtypeStruct((m, f), BF16), shape],
        compiler_params=_params(("arbitrary", "arbitrary"), vmem),
        name="swiglu_up",
    )(a, w_gu, w_gu, side)


def _post_router_kernel(x_ref, y_ref, gp_ref, gt_ref, g_ref, sh_ref, sc_ref, w_ref, b_ref, o_ref, sel_ref, wts_ref):
    y = y_ref[...].astype(F32)
    yn = y * lax.rsqrt(jnp.mean(y * y, axis=-1, keepdims=True) + NORM_EPS)
    x = x_ref[...] + gt_ref[...] * (yn * gp_ref[...])
    o_ref[...] = x
    h = _norm_mod(x, g_ref[...], sh_ref[...], sc_ref[...])
    logits = jnp.dot(h, w_ref[...], preferred_element_type=F32, precision=lax.Precision.HIGHEST) + b_ref[...]
    lane = lax.broadcasted_iota(jnp.int32, logits.shape, 1)
    neg = jnp.float32(-jnp.inf)
    logits = jnp.where(lane < N_EXPERTS, logits, neg)
    v1 = jnp.max(logits, axis=1, keepdims=True)
    i1 = jnp.min(jnp.where(logits == v1, lane, LANES), axis=1, keepdims=True)
    rest = jnp.where(lane == i1, neg, logits)
    v2 = jnp.max(rest, axis=1, keepdims=True)
    i2 = jnp.min(jnp.where(rest == v2, lane, LANES), axis=1, keepdims=True)
    e2 = jnp.exp(v2 - v1)
    w1 = 1.0 / (1.0 + e2)
    w2 = e2 / (1.0 + e2)
    sel_ref[...] = jnp.where(lane == 0, i1, jnp.where(lane == 1, i2, 0))
    wts_ref[...] = jnp.where(lane == 0, w1, jnp.where(lane == 1, w2, 0.0))


def post_router_top2(x, y, g_post, gate, g, shift, scale, router_w, router_b, tm=256):
    m, d = x.shape
    wp = jnp.zeros((d, LANES), F32).at[:, :N_EXPERTS].set(router_w)
    bp = jnp.zeros((1, LANES), F32).at[0, :N_EXPERTS].set(router_b)
    row = pl.BlockSpec((1, d), lambda i: (0, 0))
    blk = pl.BlockSpec((tm, d), lambda i: (i, 0))
    out = pl.BlockSpec((tm, LANES), lambda i: (i, 0))
    return pl.pallas_call(
        _post_router_kernel,
        grid=(m // tm,),
        in_specs=[blk, blk, row, row, row, row, row,
                  pl.BlockSpec((d, LANES), lambda i: (0, 0)), pl.BlockSpec((1, LANES), lambda i: (0, 0))],
        out_specs=[blk, out, out],
        out_shape=[jax.ShapeDtypeStruct((m, d), F32), jax.ShapeDtypeStruct((m, LANES), jnp.int32),
                   jax.ShapeDtypeStruct((m, LANES), F32)],
        compiler_params=_params(("parallel",), 6 * tm * d * 4 + 6 * tm * d * 4 + 2 * d * LANES * 4 + (4 << 20)),
        name="post_router",
    )(x, y, g_post, gate, g, shift, scale, wp, bp)


MOE_TILE = 512


def moe_plan(sel):
    t = sel.shape[0]
    npairs = 2 * t
    rows = npairs + N_EXPERTS * MOE_TILE
    e_flat = sel[:, :2].reshape(npairs)
    order = jnp.argsort(e_flat, stable=True).astype(jnp.int32)
    rank = jnp.argsort(order).astype(jnp.int32)
    counts = jnp.sum((e_flat[:, None] == jnp.arange(N_EXPERTS, dtype=jnp.int32)[None, :]).astype(jnp.int32), axis=0)
    padded = (counts + MOE_TILE - 1) // MOE_TILE * MOE_TILE
    ends_p = jnp.cumsum(padded)
    starts_p = ends_p - padded
    starts_u = jnp.cumsum(counts) - counts
    pos = starts_p[e_flat] + rank - starts_u[e_flat]
    tile_start = jnp.arange(rows // MOE_TILE, dtype=jnp.int32) * MOE_TILE
    tile_expert = jnp.minimum(jnp.searchsorted(ends_p, tile_start, side="right"), N_EXPERTS - 1).astype(jnp.int32)
    n_valid = (ends_p[-1] // MOE_TILE).astype(jnp.int32).reshape(1)
    r = jnp.arange(rows, dtype=jnp.int32)
    e_row = jnp.repeat(tile_expert, MOE_TILE)
    in_group = r - starts_p[e_row]
    valid = (in_group < counts[e_row]) & (r < ends_p[-1])
    src_tok = jnp.where(valid, order[jnp.clip(starts_u[e_row] + in_group, 0, npairs - 1)] // 2, 0)
    return pos.astype(jnp.int32), src_tok.astype(jnp.int32), tile_expert, n_valid


def _gather_rows(idx_ref, base, stride, src_hbm, dst_ref, sem, r0, n, wait):
    def body(r, carry):
        copy = pltpu.make_async_copy(src_hbm.at[pl.ds(idx_ref[base + stride * r], 1)], dst_ref.at[pl.ds(r, 1)], sem)
        if wait:
            copy.wait()
        else:
            copy.start()
        return carry

    lax.fori_loop(r0, r0 + n, body, 0, unroll=8)


def _gswiglu_kernel(te_ref, nv_ref, src_ref, x_hbm, g_ref, sh_ref, sc_ref, wg_ref, wu_ref, o_ref,
                    xbuf_ref, a_ref, sems):
    i, j = pl.program_id(0), pl.program_id(1)
    tm = a_ref.shape[0]
    n_valid = nv_ref[0]
    slot = i % 2

    def gather(tile, to_slot, wait):
        @pl.when(tile < n_valid)
        def _():
            _gather_rows(src_ref, tile * tm, 1, x_hbm, xbuf_ref.at[to_slot], sems.at[to_slot], 0, tm, wait=wait)

    @pl.when(j == 0)
    def _():
        @pl.when(i == 0)
        def _():
            gather(i, slot, False)

        gather(i, slot, True)
        gather(i + 1, 1 - slot, False)

        @pl.when(i < n_valid)
        def _():
            a_ref[...] = _norm_mod(xbuf_ref[slot], g_ref[...], sh_ref[...], sc_ref[...]).astype(a_ref.dtype)

    @pl.when(i < n_valid)
    def _():
        a = a_ref[...]
        g = jnp.dot(a, wg_ref[0], preferred_element_type=F32)
        u = jnp.dot(a, wu_ref[0], preferred_element_type=F32)
        o_ref[...] = (g * _sigmoid(g) * u).astype(o_ref.dtype)

    @pl.when(i >= n_valid)
    def _():
        o_ref[...] = jnp.zeros_like(o_ref)


def grouped_swiglu_up(x, src_tok, g, shift, scale, w_gu, tile_expert, n_valid, tn=512):
    k = x.shape[1]
    rows = src_tok.shape[0]
    f = w_gu.shape[2] // 2
    nj = f // tn
    tm = MOE_TILE
    row = pl.BlockSpec((1, k), lambda i, j, te, nv, src: (0, 0))
    vmem = 2 * tm * k * 4 + tm * k * 2 + 3 * tm * k * 4 + 4 * k * tn * 2 + 2 * tm * tn * 2 + 4 * tm * tn * 4 + (2 << 20)
    return pl.pallas_call(
        _gswiglu_kernel,
        grid_spec=pltpu.PrefetchScalarGridSpec(
            num_scalar_prefetch=3,
            grid=(rows // tm, nj),
            in_specs=[
                pl.BlockSpec(memory_space=pl.ANY), row, row, row,
                pl.BlockSpec((1, k, tn), lambda i, j, te, nv, src: (te[i], 0, j)),
                pl.BlockSpec((1, k, tn), lambda i, j, te, nv, src: (te[i], 0, j + nj)),
            ],
            out_specs=pl.BlockSpec((tm, tn), lambda i, j, te, nv, src: (i, j)),
            scratch_shapes=[pltpu.VMEM((2, tm, k), F32), pltpu.VMEM((tm, k), BF16), pltpu.SemaphoreType.DMA((2,))],
        ),
        out_shape=jax.ShapeDtypeStruct((rows, f), BF16),
        compiler_params=_params(("arbitrary", "arbitrary"), vmem),
        name="grouped_swiglu_up",
    )(tile_expert, n_valid, src_tok, x, g, shift, scale, w_gu, w_gu)


def _gdown_kernel(te_ref, nv_ref, a_ref, w_ref, o_ref):
    valid = pl.program_id(0) < nv_ref[0]

    @pl.when(valid)
    def _():
        o_ref[...] = jnp.dot(a_ref[...], w_ref[0], preferred_element_type=F32)

    @pl.when(jnp.logical_not(valid))
    def _():
        o_ref[...] = jnp.zeros_like(o_ref)


def grouped_down(act, w_down, tile_expert, n_valid, tn=1024):
    rows, f = act.shape
    n = w_down.shape[2]
    tm = MOE_TILE
    vmem = 2 * tm * f * 2 + 2 * f * tn * 2 + 2 * tm * tn * 4 + 2 * tm * tn * 4 + (2 << 20)
    return pl.pallas_call(
        _gdown_kernel,
        grid_spec=pltpu.PrefetchScalarGridSpec(
            num_scalar_prefetch=2,
            grid=(rows // tm, n // tn),
            in_specs=[
                pl.BlockSpec((tm, f), lambda i, j, te, nv: (i, 0)),
                pl.BlockSpec((1, f, tn), lambda i, j, te, nv: (te[i], 0, j)),
            ],
            out_specs=pl.BlockSpec((tm, tn), lambda i, j, te, nv: (i, j)),
        ),
        out_shape=jax.ShapeDtypeStruct((rows, n), F32),
        compiler_params=_params(("parallel", "arbitrary"), vmem),
        name="grouped_down",
    )(tile_expert, n_valid, act, w_down)


def _moe_combine_kernel(pos_ref, x_ref, w_ref, ys_hbm, g_ref, gt_ref, o_ref, buf_ref, sems):
    i = pl.program_id(0)
    tm = x_ref.shape[0]
    slot = i % 2

    def gather(step, to_slot, wait):
        for e in range(2):
            _gather_rows(pos_ref, 2 * step * tm + e, 2, ys_hbm, buf_ref.at[to_slot, e], sems.at[to_slot], 0, tm,
                         wait=wait)

    @pl.when(i == 0)
    def _():
        gather(i, slot, False)

    @pl.when(i + 1 < pl.num_programs(0))
    def _():
        gather(i + 1, 1 - slot, False)

    gather(i, slot, True)
    w = w_ref[...]
    y = w[:, 0:1] * buf_ref[slot, 0] + w[:, 1:2] * buf_ref[slot, 1]
    yn = y * lax.rsqrt(jnp.mean(y * y, axis=-1, keepdims=True) + NORM_EPS)
    o_ref[...] = x_ref[...] + gt_ref[...] * (yn * g_ref[...])


def moe_combine_residual(x, wts, ys, pos, g, gate, tm=128):
    t, d = x.shape
    row = pl.BlockSpec((1, d), lambda i, p: (0, 0))
    return pl.pallas_call(
        _moe_combine_kernel,
        grid_spec=pltpu.PrefetchScalarGridSpec(
            num_scalar_prefetch=1,
            grid=(t // tm,),
            in_specs=[pl.BlockSpec((tm, d), lambda i, p: (i, 0)), pl.BlockSpec((tm, LANES), lambda i, p: (i, 0)),
                      pl.BlockSpec(memory_space=pl.ANY), row, row],
            out_specs=pl.BlockSpec((tm, d), lambda i, p: (i, 0)),
            scratch_shapes=[pltpu.VMEM((2, 2, tm, d), F32), pltpu.SemaphoreType.DMA((2,))],
        ),
        out_shape=jax.ShapeDtypeStruct((t, d), F32),
        compiler_params=_params(("arbitrary",), 12 * tm * d * 4 + (2 << 20)),
        name="moe_combine_residual",
    )(pos, x, wts, ys, g, gate)


def _rope(x, cs, sn):
    lane = lax.broadcasted_iota(jnp.int32, x.shape, 1)
    first = (lane % (2 * ROPE_PAIRS)) < ROPE_PAIRS
    partner = jnp.where(first, pltpu.roll(x, HEAD_DIM - ROPE_PAIRS, 1), pltpu.roll(x, ROPE_PAIRS, 1))
    return x * cs + partner * sn


def _shift_rows(u, halo_prev, halo_next):
    n = u.shape[0]
    row = lax.broadcasted_iota(jnp.int32, u.shape, 0)
    prev = jnp.where(row == 0, halo_prev, pltpu.roll(u, 1, 0))
    nxt = jnp.where(row == n - 1, halo_next, pltpu.roll(u, n - 1, 0))
    return prev, nxt


def _mixer_kernel(sink_ref, p_ref, kvp_ref, kvn_ref, cp_ref, hp_ref, cn_ref, hn_ref, ckv_ref,
                  cs_ref, sn_ref, csp_ref, snp_ref, csn_ref, snn_ref, cw_ref, o_ref):
    i = pl.program_id(0)
    nb = pl.num_programs(0)
    cs, sn = cs_ref[...], sn_ref[...]

    last = BF16_SUBLANES - 1
    for c0 in range(0, CONV_WIDTH, LANES):
        cl = slice(c0, c0 + LANES)
        cc = slice(CONV_WIDTH + c0, CONV_WIDTH + c0 + LANES)
        ch = slice(2 * CONV_WIDTH + c0, 2 * CONV_WIDTH + c0 + LANES)
        cu = p_ref[:, cc].astype(F32) * p_ref[:, ch].astype(F32)
        halo_p = cp_ref[last:last + 1, cl].astype(F32) * hp_ref[last:last + 1, cl].astype(F32)
        halo_n = cn_ref[0:1, cl].astype(F32) * hn_ref[0:1, cl].astype(F32)
        halo_p = jnp.where(i > 0, halo_p, 0.0)
        halo_n = jnp.where(i < nb - 1, halo_n, 0.0)
        cu_prev, cu_next = _shift_rows(cu, halo_p, halo_n)
        y_conv = p_ref[:, cl].astype(F32) * (cu_prev * cw_ref[0:1, cl] + cu * cw_ref[1:2, cl] + cu_next * cw_ref[2:3, cl])
        o_ref[:, cl] = y_conv.astype(o_ref.dtype)

    n_ctx = ckv_ref.shape[0]
    n_keys = 3 * BLOCK + n_ctx
    qi = lax.broadcasted_iota(jnp.int32, (BLOCK, n_keys), 0)
    kj = lax.broadcasted_iota(jnp.int32, (BLOCK, n_keys), 1)
    key_pos = i * BLOCK + kj - BLOCK
    visible = (kj >= 3 * BLOCK) | ((jnp.abs(qi + BLOCK - kj) <= BLOCK) & (key_pos >= 0) & (key_pos < nb * BLOCK))
    nt = (((1,), (1,)), ((), ()))

    for kh in range(N_KV_HEADS):
        ko = KV_START + kh * HEAD_DIM
        vo = KV_START + KV_WIDTH + kh * HEAD_DIM
        kb = kh * HEAD_DIM
        vb = KV_WIDTH + kh * HEAD_DIM
        keys = jnp.concatenate([
            _rope(kvp_ref[:, kb:kb + HEAD_DIM].astype(F32), csp_ref[...], snp_ref[...]).astype(BF16),
            _rope(p_ref[:, ko:ko + HEAD_DIM].astype(F32), cs, sn).astype(BF16),
            _rope(kvn_ref[:, kb:kb + HEAD_DIM].astype(F32), csn_ref[...], snn_ref[...]).astype(BF16),
            ckv_ref[:, kb:kb + HEAD_DIM],
        ], axis=0)
        vals = jnp.concatenate([kvp_ref[:, vb:vb + HEAD_DIM], p_ref[:, vo:vo + HEAD_DIM],
                                kvn_ref[:, vb:vb + HEAD_DIM], ckv_ref[:, vb:vb + HEAD_DIM]], axis=0)
        for g in range(GQA_GROUP):
            h = kh * GQA_GROUP + g
            qo = Q_START + h * HEAD_DIM
            q = (_rope(p_ref[:, qo:qo + HEAD_DIM].astype(F32), cs, sn) * ATTN_SCALE).astype(BF16)
            sc = jnp.where(visible, lax.dot_general(q, keys, nt, preferred_element_type=F32), -jnp.inf)
            sink = sink_ref[h]
            m = jnp.maximum(jnp.max(sc, axis=1, keepdims=True), sink)
            e = jnp.exp(sc - m)
            denom = jnp.exp(sink - m) + jnp.sum(e, axis=1, keepdims=True)
            o = jnp.dot(e.astype(BF16), vals, preferred_element_type=F32) / denom
            oo = CONV_WIDTH + h * HEAD_DIM
            o_ref[:, oo:oo + HEAD_DIM] = o.astype(o_ref.dtype)


def even_mixer_core(p, ckv, cs, sn, conv_w, sink):
    s = p.shape[0]
    nb = s // BLOCK
    hb = BLOCK // BF16_SUBLANES
    kvc = KV_START // (2 * KV_WIDTH)
    prev = lambda i: jnp.maximum(i - 1, 0)
    nxt = lambda i: jnp.minimum(i + 1, nb - 1)
    tab = lambda f: pl.BlockSpec((BLOCK, HEAD_DIM), lambda i: (f(i), 0))
    same = lambda i: i
    in_specs = [
        pl.BlockSpec(memory_space=pltpu.SMEM),
        pl.BlockSpec((BLOCK, IN_WIDTH), lambda i: (i, 0)),
        pl.BlockSpec((BLOCK, 2 * KV_WIDTH), lambda i: (prev(i), kvc)),
        pl.BlockSpec((BLOCK, 2 * KV_WIDTH), lambda i: (nxt(i), kvc)),
        pl.BlockSpec((BF16_SUBLANES, CONV_WIDTH), lambda i: (jnp.maximum(i * hb - 1, 0), 1)),
        pl.BlockSpec((BF16_SUBLANES, CONV_WIDTH), lambda i: (jnp.maximum(i * hb - 1, 0), 2)),
        pl.BlockSpec((BF16_SUBLANES, CONV_WIDTH), lambda i: (jnp.minimum((i + 1) * hb, nb * hb - 1), 1)),
        pl.BlockSpec((BF16_SUBLANES, CONV_WIDTH), lambda i: (jnp.minimum((i + 1) * hb, nb * hb - 1), 2)),
        pl.BlockSpec(ckv.shape, lambda i: (0, 0)),
        tab(same), tab(same), tab(prev), tab(prev), tab(nxt), tab(nxt),
        pl.BlockSpec(conv_w.shape, lambda i: (0, 0)),
    ]
    return pl.pallas_call(
        _mixer_kernel,
        grid=(nb,),
        in_specs=in_specs,
        out_specs=pl.BlockSpec((BLOCK, CONV_WIDTH + Q_WIDTH), lambda i: (i, 0)),
        out_shape=jax.ShapeDtypeStruct((s, CONV_WIDTH + Q_WIDTH), BF16),
        compiler_params=_params(("parallel",), 32 << 20),
        name="even_mixer_core",
    )(sink, p, p, p, p, p, p, p, ckv, cs, sn, cs, sn, cs, sn, conv_w)


def rope_tables(s):
    t = np.arange(s)
    inv = ROPE_BASE ** (-np.arange(ROPE_PAIRS, dtype=np.float64) / ROPE_PAIRS)
    ang_r = (t // GRID_W)[:, None] * inv[None, :]
    ang_c = (t % GRID_W)[:, None] * inv[None, :]
    cs = np.concatenate([np.cos(ang_r)] * 2 + [np.cos(ang_c)] * 2, axis=1)
    sn = np.concatenate([-np.sin(ang_r), np.sin(ang_r), -np.sin(ang_c), np.sin(ang_c)], axis=1)
    return cs.astype(np.float32), sn.astype(np.float32)


def _hyena_in_kernel(*refs, side_chunks):
    a_ref, ap_ref, an_ref, w_ref, b_ref, cw_ref, cb_ref = refs[:7]
    o_ref, ax_ref = (refs[8], refs[10]) if side_chunks else (refs[7], refs[8])
    i, j = pl.program_id(0), pl.program_id(1)
    halo = ap_ref.shape[0]
    tm = a_ref.shape[0]
    rows = ax_ref.shape[0]

    @pl.when(j == 0)
    def _():
        ax_ref[0:halo, :] = ap_ref[...]
        ax_ref[halo:halo + tm, :] = a_ref[...]
        ax_ref[halo + tm:, :] = an_ref[...]

    for c0 in range(0, w_ref.shape[1], V7X_MXU_COLS):
        cl = slice(c0, c0 + V7X_MXU_COLS)
        u = jnp.dot(ax_ref[...], w_ref[:, cl], preferred_element_type=F32) + b_ref[:, cl]
        r = lax.broadcasted_iota(jnp.int32, u.shape, 0)
        prev = jnp.where((r == halo) & (i == 0), 0.0, pltpu.roll(u, 1, 0))
        nxt = jnp.where((r == halo + tm - 1) & (i == pl.num_programs(0) - 1), 0.0, pltpu.roll(u, rows - 1, 0))
        y = prev * cw_ref[0:1, cl] + u * cw_ref[1:2, cl] + nxt * cw_ref[2:3, cl] + cb_ref[:, cl]
        o_ref[:, cl] = y[halo:halo + tm].astype(o_ref.dtype)

    if side_chunks:
        _side_cast(refs[7], refs[9], i * pl.num_programs(1) + j, side_chunks)


def hyena_in_proj(a, w, bias, conv_w, conv_b, col0, width, out_dtype, side=None, side_chunks=0, tm=1024, tn=512):
    s, k = a.shape
    halo = BF16_SUBLANES
    nj, j0 = width // tn, col0 // tn
    hb = tm // halo
    col = lambda rows: pl.BlockSpec((rows, tn), lambda i, j: (0, j + j0))
    in_specs = [
        pl.BlockSpec((tm, k), lambda i, j: (i, 0)),
        pl.BlockSpec((halo, k), lambda i, j: (jnp.maximum(i * hb - 1, 0), 0)),
        pl.BlockSpec((halo, k), lambda i, j: (jnp.minimum((i + 1) * hb, s // halo - 1), 0)),
        pl.BlockSpec((k, tn), lambda i, j: (0, j + j0)),
        col(1), col(3), col(1),
    ]
    args = [a, a, a, w, bias, conv_w, conv_b]
    osz = jnp.dtype(out_dtype).itemsize
    vmem = 2 * tm * k * 2 + (tm + 2 * halo) * k * 2 + 2 * k * tn * 2 + 2 * tm * tn * osz + 8 * tm * tn * 4 + (2 << 20)
    out_specs = pl.BlockSpec((tm, tn), lambda i, j: (i, j))
    out_shape = jax.ShapeDtypeStruct((s, width), out_dtype)
    if side is not None:
        assert side_chunks <= (s // tm) * nj
        spec, shape, side_vmem = _side_cast_plumbing(side, side_chunks, lambda i, j: i * nj + j)
        in_specs.append(spec)
        args.append(side)
        out_specs, out_shape, vmem = [out_specs, spec], [out_shape, shape], vmem + side_vmem
    return pl.pallas_call(
        functools.partial(_hyena_in_kernel, side_chunks=side_chunks),
        grid=(s // tm, nj),
        in_specs=in_specs,
        out_specs=out_specs,
        out_shape=out_shape,
        scratch_shapes=[pltpu.VMEM((tm + 2 * halo, k), BF16)],
        compiler_params=_params(("arbitrary", "arbitrary"), vmem),
        name="hyena_in_proj",
    )(*args)


def _hid_kernel(z_ref, w1_ref, b1_ref, w2_ref, b2_ref, w3_ref, b3_ref, fr_ref, o_ref):
    hp = lax.Precision.HIGHEST
    fr = fr_ref[...]
    h = jnp.sin(fr * (jnp.dot(z_ref[...], w1_ref[...], preferred_element_type=F32, precision=hp) + b1_ref[...]))
    h = jnp.sin(fr * (jnp.dot(h, w2_ref[...], preferred_element_type=F32, precision=hp) + b2_ref[...]))
    o_ref[...] = jnp.sin(fr * (jnp.dot(h, w3_ref[...], preferred_element_type=F32, precision=hp) + b3_ref[...]))


def _pad2(a, r, c):
    return jnp.zeros((r, c), F32).at[:a.shape[0], :a.shape[1]].set(a.astype(F32))


def filter_hidden(length, w1, b1, freq, w2, b2, w3, b3, tl=2048):
    t = np.linspace(0.0, 1.0, length)[:, None]
    w = (2.0 * math.pi / length) * np.arange(length)[:, None]
    bands = np.linspace(1e-4, FILTER_BANDS - 1, FILTER_BANDS)[None]
    z = np.concatenate([t, np.cos(bands * w), -np.sin(bands * w)], axis=-1)
    zp = np.zeros((length, LANES), np.float32)
    zp[:, :FILTER_EMB] = z
    full = pl.BlockSpec((LANES, LANES), lambda i: (0, 0))
    row = pl.BlockSpec((1, LANES), lambda i: (0, 0))
    return pl.pallas_call(
        _hid_kernel,
        grid=(length // tl,),
        in_specs=[pl.BlockSpec((tl, LANES), lambda i: (i, 0)), full, row, full, row, full, row, row],
        out_specs=pl.BlockSpec((tl, LANES), lambda i: (i, 0)),
        out_shape=jax.ShapeDtypeStruct((length, LANES), F32),
        compiler_params=_params(("parallel",), 16 << 20),
        name="filter_hidden",
    )(zp, _pad2(w1, LANES, LANES), _pad2(b1[None], 1, LANES), _pad2(w2, LANES, LANES), _pad2(b2[None], 1, LANES),
      _pad2(w3, LANES, LANES), _pad2(b3[None], 1, LANES), _pad2(freq[None], 1, LANES))


FFT_N1 = 256
FFT_N2 = 128
FFT_K1 = FFT_N1 // 2
FFT_GROUP = BF16_SUBLANES
FFT_PITCH = 3 * F32_SUBLANES


@functools.lru_cache(maxsize=None)
def _phase_tables(length):
    n1h, n2 = FFT_K1, FFT_N2
    assert length == n1h * n2
    n = 2 * length
    ia = np.arange(n1h, dtype=np.int64)
    tha = ((ia[None, :] * (2 * ia[:, None] + 1)) % (2 * FFT_N1)) * (2.0 * np.pi / (2 * FFT_N1))
    fa = np.concatenate([np.cos(tha), -np.sin(tha)], axis=0)
    ca = np.concatenate([np.cos(tha).T, -np.sin(tha).T], axis=1) * (2.0 / n)
    k1 = ia[:, None, None]
    k2 = np.arange(n2, dtype=np.int64)[None, :, None]
    m2 = np.arange(n2, dtype=np.int64)[None, None, :]
    phb = ((m2 * (2 * (k1 + FFT_N1 * k2) + 1)) % (2 * n)) * (2.0 * np.pi / (2 * n))
    gr, gi = np.cos(phb), -np.sin(phb)
    gb = np.concatenate([np.concatenate([gr, -gi], axis=2), np.concatenate([gi, gr], axis=2)], axis=1)
    hr, hi = np.swapaxes(gr, 1, 2), -np.swapaxes(gi, 1, 2)
    gbi = np.concatenate([np.concatenate([hr, -hi], axis=2), np.concatenate([hi, hr], axis=2)], axis=1)
    return tuple(t.astype(np.float32).astype(BF16) for t in (fa, ca, gb, gbi))


def _regroup_rows(k):
    return pl.ds(pl.multiple_of(k * FFT_PITCH, F32_SUBLANES), FFT_GROUP)


def _fft_a_core(rows_of, fa_ref, o_ref, s_refs, ns):
    fa = fa_ref[...]
    for j in range(FFT_GROUP):
        res = jnp.dot(fa, rows_of(j).astype(BF16), preferred_element_type=F32)
        for p in range(2):
            for s in range(ns):
                s_refs[p * ns + s][pl.ds(j, FFT_K1, stride=FFT_PITCH), :] = (
                    res[p * FFT_K1:(p + 1) * FFT_K1, s * LANES:(s + 1) * LANES])

    def emit(k, carry):
        for p in range(2):
            for s in range(ns):
                o_ref[0, k, p, :, s * LANES:(s + 1) * LANES] = s_refs[p * ns + s][_regroup_rows(k), :].astype(o_ref.dtype)
        return carry

    lax.fori_loop(0, FFT_K1, emit, 0, unroll=8)


def _fft_a_kernel(*refs, ns):
    x_refs, fa_ref, o_ref = refs[:ns], refs[ns], refs[ns + 1]
    xs_refs = refs[ns + 2:2 * ns + 2]
    for s in range(ns):
        xs_refs[s][...] = x_refs[s][...].reshape(FFT_K1 * FFT_GROUP, LANES)
    rows_of = lambda j: jnp.concatenate([r[pl.ds(j, FFT_K1, stride=FFT_GROUP), :] for r in xs_refs], axis=1)
    _fft_a_core(rows_of, fa_ref, o_ref, refs[2 * ns + 2:], ns)


def _fft_a_filter_kernel(hid_ref, w_ref, delta_ref, fa_ref, o_ref, n_ref, hs_ref, *s_refs, ns, length):
    g = pl.program_id(2)

    @pl.when(g == 0)
    def _():
        n_ref[...] = jnp.zeros_like(n_ref)

    hs_ref[...] = hid_ref[...].reshape(FFT_K1 * FFT_GROUP, LANES)
    w = w_ref[0].astype(BF16)
    delta = delta_ref[...]
    n1 = lax.broadcasted_iota(jnp.int32, (FFT_K1, ns * LANES), 0)
    decay_slab = jnp.exp(-((n1 * FFT_N2).astype(F32) / float(length - 1)) * delta)

    def rows_of(j):
        hid = hs_ref[pl.ds(j, FFT_K1, stride=FFT_GROUP), :]
        decay_row = jnp.exp(-((g * FFT_GROUP + j).astype(F32) / float(length - 1)) * delta)
        f = jnp.dot(hid.astype(BF16), w, preferred_element_type=F32) * decay_slab * decay_row
        n_ref[0, 0:1, :] += jnp.sum(jnp.abs(f), axis=0, keepdims=True)
        return f

    _fft_a_core(rows_of, fa_ref, o_ref, s_refs, ns)


def _fft_a_scratch(ns):
    return [pltpu.VMEM((FFT_K1 * FFT_PITCH, LANES), F32)] * (2 * ns)


def _fft_a_vmem(ns):
    blk = FFT_K1 * FFT_GROUP * LANES
    return 2 * ns * blk * 4 + 2 * 2 * ns * blk * 2 + ns * blk * 4 + 2 * ns * FFT_K1 * FFT_PITCH * LANES * 4 + (8 << 20)


def fft_filter_pass_a(hid, w_o, length, dc=256):
    q, _, d = w_o.shape
    ns = dc // LANES
    delta = jnp.abs(jnp.linspace(MIN_DECAY, MAX_DECAY, d, dtype=F32))[None]
    hid3 = hid.reshape(FFT_K1, FFT_N2, LANES)
    fa = _phase_tables(length)[0]
    return pl.pallas_call(
        functools.partial(_fft_a_filter_kernel, ns=ns, length=length),
        grid=(q, d // dc, FFT_N2 // FFT_GROUP),
        in_specs=[pl.BlockSpec((FFT_K1, FFT_GROUP, LANES), lambda w, ci, g: (0, g, 0)),
                  pl.BlockSpec((1, LANES, dc), lambda w, ci, g: (w, 0, ci)),
                  pl.BlockSpec((1, dc), lambda w, ci, g: (0, ci)),
                  pl.BlockSpec(fa.shape, lambda w, ci, g: (0, 0))],
        out_specs=[pl.BlockSpec((1, FFT_K1, 2, FFT_GROUP, dc), lambda w, ci, g: (w, 0, 0, g, ci)),
                   pl.BlockSpec((1, F32_SUBLANES, dc), lambda w, ci, g: (w, 0, ci))],
        out_shape=[jax.ShapeDtypeStruct((q, FFT_K1, 2, FFT_N2, d), BF16),
                   jax.ShapeDtypeStruct((q, F32_SUBLANES, d), F32)],
        scratch_shapes=[pltpu.VMEM((FFT_K1 * FFT_GROUP, LANES), F32)] + _fft_a_scratch(ns),
        compiler_params=_params(("parallel", "parallel", "arbitrary"), _fft_a_vmem(ns)),
        name="fft_filter_pass_a",
    )(hid3, w_o, delta, fa)


def fft_pass_a(x, col0, d, dc=256):
    q, length, c = x.shape
    ns = dc // LANES
    x4 = x.reshape(q, FFT_K1, FFT_N2, c)
    slab0 = col0 // LANES
    in_specs = [pl.BlockSpec((None, FFT_K1, FFT_GROUP, LANES),
                             functools.partial(lambda w, ci, g, s: (w, 0, g, slab0 + ci * ns + s), s=s))
                for s in range(ns)]
    fa = _phase_tables(length)[0]
    in_specs.append(pl.BlockSpec(fa.shape, lambda w, ci, g: (0, 0)))
    return pl.pallas_call(
        functools.partial(_fft_a_kernel, ns=ns),
        grid=(q, d // dc, FFT_N2 // FFT_GROUP),
        in_specs=in_specs,
        out_specs=pl.BlockSpec((1, FFT_K1, 2, FFT_GROUP, dc), lambda w, ci, g: (w, 0, 0, g, ci)),
        out_shape=jax.ShapeDtypeStruct((q, FFT_K1, 2, FFT_N2, d), BF16),
        scratch_shapes=[pltpu.VMEM((FFT_K1 * FFT_GROUP, LANES), F32)] * ns + _fft_a_scratch(ns),
        compiler_params=_params(("parallel", "parallel", "parallel"), _fft_a_vmem(ns)),
        name="fft_pass_a",
    )(*([x4] * ns), fa)


def _fft_b_kernel(a_ref, af_ref, ab_ref, gb_ref, gbi_ref, o_ref):
    n2 = FFT_N2
    dc = a_ref.shape[2]
    for i in range(a_ref.shape[0]):
        rhs = jnp.concatenate([a_ref[i], af_ref[0, i], ab_ref[0, i]], axis=1)
        res = jnp.dot(gb_ref[i], rhs, preferred_element_type=F32)
        u, uf, ub = res[:, 0:dc], res[:, dc:2 * dc], res[:, 2 * dc:3 * dc]
        ur, ui = u[0:n2], u[n2:2 * n2]
        kr, ki = uf[0:n2] + ub[0:n2], uf[n2:2 * n2] - ub[n2:2 * n2]
        v = jnp.concatenate([ur * kr - ui * ki, ur * ki + ui * kr], axis=0).astype(BF16)
        o_ref[i] = jnp.dot(gbi_ref[i], v, preferred_element_type=F32).astype(o_ref.dtype)


def fft_pass_b(a, filt_a, order, gb, gbi, kb=8, dc=512):
    k1, r, d = a.shape
    vmem = 2 * 4 * kb * r * dc * 2 + 2 * 2 * kb * r * r * 2 + 10 * r * dc * 4 + (4 << 20)
    return pl.pallas_call(
        _fft_b_kernel,
        grid=(k1 // kb, d // dc),
        in_specs=[pl.BlockSpec((kb, r, dc), lambda k, c: (k, 0, c)),
                  pl.BlockSpec((1, kb, r, dc), lambda k, c: (2 * order, k, 0, c)),
                  pl.BlockSpec((1, kb, r, dc), lambda k, c: (2 * order + 1, k, 0, c)),
                  pl.BlockSpec((kb, r, r), lambda k, c: (k, 0, 0)),
                  pl.BlockSpec((kb, r, r), lambda k, c: (k, 0, 0))],
        out_specs=pl.BlockSpec((kb, r, dc), lambda k, c: (k, 0, c)),
        out_shape=jax.ShapeDtypeStruct((k1, r, d), BF16),
        compiler_params=_params(("parallel", "parallel"), vmem),
        name="fft_pass_b",
    )(a, filt_a, filt_a, gb, gbi)


def _fft_c_kernel(*refs, ns):
    b_ref, ca_ref, gate_refs, z_refs = refs[0], refs[1], refs[2:2 + ns], refs[2 + ns:2 + 2 * ns]
    n_ref, bias_ref, o_ref = refs[2 + 2 * ns:5 + 2 * ns]
    s_refs = refs[5 + 2 * ns:5 + 4 * ns]
    t_refs = refs[5 + 4 * ns:]
    ca = ca_ref[...]

    def spread(k, carry):
        for p in range(2):
            for s in range(ns):
                s_refs[p * ns + s][_regroup_rows(k), :] = b_ref[k, p, :, s * LANES:(s + 1) * LANES].astype(F32)
        return carry

    lax.fori_loop(0, FFT_K1, spread, 0, unroll=8)
    for j in range(FFT_GROUP):
        b = jnp.concatenate([
            jnp.concatenate([s_refs[p * ns + s][pl.ds(j, FFT_K1, stride=FFT_PITCH), :] for s in range(ns)], axis=1)
            for p in range(2)], axis=0)
        y = jnp.dot(ca, b.astype(BF16), preferred_element_type=F32)
        for s in range(ns):
            t_refs[s][pl.ds(j, FFT_K1, stride=FFT_PITCH), :] = y[:, s * LANES:(s + 1) * LANES]

    def emit(k, carry):
        for s in range(ns):
            sl = slice(s * LANES, (s + 1) * LANES)
            y = t_refs[s][_regroup_rows(k), :] / (n_ref[0:1, sl] + n_ref[1:2, sl] + 1e-6)
            o_ref[k, :, sl] = (gate_refs[s][k].astype(F32) * (y + bias_ref[:, sl] * z_refs[s][k])).astype(o_ref.dtype)
        return carry

    lax.fori_loop(0, FFT_K1, emit, 0, unroll=8)


def fft_pass_c(bp, ca, gate_arr, gate_col0, z_arr, z_col0, nsum, bias, out_dtype, dc=256):
    k1, r, d = bp.shape
    ns = dc // LANES
    length = FFT_K1 * FFT_N2
    b4 = bp.reshape(k1, 2, FFT_N2, d)
    g3 = gate_arr.reshape(FFT_K1, FFT_N2, gate_arr.shape[1])
    z3 = z_arr.reshape(FFT_K1, FFT_N2, z_arr.shape[1])
    sig = lambda slab0: [pl.BlockSpec((FFT_K1, FFT_GROUP, LANES),
                                      functools.partial(lambda ci, g, s: (0, g, slab0 + ci * ns + s), s=s))
                         for s in range(ns)]
    row = pl.BlockSpec((1, dc), lambda ci, g: (0, ci))
    blk = FFT_K1 * FFT_GROUP * dc
    vmem = 2 * 2 * blk * 2 + 2 * 3 * blk * 4 + 3 * ns * FFT_K1 * FFT_PITCH * LANES * 4 + (8 << 20)
    out = pl.pallas_call(
        functools.partial(_fft_c_kernel, ns=ns),
        grid=(d // dc, FFT_N2 // FFT_GROUP),
        in_specs=[pl.BlockSpec((k1, 2, FFT_GROUP, dc), lambda ci, g: (0, 0, g, ci)),
                  pl.BlockSpec(ca.shape, lambda ci, g: (0, 0)),
                  *sig(gate_col0 // LANES), *sig(z_col0 // LANES),
                  pl.BlockSpec((2, dc), lambda ci, g: (0, ci)), row],
        out_specs=pl.BlockSpec((FFT_K1, FFT_GROUP, dc), lambda ci, g: (0, g, ci)),
        out_shape=jax.ShapeDtypeStruct((FFT_K1, FFT_N2, d), out_dtype),
        scratch_shapes=[pltpu.VMEM((FFT_K1 * FFT_PITCH, LANES), F32)] * (3 * ns),
        compiler_params=_params(("parallel", "parallel"), vmem),
        name="fft_pass_c",
    )(b4, ca, *([g3] * ns), *([z3] * ns), nsum, bias)
    return out.reshape(length, d)


def kernel(x, c, ctx, c_ctx, ada_w, ada_b, norm_g, mix_w_in, mix_conv_w, mix_sink, mix_w_out, ffn_w_gu, ffn_w_down, hy_w_in, hy_b_in, hy_conv_w, hy_conv_b, hf_w1, hf_b1, hf_freq, hf_w2, hf_b2, hf_w3, hf_b3, hf_w_out, hf_bias, hy_w_out, hy_b_out, router_w, router_b, moe_w_gu, moe_w_down):
    assert x.shape[0] == 1 and ada_w.shape[0] == 2
    s, d = x.shape[1], x.shape[2]
    xs = x[0]
    ctxs = ctx[0]

    acts = jnp.zeros((F32_SUBLANES, d), F32).at[0].set(c[0]).at[1].set(c_ctx)
    mods = adaln_all(acts, ada_w, ada_b)
    row = lambda layer, r, k: mods[layer, r:r + 1, k * d:(k + 1) * d]

    g = norm_g[0]
    h = norm_mod(xs, g[0:1], row(0, 0, 0), row(0, 0, 1))
    hc = norm_mod(ctxs, g[0:1], row(0, 1, 0), row(0, 1, 1))
    w_in = mix_w_in[0].astype(BF16)
    p, ffn_gu = matmul(h, w_in, out_dtype=BF16, side=ffn_w_gu[0], side_chunks=256)
    ckv = matmul(hc, w_in[:, KV_START:], out_dtype=BF16)
    cs, sn = rope_tables(s)
    y = even_mixer_core(p, ckv, cs, sn, mix_conv_w[0], mix_sink[0])
    out = matmul(y, mix_w_out[0].astype(BF16), out_dtype=BF16)
    xs, h = post_norm(xs, out, g[1:2], row(0, 0, 2), g[2:3], row(0, 0, 3), row(0, 0, 4))

    act, ffn_down = swiglu_up(h, ffn_gu, tm=2048, tn=256, side=ffn_w_down[0], side_chunks=344)
    out, hy_in = matmul(act, ffn_down, out_dtype=BF16, tm=1024, tn=256, a_buffers=1, side=hy_w_in[0], side_chunks=256)
    g1 = norm_g[1]
    xs, h = post_norm(xs, out, g[3:4], row(0, 0, 5), g1[0:1], row(1, 0, 0), row(1, 0, 1))

    g = g1
    moe_gu_f32 = moe_w_gu[0].reshape(-1, moe_w_gu.shape[-1])
    hy_args = (h, hy_in, hy_b_in[0][None], hy_conv_w[0], hy_conv_b[0][None])
    x12, moe_gu = hyena_in_proj(*hy_args, 0, 2 * d, BF16, side=moe_gu_f32, side_chunks=256)
    v = hyena_in_proj(*hy_args, 2 * d, d, F32)
    hid = filter_hidden(s, hf_w1[0], hf_b1[0], hf_freq[0], hf_w2[0], hf_b2[0], hf_w3[0], hf_b3[0])
    w_o = hf_w_out[0].astype(F32).reshape(FILTER_HIDDEN, HYENA_ORDER * 2, d).transpose(1, 0, 2)
    w_o = jnp.zeros((HYENA_ORDER * 2, LANES, d), F32).at[:, :FILTER_HIDDEN].set(w_o)
    filt_a, fsum = fft_filter_pass_a(hid, w_o, s)
    fsum = fsum[:, 0, :]
    _, ca, gb, gbi = _phase_tables(s)
    spec = lambda a: a.reshape(a.shape[0], FFT_K1, 2 * FFT_N2, d)
    filt_a = spec(filt_a)
    a = spec(fft_pass_a(v[None], 0, d))[0]
    z1 = fft_pass_c(fft_pass_b(a, filt_a, 0, gb, gbi), ca, x12, 0, v, 0, fsum[0:2], hf_bias[0, 0][None], F32)
    a = spec(fft_pass_a(z1[None], 0, d))[0]
    z2 = fft_pass_c(fft_pass_b(a, filt_a, 1, gb, gbi), ca, x12, d, z1, 0, fsum[2:4], hf_bias[0, 1][None], BF16)
    moe_down_f32 = moe_w_down[0].reshape(-1, moe_w_down.shape[-1])
    out, moe_down = matmul(z2, hy_w_out[0].astype(BF16), bias=hy_b_out[0][None], out_dtype=BF16,
                           side=moe_down_f32, side_chunks=128)
    xs, sel, wts = post_router_top2(xs, out, g[1:2], row(1, 0, 2), g[2:3], row(1, 0, 3), row(1, 0, 4),
                                    router_w[0], router_b[0])
    pos, src_tok, tile_expert, n_valid = moe_plan(sel)
    act = grouped_swiglu_up(xs, src_tok, g[2:3], row(1, 0, 3), row(1, 0, 4), moe_gu.reshape(moe_w_gu.shape[1:]),
                            tile_expert, n_valid)
    ys = grouped_down(act, moe_down.reshape(moe_w_down.shape[1:]), tile_expert, n_valid)
    xs = moe_combine_residual(xs, wts, ys, pos, g[3:4], row(1, 0, 5))
    return xs[None]
```

```python
import functools
import math

import jax
import jax.numpy as jnp
import numpy as np
from jax import lax
from jax.experimental import pallas as pl
from jax.experimental.pallas import tpu as pltpu

F32 = jnp.float32
BF16 = jnp.bfloat16

D_MODEL = 4096
GRID_W = 64
HEAD_DIM = 128
CONV_WIDTH = D_MODEL // 2
N_HEADS = (D_MODEL // 2) // HEAD_DIM
N_KV_HEADS = N_HEADS // 4
GQA_GROUP = N_HEADS // N_KV_HEADS
Q_WIDTH = N_HEADS * HEAD_DIM
KV_WIDTH = N_KV_HEADS * HEAD_DIM
Q_START = 3 * CONV_WIDTH
KV_START = Q_START + Q_WIDTH
IN_WIDTH = KV_START + 2 * KV_WIDTH
BLOCK = 128
ATTN_SCALE = HEAD_DIM ** -0.5
ROPE_BASE = 10000.0
ROPE_PAIRS = HEAD_DIM // 4
HYENA_ORDER = 2
FILTER_EMB = 33
FILTER_BANDS = (FILTER_EMB - 1) // 2
FILTER_HIDDEN = 64
MIN_DECAY = math.log(1e-2) / 0.3
MAX_DECAY = math.log(1e-2) / 1.5
N_EXPERTS = 8
NORM_EPS = 1e-6

V7X_VMEM_BYTES = 64 * 1024 * 1024
V7X_MXU_COLS = 256
LANES = 128
F32_SUBLANES = 8
BF16_SUBLANES = 16


def _params(semantics, vmem_bytes):
    limit = min(int(vmem_bytes), V7X_VMEM_BYTES - 4 * 1024 * 1024)
    return pltpu.CompilerParams(dimension_semantics=semantics, vmem_limit_bytes=limit)


def _sigmoid(v):
    return 1.0 / (1.0 + jnp.exp(-v))


def _adaln_kernel(a_ref, w_ref, b_ref, o_ref, acc_ref):
    k = pl.program_id(2)

    @pl.when(k == 0)
    def _():
        acc_ref[...] = jnp.zeros_like(acc_ref)

    a = a_ref[...]
    a = a * _sigmoid(a)
    acc_ref[...] += jnp.dot(a.astype(BF16), w_ref[0].astype(BF16), preferred_element_type=F32)

    @pl.when(k == pl.num_programs(2) - 1)
    def _():
        o_ref[0] = acc_ref[...] + b_ref[0]


def adaln_all(acts, ada_w, ada_b):
    depth, d, n = ada_w.shape
    tn, tk = 2048, 1024
    return pl.pallas_call(
        _adaln_kernel,
        grid=(depth, n // tn, d // tk),
        in_specs=[
            pl.BlockSpec((F32_SUBLANES, tk), lambda l, j, k: (0, k)),
            pl.BlockSpec((1, tk, tn), lambda l, j, k: (l, k, j)),
            pl.BlockSpec((1, 1, tn), lambda l, j, k: (l, 0, j)),
        ],
        out_specs=pl.BlockSpec((1, F32_SUBLANES, tn), lambda l, j, k: (l, 0, j)),
        out_shape=jax.ShapeDtypeStruct((depth, F32_SUBLANES, n), F32),
        scratch_shapes=[pltpu.VMEM((F32_SUBLANES, tn), F32)],
        compiler_params=_params(("parallel", "parallel", "arbitrary"), 2 * tk * tn * 4 + tk * tn * 2 + (4 << 20)),
        name="adaln",
    )(acts, ada_w, ada_b.reshape(depth, 1, n))


def _norm_mod(x, g, sh, sc):
    y = x * lax.rsqrt(jnp.mean(x * x, axis=-1, keepdims=True) + NORM_EPS)
    return (y * g) * (1.0 + sc) + sh


def _norm_mod_kernel(x_ref, g_ref, sh_ref, sc_ref, o_ref):
    o_ref[...] = _norm_mod(x_ref[...], g_ref[...], sh_ref[...], sc_ref[...]).astype(o_ref.dtype)


def norm_mod(x, g, shift, scale, tm=512):
    m, d = x.shape
    tm = min(tm, m)
    row = pl.BlockSpec((1, d), lambda i: (0, 0))
    return pl.pallas_call(
        _norm_mod_kernel,
        grid=(m // tm,),
        in_specs=[pl.BlockSpec((tm, d), lambda i: (i, 0)), row, row, row],
        out_specs=pl.BlockSpec((tm, d), lambda i: (i, 0)),
        out_shape=jax.ShapeDtypeStruct((m, d), BF16),
        compiler_params=_params(("parallel",), 2 * tm * d * 4 + 2 * tm * d * 2 + 3 * tm * d * 4 + (2 << 20)),
        name="norm_mod",
    )(x, g, shift, scale)


def _post_norm_kernel(x_ref, y_ref, g_ref, gt_ref, g2_ref, sh_ref, sc_ref, o_ref, h_ref):
    y = y_ref[...].astype(F32)
    yn = y * lax.rsqrt(jnp.mean(y * y, axis=-1, keepdims=True) + NORM_EPS)
    x = x_ref[...] + gt_ref[...] * (yn * g_ref[...])
    o_ref[...] = x
    h_ref[...] = _norm_mod(x, g2_ref[...], sh_ref[...], sc_ref[...]).astype(h_ref.dtype)


def post_norm(x, y, g, gate, g_next, shift, scale, tm=256):
    m, d = x.shape
    row = pl.BlockSpec((1, d), lambda i: (0, 0))
    blk = pl.BlockSpec((tm, d), lambda i: (i, 0))
    return pl.pallas_call(
        _post_norm_kernel,
        grid=(m // tm,),
        in_specs=[blk, blk, row, row, row, row, row],
        out_specs=[blk, blk],
        out_shape=[jax.ShapeDtypeStruct((m, d), F32), jax.ShapeDtypeStruct((m, d), BF16)],
        compiler_params=_params(("parallel",), 7 * tm * d * 4 + 5 * tm * d * 4 + (2 << 20)),
        name="post_norm",
    )(x, y, g, gate, g_next, shift, scale)


def _side_cast(src_ref, dst_ref, step, n_chunks):
    @pl.when(step < n_chunks)
    def _():
        dst_ref[...] = src_ref[...].astype(dst_ref.dtype)


def _side_cast_plumbing(side, n_chunks, step_of):
    rows, cols = side.shape
    spec = pl.BlockSpec((rows // n_chunks, cols), lambda *g: (jnp.minimum(step_of(*g), n_chunks - 1), 0))
    vmem = 2 * (rows // n_chunks) * cols * (4 + 2)
    return spec, jax.ShapeDtypeStruct((rows, cols), BF16), vmem


def _mm_kernel(*refs, has_bias, side_chunks):
    a_ref, w_ref = refs[0], refs[1]
    n_in = 2 + has_bias + (side_chunks > 0)
    o_ref = refs[n_in]
    acc = jnp.dot(a_ref[...], w_ref[...], preferred_element_type=F32)
    if has_bias:
        acc = acc + refs[2][...]
    o_ref[...] = acc.astype(o_ref.dtype)
    if side_chunks:
        step = pl.program_id(0) * pl.num_programs(1) + pl.program_id(1)
        _side_cast(refs[n_in - 1], refs[n_in + 1], step, side_chunks)


def matmul(a, w, bias=None, out_dtype=F32, tm=1024, tn=512, side=None, side_chunks=0, a_buffers=2):
    m, k = a.shape
    n = w.shape[1]
    tm, tn = min(tm, m), min(tn, n)
    nj = n // tn
    a_mode = {} if a_buffers == 2 else {"pipeline_mode": pl.Buffered(a_buffers)}
    in_specs = [pl.BlockSpec((tm, k), lambda i, j: (i, 0), **a_mode), pl.BlockSpec((k, tn), lambda i, j: (0, j))]
    args = [a, w]
    if bias is not None:
        in_specs.append(pl.BlockSpec((1, tn), lambda i, j: (0, j)))
        args.append(bias)
    osz = jnp.dtype(out_dtype).itemsize
    vmem = a_buffers * tm * k * 2 + 2 * k * tn * 2 + 2 * tm * tn * osz + 2 * tm * tn * 4 + (2 << 20)
    out_specs = pl.BlockSpec((tm, tn), lambda i, j: (i, j))
    out_shape = jax.ShapeDtypeStruct((m, n), out_dtype)
    if side is not None:
        assert side_chunks <= (m // tm) * nj
        spec, shape, side_vmem = _side_cast_plumbing(side, side_chunks, lambda i, j: i * nj + j)
        in_specs.append(spec)
        args.append(side)
        out_specs, out_shape, vmem = [out_specs, spec], [out_shape, shape], vmem + side_vmem
    return pl.pallas_call(
        functools.partial(_mm_kernel, has_bias=bias is not None, side_chunks=side_chunks),
        grid=(m // tm, nj),
        in_specs=in_specs,
        out_specs=out_specs,
        out_shape=out_shape,
        compiler_params=_params(("arbitrary", "arbitrary") if side is not None else ("parallel", "arbitrary"), vmem),
        name="matmul",
    )(*args)


def _swiglu_up_kernel(a_ref, wg_ref, wu_ref, side_ref, o_ref, side_o_ref, *, side_chunks):
    a = a_ref[...]
    g = jnp.dot(a, wg_ref[...], preferred_element_type=F32)
    u = jnp.dot(a, wu_ref[...], preferred_element_type=F32)
    o_ref[...] = (g * _sigmoid(g) * u).astype(o_ref.dtype)
    _side_cast(side_ref, side_o_ref, pl.program_id(0) * pl.num_programs(1) + pl.program_id(1), side_chunks)


def swiglu_up(a, w_gu, tm, tn, side, side_chunks):
    m, k = a.shape
    f = w_gu.shape[1] // 2
    nj = f // tn
    assert side_chunks <= (m // tm) * nj
    spec, shape, side_vmem = _side_cast_plumbing(side, side_chunks, lambda i, j: i * nj + j)
    vmem = 2 * tm * k * 2 + 4 * k * tn * 2 + 2 * tm * tn * 2 + 4 * tm * tn * 4 + side_vmem + (2 << 20)
    return pl.pallas_call(
        functools.partial(_swiglu_up_kernel, side_chunks=side_chunks),
        grid=(m // tm, nj),
        in_specs=[
            pl.BlockSpec((tm, k), lambda i, j: (i, 0)),
            pl.BlockSpec((k, tn), lambda i, j: (0, j)),
            pl.BlockSpec((k, tn), lambda i, j: (0, j + nj)),
            spec,
        ],
        out_specs=[pl.BlockSpec((tm, tn), lambda i, j: (i, j)), spec],
        out_shape=[jax.ShapeDtypeStruct((m, f), BF16), shape],
        compiler_params=_params(("arbitrary", "arbitrary"), vmem),
        name="swiglu_up",
    )(a, w_gu, w_gu, side)


def _post_router_kernel(x_ref, y_ref, gp_ref, gt_ref, g_ref, sh_ref, sc_ref, w_ref, b_ref, o_ref, sel_ref, wts_ref):
    y = y_ref[...].astype(F32)
    yn = y * lax.rsqrt(jnp.mean(y * y, axis=-1, keepdims=True) + NORM_EPS)
    x = x_ref[...] + gt_ref[...] * (yn * gp_ref[...])
    o_ref[...] = x
    h = _norm_mod(x, g_ref[...], sh_ref[...], sc_ref[...])
    logits = jnp.dot(h, w_ref[...], preferred_element_type=F32, precision=lax.Precision.HIGHEST) + b_ref[...]
    lane = lax.broadcasted_iota(jnp.int32, logits.shape, 1)
    neg = jnp.float32(-jnp.inf)
    logits = jnp.where(lane < N_EXPERTS, logits, neg)
    v1 = jnp.max(logits, axis=1, keepdims=True)
    i1 = jnp.min(jnp.where(logits == v1, lane, LANES), axis=1, keepdims=True)
    rest = jnp.where(lane == i1, neg, logits)
    v2 = jnp.max(rest, axis=1, keepdims=True)
    i2 = jnp.min(jnp.where(rest == v2, lane, LANES), axis=1, keepdims=True)
    e2 = jnp.exp(v2 - v1)
    w1 = 1.0 / (1.0 + e2)
    w2 = e2 / (1.0 + e2)
    sel_ref[...] = jnp.where(lane == 0, i1, jnp.where(lane == 1, i2, 0))
    wts_ref[...] = jnp.where(lane == 0, w1, jnp.where(lane == 1, w2, 0.0))


def post_router_top2(x, y, g_post, gate, g, shift, scale, router_w, router_b, tm=256):
    m, d = x.shape
    wp = jnp.zeros((d, LANES), F32).at[:, :N_EXPERTS].set(router_w)
    bp = jnp.zeros((1, LANES), F32).at[0, :N_EXPERTS].set(router_b)
    row = pl.BlockSpec((1, d), lambda i: (0, 0))
    blk = pl.BlockSpec((tm, d), lambda i: (i, 0))
    out = pl.BlockSpec((tm, LANES), lambda i: (i, 0))
    return pl.pallas_call(
        _post_router_kernel,
        grid=(m // tm,),
        in_specs=[blk, blk, row, row, row, row, row,
                  pl.BlockSpec((d, LANES), lambda i: (0, 0)), pl.BlockSpec((1, LANES), lambda i: (0, 0))],
        out_specs=[blk, out, out],
        out_shape=[jax.ShapeDtypeStruct((m, d), F32), jax.ShapeDtypeStruct((m, LANES), jnp.int32),
                   jax.ShapeDtypeStruct((m, LANES), F32)],
        compiler_params=_params(("parallel",), 6 * tm * d * 4 + 6 * tm * d * 4 + 2 * d * LANES * 4 + (4 << 20)),
        name="post_router",
    )(x, y, g_post, gate, g, shift, scale, wp, bp)


MOE_TILE = 512


def moe_plan(sel):
    t = sel.shape[0]
    npairs = 2 * t
    rows = npairs + N_EXPERTS * MOE_TILE
    e_flat = sel[:, :2].reshape(npairs)
    order = jnp.argsort(e_flat, stable=True).astype(jnp.int32)
    rank = jnp.argsort(order).astype(jnp.int32)
    counts = jnp.sum((e_flat[:, None] == jnp.arange(N_EXPERTS, dtype=jnp.int32)[None, :]).astype(jnp.int32), axis=0)
    padded = (counts + MOE_TILE - 1) // MOE_TILE * MOE_TILE
    ends_p = jnp.cumsum(padded)
    starts_p = ends_p - padded
    starts_u = jnp.cumsum(counts) - counts
    pos = starts_p[e_flat] + rank - starts_u[e_flat]
    tile_start = jnp.arange(rows // MOE_TILE, dtype=jnp.int32) * MOE_TILE
    tile_expert = jnp.minimum(jnp.searchsorted(ends_p, tile_start, side="right"), N_EXPERTS - 1).astype(jnp.int32)
    n_valid = (ends_p[-1] // MOE_TILE).astype(jnp.int32).reshape(1)
    r = jnp.arange(rows, dtype=jnp.int32)
    e_row = jnp.repeat(tile_expert, MOE_TILE)
    in_group = r - starts_p[e_row]
    valid = (in_group < counts[e_row]) & (r < ends_p[-1])
    src_tok = jnp.where(valid, order[jnp.clip(starts_u[e_row] + in_group, 0, npairs - 1)] // 2, 0)
    return pos.astype(jnp.int32), src_tok.astype(jnp.int32), tile_expert, n_valid


def _gather_rows(idx_ref, base, stride, src_hbm, dst_ref, sem, r0, n, wait):
    def body(r, carry):
        copy = pltpu.make_async_copy(src_hbm.at[pl.ds(idx_ref[base + stride * r], 1)], dst_ref.at[pl.ds(r, 1)], sem)
        if wait:
            copy.wait()
        else:
            copy.start()
        return carry

    lax.fori_loop(r0, r0 + n, body, 0, unroll=8)


def _gswiglu_kernel(te_ref, nv_ref, src_ref, x_hbm, g_ref, sh_ref, sc_ref, wg_ref, wu_ref, o_ref,
                    xbuf_ref, a_ref, sems):
    i, j = pl.program_id(0), pl.program_id(1)
    tm = a_ref.shape[0]
    n_valid = nv_ref[0]
    slot = i % 2

    def gather(tile, to_slot, wait):
        @pl.when(tile < n_valid)
        def _():
            _gather_rows(src_ref, tile * tm, 1, x_hbm, xbuf_ref.at[to_slot], sems.at[to_slot], 0, tm, wait=wait)

    @pl.when(j == 0)
    def _():
        @pl.when(i == 0)
        def _():
            gather(i, slot, False)

        gather(i, slot, True)
        gather(i + 1, 1 - slot, False)

        @pl.when(i < n_valid)
        def _():
            gain = g_ref[...] * (1.0 + sc_ref[...])
            rc = 2 * BF16_SUBLANES

            def norm_rows(c, carry):
                rows = pl.ds(pl.multiple_of(c * rc, rc), rc)
                x = xbuf_ref[slot, rows, :]
                y = x * lax.rsqrt(jnp.mean(x * x, axis=-1, keepdims=True) + NORM_EPS)
                a_ref[rows, :] = (y * gain + sh_ref[...]).astype(a_ref.dtype)
                return carry

            lax.fori_loop(0, tm // rc, norm_rows, 0, unroll=2)

    @pl.when(i < n_valid)
    def _():
        a = a_ref[...]
        g = jnp.dot(a, wg_ref[0], preferred_element_type=F32)
        u = jnp.dot(a, wu_ref[0], preferred_element_type=F32)
        o_ref[...] = (g * _sigmoid(g) * u).astype(o_ref.dtype)

    @pl.when(i >= n_valid)
    def _():
        o_ref[...] = jnp.zeros_like(o_ref)


def grouped_swiglu_up(x, src_tok, g, shift, scale, w_gu, tile_expert, n_valid, tn=512):
    k = x.shape[1]
    rows = src_tok.shape[0]
    f = w_gu.shape[2] // 2
    nj = f // tn
    tm = MOE_TILE
    row = pl.BlockSpec((1, k), lambda i, j, te, nv, src: (0, 0))
    vmem = 2 * tm * k * 4 + tm * k * 2 + 3 * tm * k * 4 + 4 * k * tn * 2 + 2 * tm * tn * 2 + 4 * tm * tn * 4 + (2 << 20)
    return pl.pallas_call(
        _gswiglu_kernel,
        grid_spec=pltpu.PrefetchScalarGridSpec(
            num_scalar_prefetch=3,
            grid=(rows // tm, nj),
            in_specs=[
                pl.BlockSpec(memory_space=pl.ANY), row, row, row,
                pl.BlockSpec((1, k, tn), lambda i, j, te, nv, src: (te[i], 0, j)),
                pl.BlockSpec((1, k, tn), lambda i, j, te, nv, src: (te[i], 0, j + nj)),
            ],
            out_specs=pl.BlockSpec((tm, tn), lambda i, j, te, nv, src: (i, j)),
            scratch_shapes=[pltpu.VMEM((2, tm, k), F32), pltpu.VMEM((tm, k), BF16), pltpu.SemaphoreType.DMA((2,))],
        ),
        out_shape=jax.ShapeDtypeStruct((rows, f), BF16),
        compiler_params=_params(("arbitrary", "arbitrary"), vmem),
        name="grouped_swiglu_up",
    )(tile_expert, n_valid, src_tok, x, g, shift, scale, w_gu, w_gu)


def _gdown_kernel(te_ref, nv_ref, a_ref, w_ref, o_ref):
    valid = pl.program_id(0) < nv_ref[0]

    @pl.when(valid)
    def _():
        o_ref[...] = jnp.dot(a_ref[...], w_ref[0], preferred_element_type=F32)

    @pl.when(jnp.logical_not(valid))
    def _():
        o_ref[...] = jnp.zeros_like(o_ref)


def grouped_down(act, w_down, tile_expert, n_valid, tn=1024):
    rows, f = act.shape
    n = w_down.shape[2]
    tm = MOE_TILE
    vmem = 2 * tm * f * 2 + 2 * f * tn * 2 + 2 * tm * tn * 4 + 2 * tm * tn * 4 + (2 << 20)
    return pl.pallas_call(
        _gdown_kernel,
        grid_spec=pltpu.PrefetchScalarGridSpec(
            num_scalar_prefetch=2,
            grid=(rows // tm, n // tn),
            in_specs=[
                pl.BlockSpec((tm, f), lambda i, j, te, nv: (i, 0)),
                pl.BlockSpec((1, f, tn), lambda i, j, te, nv: (te[i], 0, j)),
            ],
            out_specs=pl.BlockSpec((tm, tn), lambda i, j, te, nv: (i, j)),
        ),
        out_shape=jax.ShapeDtypeStruct((rows, n), F32),
        compiler_params=_params(("parallel", "arbitrary"), vmem),
        name="grouped_down",
    )(tile_expert, n_valid, act, w_down)


def _moe_combine_kernel(pos_ref, x_ref, w_ref, ys_hbm, g_ref, gt_ref, o_ref, buf_ref, sems):
    i = pl.program_id(0)
    tm = x_ref.shape[0]
    slot = i % 2

    def gather(step, to_slot, wait):
        for e in range(2):
            _gather_rows(pos_ref, 2 * step * tm + e, 2, ys_hbm, buf_ref.at[to_slot, e], sems.at[to_slot], 0, tm,
                         wait=wait)

    @pl.when(i == 0)
    def _():
        gather(i, slot, False)

    @pl.when(i + 1 < pl.num_programs(0))
    def _():
        gather(i + 1, 1 - slot, False)

    gather(i, slot, True)
    w = w_ref[...]
    y = w[:, 0:1] * buf_ref[slot, 0] + w[:, 1:2] * buf_ref[slot, 1]
    yn = y * lax.rsqrt(jnp.mean(y * y, axis=-1, keepdims=True) + NORM_EPS)
    o_ref[...] = x_ref[...] + gt_ref[...] * (yn * g_ref[...])


def moe_combine_residual(x, wts, ys, pos, g, gate, tm=128):
    t, d = x.shape
    row = pl.BlockSpec((1, d), lambda i, p: (0, 0))
    return pl.pallas_call(
        _moe_combine_kernel,
        grid_spec=pltpu.PrefetchScalarGridSpec(
            num_scalar_prefetch=1,
            grid=(t // tm,),
            in_specs=[pl.BlockSpec((tm, d), lambda i, p: (i, 0)), pl.BlockSpec((tm, LANES), lambda i, p: (i, 0)),
                      pl.BlockSpec(memory_space=pl.ANY), row, row],
            out_specs=pl.BlockSpec((tm, d), lambda i, p: (i, 0)),
            scratch_shapes=[pltpu.VMEM((2, 2, tm, d), F32), pltpu.SemaphoreType.DMA((2,))],
        ),
        out_shape=jax.ShapeDtypeStruct((t, d), F32),
        compiler_params=_params(("arbitrary",), 12 * tm * d * 4 + (2 << 20)),
        name="moe_combine_residual",
    )(pos, x, wts, ys, g, gate)


def _rope(x, cs, sn):
    lane = lax.broadcasted_iota(jnp.int32, x.shape, 1)
    first = (lane % (2 * ROPE_PAIRS)) < ROPE_PAIRS
    partner = jnp.where(first, pltpu.roll(x, HEAD_DIM - ROPE_PAIRS, 1), pltpu.roll(x, ROPE_PAIRS, 1))
    return x * cs + partner * sn


def _shift_rows(u, halo_prev, halo_next):
    n = u.shape[0]
    row = lax.broadcasted_iota(jnp.int32, u.shape, 0)
    prev = jnp.where(row == 0, halo_prev, pltpu.roll(u, 1, 0))
    nxt = jnp.where(row == n - 1, halo_next, pltpu.roll(u, n - 1, 0))
    return prev, nxt


def _mixer_kernel(sink_ref, p_ref, kvp_ref, kvn_ref, cp_ref, hp_ref, cn_ref, hn_ref, ckv_ref,
                  cs_ref, sn_ref, csp_ref, snp_ref, csn_ref, snn_ref, cw_ref, o_ref):
    i = pl.program_id(0)
    nb = pl.num_programs(0)
    cs, sn = cs_ref[...], sn_ref[...]

    last = BF16_SUBLANES - 1
    for c0 in range(0, CONV_WIDTH, LANES):
        cl = slice(c0, c0 + LANES)
        cc = slice(CONV_WIDTH + c0, CONV_WIDTH + c0 + LANES)
        ch = slice(2 * CONV_WIDTH + c0, 2 * CONV_WIDTH + c0 + LANES)
        cu = p_ref[:, cc].astype(F32) * p_ref[:, ch].astype(F32)
        halo_p = cp_ref[last:last + 1, cl].astype(F32) * hp_ref[last:last + 1, cl].astype(F32)
        halo_n = cn_ref[0:1, cl].astype(F32) * hn_ref[0:1, cl].astype(F32)
        halo_p = jnp.where(i > 0, halo_p, 0.0)
        halo_n = jnp.where(i < nb - 1, halo_n, 0.0)
        cu_prev, cu_next = _shift_rows(cu, halo_p, halo_n)
        y_conv = p_ref[:, cl].astype(F32) * (cu_prev * cw_ref[0:1, cl] + cu * cw_ref[1:2, cl] + cu_next * cw_ref[2:3, cl])
        o_ref[:, cl] = y_conv.astype(o_ref.dtype)

    n_ctx = ckv_ref.shape[0]
    n_keys = 3 * BLOCK + n_ctx
    qi = lax.broadcasted_iota(jnp.int32, (BLOCK, n_keys), 0)
    kj = lax.broadcasted_iota(jnp.int32, (BLOCK, n_keys), 1)
    key_pos = i * BLOCK + kj - BLOCK
    visible = (kj >= 3 * BLOCK) | ((jnp.abs(qi + BLOCK - kj) <= BLOCK) & (key_pos >= 0) & (key_pos < nb * BLOCK))
    nt = (((1,), (1,)), ((), ()))

    for kh in range(N_KV_HEADS):
        ko = KV_START + kh * HEAD_DIM
        vo = KV_START + KV_WIDTH + kh * HEAD_DIM
        kb = kh * HEAD_DIM
        vb = KV_WIDTH + kh * HEAD_DIM
        keys = jnp.concatenate([
            _rope(kvp_ref[:, kb:kb + HEAD_DIM].astype(F32), csp_ref[...], snp_ref[...]).astype(BF16),
            _rope(p_ref[:, ko:ko + HEAD_DIM].astype(F32), cs, sn).astype(BF16),
            _rope(kvn_ref[:, kb:kb + HEAD_DIM].astype(F32), csn_ref[...], snn_ref[...]).astype(BF16),
            ckv_ref[:, kb:kb + HEAD_DIM],
        ], axis=0)
        vals = jnp.concatenate([kvp_ref[:, vb:vb + HEAD_DIM], p_ref[:, vo:vo + HEAD_DIM],
                                kvn_ref[:, vb:vb + HEAD_DIM], ckv_ref[:, vb:vb + HEAD_DIM]], axis=0)
        for g in range(GQA_GROUP):
            h = kh * GQA_GROUP + g
            qo = Q_START + h * HEAD_DIM
            q = (_rope(p_ref[:, qo:qo + HEAD_DIM].astype(F32), cs, sn) * ATTN_SCALE).astype(BF16)
            sc = jnp.where(visible, lax.dot_general(q, keys, nt, preferred_element_type=F32), -jnp.inf)
            sink = sink_ref[h]
            m = jnp.maximum(jnp.max(sc, axis=1, keepdims=True), sink)
            e = jnp.exp(sc - m)
            denom = jnp.exp(sink - m) + jnp.sum(e, axis=1, keepdims=True)
            o = jnp.dot(e.astype(BF16), vals, preferred_element_type=F32) / denom
            oo = CONV_WIDTH + h * HEAD_DIM
            o_ref[:, oo:oo + HEAD_DIM] = o.astype(o_ref.dtype)


def even_mixer_core(p, ckv, cs, sn, conv_w, sink):
    s = p.shape[0]
    nb = s // BLOCK
    hb = BLOCK // BF16_SUBLANES
    kvc = KV_START // (2 * KV_WIDTH)
    prev = lambda i: jnp.maximum(i - 1, 0)
    nxt = lambda i: jnp.minimum(i + 1, nb - 1)
    tab = lambda f: pl.BlockSpec((BLOCK, HEAD_DIM), lambda i: (f(i), 0))
    same = lambda i: i
    in_specs = [
        pl.BlockSpec(memory_space=pltpu.SMEM),
        pl.BlockSpec((BLOCK, IN_WIDTH), lambda i: (i, 0)),
        pl.BlockSpec((BLOCK, 2 * KV_WIDTH), lambda i: (prev(i), kvc)),
        pl.BlockSpec((BLOCK, 2 * KV_WIDTH), lambda i: (nxt(i), kvc)),
        pl.BlockSpec((BF16_SUBLANES, CONV_WIDTH), lambda i: (jnp.maximum(i * hb - 1, 0), 1)),
        pl.BlockSpec((BF16_SUBLANES, CONV_WIDTH), lambda i: (jnp.maximum(i * hb - 1, 0), 2)),
        pl.BlockSpec((BF16_SUBLANES, CONV_WIDTH), lambda i: (jnp.minimum((i + 1) * hb, nb * hb - 1), 1)),
        pl.BlockSpec((BF16_SUBLANES, CONV_WIDTH), lambda i: (jnp.minimum((i + 1) * hb, nb * hb - 1), 2)),
        pl.BlockSpec(ckv.shape, lambda i: (0, 0)),
        tab(same), tab(same), tab(prev), tab(prev), tab(nxt), tab(nxt),
        pl.BlockSpec(conv_w.shape, lambda i: (0, 0)),
    ]
    return pl.pallas_call(
        _mixer_kernel,
        grid=(nb,),
        in_specs=in_specs,
        out_specs=pl.BlockSpec((BLOCK, CONV_WIDTH + Q_WIDTH), lambda i: (i, 0)),
        out_shape=jax.ShapeDtypeStruct((s, CONV_WIDTH + Q_WIDTH), BF16),
        compiler_params=_params(("parallel",), 32 << 20),
        name="even_mixer_core",
    )(sink, p, p, p, p, p, p, p, ckv, cs, sn, cs, sn, cs, sn, conv_w)


def rope_tables(s):
    t = np.arange(s)
    inv = ROPE_BASE ** (-np.arange(ROPE_PAIRS, dtype=np.float64) / ROPE_PAIRS)
    ang_r = (t // GRID_W)[:, None] * inv[None, :]
    ang_c = (t % GRID_W)[:, None] * inv[None, :]
    cs = np.concatenate([np.cos(ang_r)] * 2 + [np.cos(ang_c)] * 2, axis=1)
    sn = np.concatenate([-np.sin(ang_r), np.sin(ang_r), -np.sin(ang_c), np.sin(ang_c)], axis=1)
    return cs.astype(np.float32), sn.astype(np.float32)


def _hyena_in_kernel(*refs, side_chunks):
    a_ref, ap_ref, an_ref, w_ref, b_ref, cw_ref, cb_ref = refs[:7]
    o_ref, ax_ref = (refs[8], refs[10]) if side_chunks else (refs[7], refs[8])
    i, j = pl.program_id(0), pl.program_id(1)
    halo = ap_ref.shape[0]
    tm = a_ref.shape[0]
    rows = ax_ref.shape[0]

    @pl.when(j == 0)
    def _():
        ax_ref[0:halo, :] = ap_ref[...]
        ax_ref[halo:halo + tm, :] = a_ref[...]
        ax_ref[halo + tm:, :] = an_ref[...]

    for c0 in range(0, w_ref.shape[1], V7X_MXU_COLS):
        cl = slice(c0, c0 + V7X_MXU_COLS)
        u = jnp.dot(ax_ref[...], w_ref[:, cl], preferred_element_type=F32) + b_ref[:, cl]
        r = lax.broadcasted_iota(jnp.int32, u.shape, 0)
        prev = jnp.where((r == halo) & (i == 0), 0.0, pltpu.roll(u, 1, 0))
        nxt = jnp.where((r == halo + tm - 1) & (i == pl.num_programs(0) - 1), 0.0, pltpu.roll(u, rows - 1, 0))
        y = prev * cw_ref[0:1, cl] + u * cw_ref[1:2, cl] + nxt * cw_ref[2:3, cl] + cb_ref[:, cl]
        o_ref[:, cl] = y[halo:halo + tm].astype(o_ref.dtype)

    if side_chunks:
        _side_cast(refs[7], refs[9], i * pl.num_programs(1) + j, side_chunks)


def hyena_in_proj(a, w, bias, conv_w, conv_b, col0, width, out_dtype, side=None, side_chunks=0, tm=1024, tn=512):
    s, k = a.shape
    halo = BF16_SUBLANES
    nj, j0 = width // tn, col0 // tn
    hb = tm // halo
    col = lambda rows: pl.BlockSpec((rows, tn), lambda i, j: (0, j + j0))
    in_specs = [
        pl.BlockSpec((tm, k), lambda i, j: (i, 0)),
        pl.BlockSpec((halo, k), lambda i, j: (jnp.maximum(i * hb - 1, 0), 0)),
        pl.BlockSpec((halo, k), lambda i, j: (jnp.minimum((i + 1) * hb, s // halo - 1), 0)),
        pl.BlockSpec((k, tn), lambda i, j: (0, j + j0)),
        col(1), col(3), col(1),
    ]
    args = [a, a, a, w, bias, conv_w, conv_b]
    osz = jnp.dtype(out_dtype).itemsize
    vmem = 2 * tm * k * 2 + (tm + 2 * halo) * k * 2 + 2 * k * tn * 2 + 2 * tm * tn * osz + 8 * tm * tn * 4 + (2 << 20)
    out_specs = pl.BlockSpec((tm, tn), lambda i, j: (i, j))
    out_shape = jax.ShapeDtypeStruct((s, width), out_dtype)
    if side is not None:
        assert side_chunks <= (s // tm) * nj
        spec, shape, side_vmem = _side_cast_plumbing(side, side_chunks, lambda i, j: i * nj + j)
        in_specs.append(spec)
        args.append(side)
        out_specs, out_shape, vmem = [out_specs, spec], [out_shape, shape], vmem + side_vmem
    return pl.pallas_call(
        functools.partial(_hyena_in_kernel, side_chunks=side_chunks),
        grid=(s // tm, nj),
        in_specs=in_specs,
        out_specs=out_specs,
        out_shape=out_shape,
        scratch_shapes=[pltpu.VMEM((tm + 2 * halo, k), BF16)],
        compiler_params=_params(("arbitrary", "arbitrary"), vmem),
        name="hyena_in_proj",
    )(*args)


def _hid_kernel(z_ref, w1_ref, b1_ref, w2_ref, b2_ref, w3_ref, b3_ref, fr_ref, o_ref):
    hp = lax.Precision.HIGHEST
    fr = fr_ref[...]
    h = jnp.sin(fr * (jnp.dot(z_ref[...], w1_ref[...], preferred_element_type=F32, precision=hp) + b1_ref[...]))
    h = jnp.sin(fr * (jnp.dot(h, w2_ref[...], preferred_element_type=F32, precision=hp) + b2_ref[...]))
    o_ref[...] = jnp.sin(fr * (jnp.dot(h, w3_ref[...], preferred_element_type=F32, precision=hp) + b3_ref[...]))


def _pad2(a, r, c):
    return jnp.zeros((r, c), F32).at[:a.shape[0], :a.shape[1]].set(a.astype(F32))


def filter_hidden(length, w1, b1, freq, w2, b2, w3, b3, tl=2048):
    t = np.linspace(0.0, 1.0, length)[:, None]
    w = (2.0 * math.pi / length) * np.arange(length)[:, None]
    bands = np.linspace(1e-4, FILTER_BANDS - 1, FILTER_BANDS)[None]
    z = np.concatenate([t, np.cos(bands * w), -np.sin(bands * w)], axis=-1)
    zp = np.zeros((length, LANES), np.float32)
    zp[:, :FILTER_EMB] = z
    full = pl.BlockSpec((LANES, LANES), lambda i: (0, 0))
    row = pl.BlockSpec((1, LANES), lambda i: (0, 0))
    return pl.pallas_call(
        _hid_kernel,
        grid=(length // tl,),
        in_specs=[pl.BlockSpec((tl, LANES), lambda i: (i, 0)), full, row, full, row, full, row, row],
        out_specs=pl.BlockSpec((tl, LANES), lambda i: (i, 0)),
        out_shape=jax.ShapeDtypeStruct((length, LANES), F32),
        compiler_params=_params(("parallel",), 16 << 20),
        name="filter_hidden",
    )(zp, _pad2(w1, LANES, LANES), _pad2(b1[None], 1, LANES), _pad2(w2, LANES, LANES), _pad2(b2[None], 1, LANES),
      _pad2(w3, LANES, LANES), _pad2(b3[None], 1, LANES), _pad2(freq[None], 1, LANES))


FFT_N1 = 256
FFT_N2 = 128
FFT_K1 = FFT_N1 // 2
FFT_GROUP = BF16_SUBLANES
FFT_PITCH = 3 * F32_SUBLANES


@functools.lru_cache(maxsize=None)
def _phase_tables(length):
    n1h, n2 = FFT_K1, FFT_N2
    assert length == n1h * n2
    n = 2 * length
    ia = np.arange(n1h, dtype=np.int64)
    tha = ((ia[None, :] * (2 * ia[:, None] + 1)) % (2 * FFT_N1)) * (2.0 * np.pi / (2 * FFT_N1))
    fa = np.concatenate([np.cos(tha), -np.sin(tha)], axis=0)
    ca = np.concatenate([np.cos(tha).T, -np.sin(tha).T], axis=1) * (2.0 / n)
    k1 = ia[:, None, None]
    k2 = np.arange(n2, dtype=np.int64)[None, :, None]
    m2 = np.arange(n2, dtype=np.int64)[None, None, :]
    phb = ((m2 * (2 * (k1 + FFT_N1 * k2) + 1)) % (2 * n)) * (2.0 * np.pi / (2 * n))
    gr, gi = np.cos(phb), -np.sin(phb)
    gb = np.concatenate([np.concatenate([gr, -gi], axis=2), np.concatenate([gi, gr], axis=2)], axis=1)
    hr, hi = np.swapaxes(gr, 1, 2), -np.swapaxes(gi, 1, 2)
    gbi = np.concatenate([np.concatenate([hr, -hi], axis=2), np.concatenate([hi, hr], axis=2)], axis=1)
    return tuple(t.astype(np.float32).astype(BF16) for t in (fa, ca, gb, gbi))


def _regroup_rows(k):
    return pl.ds(pl.multiple_of(k * FFT_PITCH, F32_SUBLANES), FFT_GROUP)


def _fft_a_core(rows_of, fa_ref, o_ref, s_refs, ns):
    fa = fa_ref[...]
    for j in range(FFT_GROUP):
        res = jnp.dot(fa, rows_of(j).astype(BF16), preferred_element_type=F32)
        for p in range(2):
            for s in range(ns):
                s_refs[p * ns + s][pl.ds(j, FFT_K1, stride=FFT_PITCH), :] = (
                    res[p * FFT_K1:(p + 1) * FFT_K1, s * LANES:(s + 1) * LANES])

    def emit(k, carry):
        for p in range(2):
            for s in range(ns):
                o_ref[0, k, p, :, s * LANES:(s + 1) * LANES] = s_refs[p * ns + s][_regroup_rows(k), :].astype(o_ref.dtype)
        return carry

    lax.fori_loop(0, FFT_K1, emit, 0, unroll=8)


def _fft_a_kernel(*refs, ns):
    x_refs, fa_ref, o_ref = refs[:ns], refs[ns], refs[ns + 1]
    xs_refs = refs[ns + 2:2 * ns + 2]
    for s in range(ns):
        xs_refs[s][...] = x_refs[s][...].reshape(FFT_K1 * FFT_GROUP, LANES)
    rows_of = lambda j: jnp.concatenate([r[pl.ds(j, FFT_K1, stride=FFT_GROUP), :] for r in xs_refs], axis=1)
    _fft_a_core(rows_of, fa_ref, o_ref, refs[2 * ns + 2:], ns)


def _fft_a_filter_kernel(hid_ref, w_ref, delta_ref, fa_ref, o_ref, n_ref, *scratch, ns, length):
    fs_refs, s_refs = scratch[:ns], scratch[ns:]
    g = pl.program_id(2)

    @pl.when(g == 0)
    def _():
        n_ref[...] = jnp.zeros_like(n_ref)

    rows = FFT_K1 * FFT_GROUP
    hid = hid_ref[...].reshape(rows, LANES).astype(BF16)
    r = lax.broadcasted_iota(jnp.int32, (rows, ns * LANES), 0)
    pos = (r // FFT_GROUP) * FFT_N2 + g * FFT_GROUP + r % FFT_GROUP
    decay = jnp.exp(-(pos.astype(F32) / float(length - 1)) * delta_ref[...])
    f = jnp.dot(hid, w_ref[0].astype(BF16), preferred_element_type=F32) * decay
    n_ref[0, 0:1, :] += jnp.sum(jnp.abs(f), axis=0, keepdims=True)
    for s in range(ns):
        fs_refs[s][...] = f[:, s * LANES:(s + 1) * LANES]
    rows_of = lambda j: jnp.concatenate([fr[pl.ds(j, FFT_K1, stride=FFT_GROUP), :] for fr in fs_refs], axis=1)
    _fft_a_core(rows_of, fa_ref, o_ref, s_refs, ns)


def _fft_a_scratch(ns):
    return [pltpu.VMEM((FFT_K1 * FFT_PITCH, LANES), F32)] * (2 * ns)


def _fft_a_vmem(ns):
    blk = FFT_K1 * FFT_GROUP * LANES
    return 2 * ns * blk * 4 + 2 * 2 * ns * blk * 2 + ns * blk * 4 + 2 * ns * FFT_K1 * FFT_PITCH * LANES * 4 + (8 << 20)


def fft_filter_pass_a(hid, w_o, length, dc=256):
    q, _, d = w_o.shape
    ns = dc // LANES
    delta = jnp.abs(jnp.linspace(MIN_DECAY, MAX_DECAY, d, dtype=F32))[None]
    hid3 = hid.reshape(FFT_K1, FFT_N2, LANES)
    fa = _phase_tables(length)[0]
    return pl.pallas_call(
        functools.partial(_fft_a_filter_kernel, ns=ns, length=length),
        grid=(q, d // dc, FFT_N2 // FFT_GROUP),
        in_specs=[pl.BlockSpec((FFT_K1, FFT_GROUP, LANES), lambda w, ci, g: (0, g, 0)),
                  pl.BlockSpec((1, LANES, dc), lambda w, ci, g: (w, 0, ci)),
                  pl.BlockSpec((1, dc), lambda w, ci, g: (0, ci)),
                  pl.BlockSpec(fa.shape, lambda w, ci, g: (0, 0))],
        out_specs=[pl.BlockSpec((1, FFT_K1, 2, FFT_GROUP, dc), lambda w, ci, g: (w, 0, 0, g, ci)),
                   pl.BlockSpec((1, F32_SUBLANES, dc), lambda w, ci, g: (w, 0, ci))],
        out_shape=[jax.ShapeDtypeStruct((q, FFT_K1, 2, FFT_N2, d), BF16),
                   jax.ShapeDtypeStruct((q, F32_SUBLANES, d), F32)],
        scratch_shapes=[pltpu.VMEM((FFT_K1 * FFT_GROUP, LANES), F32)] * ns + _fft_a_scratch(ns),
        compiler_params=_params(("parallel", "parallel", "arbitrary"), _fft_a_vmem(ns)),
        name="fft_filter_pass_a",
    )(hid3, w_o, delta, fa)


def fft_pass_a(x, col0, d, dc=256):
    q, length, c = x.shape
    ns = dc // LANES
    x4 = x.reshape(q, FFT_K1, FFT_N2, c)
    slab0 = col0 // LANES
    in_specs = [pl.BlockSpec((None, FFT_K1, FFT_GROUP, LANES),
                             functools.partial(lambda w, ci, g, s: (w, 0, g, slab0 + ci * ns + s), s=s))
                for s in range(ns)]
    fa = _phase_tables(length)[0]
    in_specs.append(pl.BlockSpec(fa.shape, lambda w, ci, g: (0, 0)))
    return pl.pallas_call(
        functools.partial(_fft_a_kernel, ns=ns),
        grid=(q, d // dc, FFT_N2 // FFT_GROUP),
        in_specs=in_specs,
        out_specs=pl.BlockSpec((1, FFT_K1, 2, FFT_GROUP, dc), lambda w, ci, g: (w, 0, 0, g, ci)),
        out_shape=jax.ShapeDtypeStruct((q, FFT_K1, 2, FFT_N2, d), BF16),
        scratch_shapes=[pltpu.VMEM((FFT_K1 * FFT_GROUP, LANES), F32)] * ns + _fft_a_scratch(ns),
        compiler_params=_params(("parallel", "parallel", "parallel"), _fft_a_vmem(ns)),
        name="fft_pass_a",
    )(*([x4] * ns), fa)


def _fft_b_kernel(a_ref, af_ref, ab_ref, gb_ref, gbi_ref, o_ref):
    n2 = FFT_N2
    dc = a_ref.shape[2]
    for i in range(a_ref.shape[0]):
        rhs = jnp.concatenate([a_ref[i], af_ref[0, i], ab_ref[0, i]], axis=1)
        res = jnp.dot(gb_ref[i], rhs, preferred_element_type=F32)
        u, uf, ub = res[:, 0:dc], res[:, dc:2 * dc], res[:, 2 * dc:3 * dc]
        ur, ui = u[0:n2], u[n2:2 * n2]
        kr, ki = uf[0:n2] + ub[0:n2], uf[n2:2 * n2] - ub[n2:2 * n2]
        v = jnp.concatenate([ur * kr - ui * ki, ur * ki + ui * kr], axis=0).astype(BF16)
        o_ref[i] = jnp.dot(gbi_ref[i], v, preferred_element_type=F32).astype(o_ref.dtype)


def fft_pass_b(a, filt_a, order, gb, gbi, kb=8, dc=512):
    k1, r, d = a.shape
    vmem = 2 * 4 * kb * r * dc * 2 + 2 * 2 * kb * r * r * 2 + 10 * r * dc * 4 + (4 << 20)
    return pl.pallas_call(
        _fft_b_kernel,
        grid=(k1 // kb, d // dc),
        in_specs=[pl.BlockSpec((kb, r, dc), lambda k, c: (k, 0, c)),
                  pl.BlockSpec((1, kb, r, dc), lambda k, c: (2 * order, k, 0, c)),
                  pl.BlockSpec((1, kb, r, dc), lambda k, c: (2 * order + 1, k, 0, c)),
                  pl.BlockSpec((kb, r, r), lambda k, c: (k, 0, 0)),
                  pl.BlockSpec((kb, r, r), lambda k, c: (k, 0, 0))],
        out_specs=pl.BlockSpec((kb, r, dc), lambda k, c: (k, 0, c)),
        out_shape=jax.ShapeDtypeStruct((k1, r, d), BF16),
        compiler_params=_params(("parallel", "parallel"), vmem),
        name="fft_pass_b",
    )(a, filt_a, filt_a, gb, gbi)


def _fft_c_kernel(*refs, ns):
    b_ref, ca_ref, gate_refs, z_refs = refs[0], refs[1], refs[2:2 + ns], refs[2 + ns:2 + 2 * ns]
    n_ref, bias_ref, o_ref = refs[2 + 2 * ns:5 + 2 * ns]
    s_refs = refs[5 + 2 * ns:5 + 4 * ns]
    t_refs = refs[5 + 4 * ns:]
    ca = ca_ref[...]

    def spread(k, carry):
        for p in range(2):
            for s in range(ns):
                s_refs[p * ns + s][_regroup_rows(k), :] = b_ref[k, p, :, s * LANES:(s + 1) * LANES].astype(F32)
        return carry

    lax.fori_loop(0, FFT_K1, spread, 0, unroll=8)
    for j in range(FFT_GROUP):
        b = jnp.concatenate([
            jnp.concatenate([s_refs[p * ns + s][pl.ds(j, FFT_K1, stride=FFT_PITCH), :] for s in range(ns)], axis=1)
            for p in range(2)], axis=0)
        y = jnp.dot(ca, b.astype(BF16), preferred_element_type=F32)
        for s in range(ns):
            t_refs[s][pl.ds(j, FFT_K1, stride=FFT_PITCH), :] = y[:, s * LANES:(s + 1) * LANES]

    def emit(k, carry):
        for s in range(ns):
            sl = slice(s * LANES, (s + 1) * LANES)
            y = t_refs[s][_regroup_rows(k), :] / (n_ref[0:1, sl] + n_ref[1:2, sl] + 1e-6)
            o_ref[k, :, sl] = (gate_refs[s][k].astype(F32) * (y + bias_ref[:, sl] * z_refs[s][k])).astype(o_ref.dtype)
        return carry

    lax.fori_loop(0, FFT_K1, emit, 0, unroll=8)


def fft_pass_c(bp, ca, gate_arr, gate_col0, z_arr, z_col0, nsum, bias, out_dtype, dc=256):
    k1, r, d = bp.shape
    ns = dc // LANES
    length = FFT_K1 * FFT_N2
    b4 = bp.reshape(k1, 2, FFT_N2, d)
    g3 = gate_arr.reshape(FFT_K1, FFT_N2, gate_arr.shape[1])
    z3 = z_arr.reshape(FFT_K1, FFT_N2, z_arr.shape[1])
    sig = lambda slab0: [pl.BlockSpec((FFT_K1, FFT_GROUP, LANES),
                                      functools.partial(lambda ci, g, s: (0, g, slab0 + ci * ns + s), s=s))
                         for s in range(ns)]
    row = pl.BlockSpec((1, dc), lambda ci, g: (0, ci))
    blk = FFT_K1 * FFT_GROUP * dc
    vmem = 2 * 2 * blk * 2 + 2 * 3 * blk * 4 + 3 * ns * FFT_K1 * FFT_PITCH * LANES * 4 + (8 << 20)
    out = pl.pallas_call(
        functools.partial(_fft_c_kernel, ns=ns),
        grid=(d // dc, FFT_N2 // FFT_GROUP),
        in_specs=[pl.BlockSpec((k1, 2, FFT_GROUP, dc), lambda ci, g: (0, 0, g, ci)),
                  pl.BlockSpec(ca.shape, lambda ci, g: (0, 0)),
                  *sig(gate_col0 // LANES), *sig(z_col0 // LANES),
                  pl.BlockSpec((2, dc), lambda ci, g: (0, ci)), row],
        out_specs=pl.BlockSpec((FFT_K1, FFT_GROUP, dc), lambda ci, g: (0, g, ci)),
        out_shape=jax.ShapeDtypeStruct((FFT_K1, FFT_N2, d), out_dtype),
        scratch_shapes=[pltpu.VMEM((FFT_K1 * FFT_PITCH, LANES), F32)] * (3 * ns),
        compiler_params=_params(("parallel", "parallel"), vmem),
        name="fft_pass_c",
    )(b4, ca, *([g3] * ns), *([z3] * ns), nsum, bias)
    return out.reshape(length, d)


def kernel(x, c, ctx, c_ctx, ada_w, ada_b, norm_g, mix_w_in, mix_conv_w, mix_sink, mix_w_out, ffn_w_gu, ffn_w_down, hy_w_in, hy_b_in, hy_conv_w, hy_conv_b, hf_w1, hf_b1, hf_freq, hf_w2, hf_b2, hf_w3, hf_b3, hf_w_out, hf_bias, hy_w_out, hy_b_out, router_w, router_b, moe_w_gu, moe_w_down):
    assert x.shape[0] == 1 and ada_w.shape[0] == 2
    s, d = x.shape[1], x.shape[2]
    xs = x[0]
    ctxs = ctx[0]

    acts = jnp.zeros((F32_SUBLANES, d), F32).at[0].set(c[0]).at[1].set(c_ctx)
    mods = adaln_all(acts, ada_w, ada_b)
    row = lambda layer, r, k: mods[layer, r:r + 1, k * d:(k + 1) * d]

    g = norm_g[0]
    h = norm_mod(xs, g[0:1], row(0, 0, 0), row(0, 0, 1))
    hc = norm_mod(ctxs, g[0:1], row(0, 1, 0), row(0, 1, 1))
    w_in = mix_w_in[0].astype(BF16)
    p, ffn_gu = matmul(h, w_in, out_dtype=BF16, side=ffn_w_gu[0], side_chunks=256)
    ckv = matmul(hc, w_in[:, KV_START:], out_dtype=BF16)
    cs, sn = rope_tables(s)
    y = even_mixer_core(p, ckv, cs, sn, mix_conv_w[0], mix_sink[0])
    out = matmul(y, mix_w_out[0].astype(BF16), out_dtype=BF16)
    xs, h = post_norm(xs, out, g[1:2], row(0, 0, 2), g[2:3], row(0, 0, 3), row(0, 0, 4))

    act, ffn_down = swiglu_up(h, ffn_gu, tm=2048, tn=256, side=ffn_w_down[0], side_chunks=344)
    out, hy_in = matmul(act, ffn_down, out_dtype=BF16, tm=1024, tn=256, a_buffers=1, side=hy_w_in[0], side_chunks=256)
    g1 = norm_g[1]
    xs, h = post_norm(xs, out, g[3:4], row(0, 0, 5), g1[0:1], row(1, 0, 0), row(1, 0, 1))

    g = g1
    moe_gu_f32 = moe_w_gu[0].reshape(-1, moe_w_gu.shape[-1])
    hy_args = (h, hy_in, hy_b_in[0][None], hy_conv_w[0], hy_conv_b[0][None])
    x12, moe_gu = hyena_in_proj(*hy_args, 0, 2 * d, BF16, side=moe_gu_f32, side_chunks=256)
    v = hyena_in_proj(*hy_args, 2 * d, d, F32)
    hid = filter_hidden(s, hf_w1[0], hf_b1[0], hf_freq[0], hf_w2[0], hf_b2[0], hf_w3[0], hf_b3[0])
    w_o = hf_w_out[0].astype(F32).reshape(FILTER_HIDDEN, HYENA_ORDER * 2, d).transpose(1, 0, 2)
    w_o = jnp.zeros((HYENA_ORDER * 2, LANES, d), F32).at[:, :FILTER_HIDDEN].set(w_o)
    filt_a, fsum = fft_filter_pass_a(hid, w_o, s)
    fsum = fsum[:, 0, :]
    _, ca, gb, gbi = _phase_tables(s)
    spec = lambda a: a.reshape(a.shape[0], FFT_K1, 2 * FFT_N2, d)
    filt_a = spec(filt_a)
    a = spec(fft_pass_a(v[None], 0, d))[0]
    z1 = fft_pass_c(fft_pass_b(a, filt_a, 0, gb, gbi), ca, x12, 0, v, 0, fsum[0:2], hf_bias[0, 0][None], F32)
    a = spec(fft_pass_a(z1[None], 0, d))[0]
    z2 = fft_pass_c(fft_pass_b(a, filt_a, 1, gb, gbi), ca, x12, d, z1, 0, fsum[2:4], hf_bias[0, 1][None], BF16)
    moe_down_f32 = moe_w_down[0].reshape(-1, moe_w_down.shape[-1])
    out, moe_down = matmul(z2, hy_w_out[0].astype(BF16), bias=hy_b_out[0][None], out_dtype=BF16,
                           side=moe_down_f32, side_chunks=128)
    xs, sel, wts = post_router_top2(xs, out, g[1:2], row(1, 0, 2), g[2:3], row(1, 0, 3), row(1, 0, 4),
                                    router_w[0], router_b[0])
    pos, src_tok, tile_expert, n_valid = moe_plan(sel)
    act = grouped_swiglu_up(xs, src_tok, g[2:3], row(1, 0, 3), row(1, 0, 4), moe_gu.reshape(moe_w_gu.shape[1:]),
                            tile_expert, n_valid)
    ys = grouped_down(act, moe_down.reshape(moe_w_down.shape[1:]), tile_expert, n_valid)
    xs = moe_combine_residual(xs, wts, ys, pos, g[3:4], row(1, 0, 5))
    return xs[None]
```

```python
import functools
import math

import jax
import jax.numpy as jnp
import numpy as np
from jax import lax
from jax.experimental import pallas as pl
from jax.experimental.pallas import tpu as pltpu

F32 = jnp.float32
BF16 = jnp.bfloat16

D_MODEL = 4096
GRID_W = 64
HEAD_DIM = 128
CONV_WIDTH = D_MODEL // 2
N_HEADS = (D_MODEL // 2) // HEAD_DIM
N_KV_HEADS = N_HEADS // 4
GQA_GROUP = N_HEADS // N_KV_HEADS
Q_WIDTH = N_HEADS * HEAD_DIM
KV_WIDTH = N_KV_HEADS * HEAD_DIM
Q_START = 3 * CONV_WIDTH
KV_START = Q_START + Q_WIDTH
IN_WIDTH = KV_START + 2 * KV_WIDTH
BLOCK = 128
ATTN_SCALE = HEAD_DIM ** -0.5
ROPE_BASE = 10000.0
ROPE_PAIRS = HEAD_DIM // 4
HYENA_ORDER = 2
FILTER_EMB = 33
FILTER_BANDS = (FILTER_EMB - 1) // 2
FILTER_HIDDEN = 64
MIN_DECAY = math.log(1e-2) / 0.3
MAX_DECAY = math.log(1e-2) / 1.5
N_EXPERTS = 8
NORM_EPS = 1e-6

V7X_VMEM_BYTES = 64 * 1024 * 1024
V7X_MXU_COLS = 256
LANES = 128
F32_SUBLANES = 8
BF16_SUBLANES = 16


def _params(semantics, vmem_bytes):
    limit = min(int(vmem_bytes), V7X_VMEM_BYTES - 4 * 1024 * 1024)
    return pltpu.CompilerParams(dimension_semantics=semantics, vmem_limit_bytes=limit)


def _sigmoid(v):
    return 1.0 / (1.0 + jnp.exp(-v))


def _adaln_kernel(a_ref, w_ref, b_ref, o_ref, acc_ref):
    k = pl.program_id(2)

    @pl.when(k == 0)
    def _():
        acc_ref[...] = jnp.zeros_like(acc_ref)

    a = a_ref[...]
    a = a * _sigmoid(a)
    acc_ref[...] += jnp.dot(a.astype(BF16), w_ref[0].astype(BF16), preferred_element_type=F32)

    @pl.when(k == pl.num_programs(2) - 1)
    def _():
        o_ref[0] = acc_ref[...] + b_ref[0]


def adaln_all(acts, ada_w, ada_b):
    depth, d, n = ada_w.shape
    tn, tk = 2048, 1024
    return pl.pallas_call(
        _adaln_kernel,
        grid=(depth, n // tn, d // tk),
        in_specs=[
            pl.BlockSpec((F32_SUBLANES, tk), lambda l, j, k: (0, k)),
            pl.BlockSpec((1, tk, tn), lambda l, j, k: (l, k, j)),
            pl.BlockSpec((1, 1, tn), lambda l, j, k: (l, 0, j)),
        ],
        out_specs=pl.BlockSpec((1, F32_SUBLANES, tn), lambda l, j, k: (l, 0, j)),
        out_shape=jax.ShapeDtypeStruct((depth, F32_SUBLANES, n), F32),
        scratch_shapes=[pltpu.VMEM((F32_SUBLANES, tn), F32)],
        compiler_params=_params(("parallel", "parallel", "arbitrary"), 2 * tk * tn * 4 + tk * tn * 2 + (4 << 20)),
        name="adaln",
    )(acts, ada_w, ada_b.reshape(depth, 1, n))


def _norm_mod(x, g, sh, sc):
    y = x * lax.rsqrt(jnp.mean(x * x, axis=-1, keepdims=True) + NORM_EPS)
    return (y * g) * (1.0 + sc) + sh


def _norm_mod_kernel(x_ref, g_ref, sh_ref, sc_ref, o_ref):
    o_ref[...] = _norm_mod(x_ref[...], g_ref[...], sh_ref[...], sc_ref[...]).astype(o_ref.dtype)


def norm_mod(x, g, shift, scale, tm=512):
    m, d = x.shape
    tm = min(tm, m)
    row = pl.BlockSpec((1, d), lambda i: (0, 0))
    return pl.pallas_call(
        _norm_mod_kernel,
        grid=(m // tm,),
        in_specs=[pl.BlockSpec((tm, d), lambda i: (i, 0)), row, row, row],
        out_specs=pl.BlockSpec((tm, d), lambda i: (i, 0)),
        out_shape=jax.ShapeDtypeStruct((m, d), BF16),
        compiler_params=_params(("parallel",), 2 * tm * d * 4 + 2 * tm * d * 2 + 3 * tm * d * 4 + (2 << 20)),
        name="norm_mod",
    )(x, g, shift, scale)


def _post_norm_kernel(x_ref, y_ref, g_ref, gt_ref, g2_ref, sh_ref, sc_ref, o_ref, h_ref):
    y = y_ref[...].astype(F32)
    yn = y * lax.rsqrt(jnp.mean(y * y, axis=-1, keepdims=True) + NORM_EPS)
    x = x_ref[...] + gt_ref[...] * (yn * g_ref[...])
    o_ref[...] = x
    h_ref[...] = _norm_mod(x, g2_ref[...], sh_ref[...], sc_ref[...]).astype(h_ref.dtype)


def post_norm(x, y, g, gate, g_next, shift, scale, tm=256):
    m, d = x.shape
    row = pl.BlockSpec((1, d), lambda i: (0, 0))
    blk = pl.BlockSpec((tm, d), lambda i: (i, 0))
    return pl.pallas_call(
        _post_norm_kernel,
        grid=(m // tm,),
        in_specs=[blk, blk, row, row, row, row, row],
        out_specs=[blk, blk],
        out_shape=[jax.ShapeDtypeStruct((m, d), F32), jax.ShapeDtypeStruct((m, d), BF16)],
        compiler_params=_params(("parallel",), 7 * tm * d * 4 + 5 * tm * d * 4 + (2 << 20)),
        name="post_norm",
    )(x, y, g, gate, g_next, shift, scale)


def _side_cast(src_ref, dst_ref, step, n_chunks):
    @pl.when(step < n_chunks)
    def _():
        dst_ref[...] = src_ref[...].astype(dst_ref.dtype)


def _side_cast_plumbing(side, n_chunks, step_of):
    rows, cols = side.shape
    spec = pl.BlockSpec((rows // n_chunks, cols), lambda *g: (jnp.minimum(step_of(*g), n_chunks - 1), 0))
    vmem = 2 * (rows // n_chunks) * cols * (4 + 2)
    return spec, jax.ShapeDtypeStruct((rows, cols), BF16), vmem


def _mm_kernel(*refs, has_bias, side_chunks):
    a_ref, w_ref = refs[0], refs[1]
    n_in = 2 + has_bias + (side_chunks > 0)
    o_ref = refs[n_in]
    acc = jnp.dot(a_ref[...], w_ref[...], preferred_element_type=F32)
    if has_bias:
        acc = acc + refs[2][...]
    o_ref[...] = acc.astype(o_ref.dtype)
    if side_chunks:
        step = pl.program_id(0) * pl.num_programs(1) + pl.program_id(1)
        _side_cast(refs[n_in - 1], refs[n_in + 1], step, side_chunks)


def matmul(a, w, bias=None, out_dtype=F32, tm=1024, tn=512, side=None, side_chunks=0, a_buffers=2):
    m, k = a.shape
    n = w.shape[1]
    tm, tn = min(tm, m), min(tn, n)
    nj = n // tn
    a_mode = {} if a_buffers == 2 else {"pipeline_mode": pl.Buffered(a_buffers)}
    in_specs = [pl.BlockSpec((tm, k), lambda i, j: (i, 0), **a_mode), pl.BlockSpec((k, tn), lambda i, j: (0, j))]
    args = [a, w]
    if bias is not None:
        in_specs.append(pl.BlockSpec((1, tn), lambda i, j: (0, j)))
        args.append(bias)
    osz = jnp.dtype(out_dtype).itemsize
    vmem = a_buffers * tm * k * 2 + 2 * k * tn * 2 + 2 * tm * tn * osz + 2 * tm * tn * 4 + (2 << 20)
    out_specs = pl.BlockSpec((tm, tn), lambda i, j: (i, j))
    out_shape = jax.ShapeDtypeStruct((m, n), out_dtype)
    if side is not None:
        assert side_chunks <= (m // tm) * nj
        spec, shape, side_vmem = _side_cast_plumbing(side, side_chunks, lambda i, j: i * nj + j)
        in_specs.append(spec)
        args.append(side)
        out_specs, out_shape, vmem = [out_specs, spec], [out_shape, shape], vmem + side_vmem
    return pl.pallas_call(
        functools.partial(_mm_kernel, has_bias=bias is not None, side_chunks=side_chunks),
        grid=(m // tm, nj),
        in_specs=in_specs,
        out_specs=out_specs,
        out_shape=out_shape,
        compiler_params=_params(("arbitrary", "arbitrary") if side is not None else ("parallel", "arbitrary"), vmem),
        name="matmul",
    )(*args)


def _swiglu_up_kernel(a_ref, wg_ref, wu_ref, side_ref, o_ref, side_o_ref, *, side_chunks):
    a = a_ref[...]
    g = jnp.dot(a, wg_ref[...], preferred_element_type=F32)
    u = jnp.dot(a, wu_ref[...], preferred_element_type=F32)
    o_ref[...] = (g * _sigmoid(g) * u).astype(o_ref.dtype)
    _side_cast(side_ref, side_o_ref, pl.program_id(0) * pl.num_programs(1) + pl.program_id(1), side_chunks)


def swiglu_up(a, w_gu, tm, tn, side, side_chunks):
    m, k = a.shape
    f = w_gu.shape[1] // 2
    nj = f // tn
    assert side_chunks <= (m // tm) * nj
    spec, shape, side_vmem = _side_cast_plumbing(side, side_chunks, lambda i, j: i * nj + j)
    vmem = 2 * tm * k * 2 + 4 * k * tn * 2 + 2 * tm * tn * 2 + 4 * tm * tn * 4 + side_vmem + (2 << 20)
    return pl.pallas_call(
        functools.partial(_swiglu_up_kernel, side_chunks=side_chunks),
        grid=(m // tm, nj),
        in_specs=[
            pl.BlockSpec((tm, k), lambda i, j: (i, 0)),
            pl.BlockSpec((k, tn), lambda i, j: (0, j)),
            pl.BlockSpec((k, tn), lambda i, j: (0, j + nj)),
            spec,
        ],
        out_specs=[pl.BlockSpec((tm, tn), lambda i, j: (i, j)), spec],
        out_shape=[jax.ShapeDtypeStruct((m, f), BF16), shape],
        compiler_params=_params(("arbitrary", "arbitrary"), vmem),
        name="swiglu_up",
    )(a, w_gu, w_gu, side)


def _post_router_kernel(x_ref, y_ref, gp_ref, gt_ref, g_ref, sh_ref, sc_ref, w_ref, b_ref, o_ref, sel_ref, wts_ref):
    y = y_ref[...].astype(F32)
    yn = y * lax.rsqrt(jnp.mean(y * y, axis=-1, keepdims=True) + NORM_EPS)
    x = x_ref[...] + gt_ref[...] * (yn * gp_ref[...])
    o_ref[...] = x
    h = _norm_mod(x, g_ref[...], sh_ref[...], sc_ref[...])
    logits = jnp.dot(h, w_ref[...], preferred_element_type=F32, precision=lax.Precision.HIGHEST) + b_ref[...]
    lane = lax.broadcasted_iota(jnp.int32, logits.shape, 1)
    neg = jnp.float32(-jnp.inf)
    logits = jnp.where(lane < N_EXPERTS, logits, neg)
    v1 = jnp.max(logits, axis=1, keepdims=True)
    i1 = jnp.min(jnp.where(logits == v1, lane, LANES), axis=1, keepdims=True)
    rest = jnp.where(lane == i1, neg, logits)
    v2 = jnp.max(rest, axis=1, keepdims=True)
    i2 = jnp.min(jnp.where(rest == v2, lane, LANES), axis=1, keepdims=True)
    e2 = jnp.exp(v2 - v1)
    w1 = 1.0 / (1.0 + e2)
    w2 = e2 / (1.0 + e2)
    sel_ref[...] = jnp.where(lane == 0, i1, jnp.where(lane == 1, i2, 0))
    wts_ref[...] = jnp.where(lane == 0, w1, jnp.where(lane == 1, w2, 0.0))


def post_router_top2(x, y, g_post, gate, g, shift, scale, router_w, router_b, tm=256):
    m, d = x.shape
    wp = jnp.zeros((d, LANES), F32).at[:, :N_EXPERTS].set(router_w)
    bp = jnp.zeros((1, LANES), F32).at[0, :N_EXPERTS].set(router_b)
    row = pl.BlockSpec((1, d), lambda i: (0, 0))
    blk = pl.BlockSpec((tm, d), lambda i: (i, 0))
    out = pl.BlockSpec((tm, LANES), lambda i: (i, 0))
    return pl.pallas_call(
        _post_router_kernel,
        grid=(m // tm,),
        in_specs=[blk, blk, row, row, row, row, row,
                  pl.BlockSpec((d, LANES), lambda i: (0, 0)), pl.BlockSpec((1, LANES), lambda i: (0, 0))],
        out_specs=[blk, out, out],
        out_shape=[jax.ShapeDtypeStruct((m, d), F32), jax.ShapeDtypeStruct((m, LANES), jnp.int32),
                   jax.ShapeDtypeStruct((m, LANES), F32)],
        compiler_params=_params(("parallel",), 6 * tm * d * 4 + 6 * tm * d * 4 + 2 * d * LANES * 4 + (4 << 20)),
        name="post_router",
    )(x, y, g_post, gate, g, shift, scale, wp, bp)


MOE_TILE = 512


def moe_plan(sel):
    t = sel.shape[0]
    npairs = 2 * t
    rows = npairs + N_EXPERTS * MOE_TILE
    e_flat = sel[:, :2].reshape(npairs)
    order = jnp.argsort(e_flat, stable=True).astype(jnp.int32)
    rank = jnp.argsort(order).astype(jnp.int32)
    counts = jnp.sum((e_flat[:, None] == jnp.arange(N_EXPERTS, dtype=jnp.int32)[None, :]).astype(jnp.int32), axis=0)
    padded = (counts + MOE_TILE - 1) // MOE_TILE * MOE_TILE
    ends_p = jnp.cumsum(padded)
    starts_p = ends_p - padded
    starts_u = jnp.cumsum(counts) - counts
    pos = starts_p[e_flat] + rank - starts_u[e_flat]
    tile_start = jnp.arange(rows // MOE_TILE, dtype=jnp.int32) * MOE_TILE
    tile_expert = jnp.minimum(jnp.searchsorted(ends_p, tile_start, side="right"), N_EXPERTS - 1).astype(jnp.int32)
    n_valid = (ends_p[-1] // MOE_TILE).astype(jnp.int32).reshape(1)
    r = jnp.arange(rows, dtype=jnp.int32)
    e_row = jnp.repeat(tile_expert, MOE_TILE)
    in_group = r - starts_p[e_row]
    valid = (in_group < counts[e_row]) & (r < ends_p[-1])
    src_tok = jnp.where(valid, order[jnp.clip(starts_u[e_row] + in_group, 0, npairs - 1)] // 2, 0)
    return pos.astype(jnp.int32), src_tok.astype(jnp.int32), tile_expert, n_valid


def _gather_rows(idx_ref, base, stride, src_hbm, dst_ref, sem, r0, n, wait):
    def body(r, carry):
        copy = pltpu.make_async_copy(src_hbm.at[pl.ds(idx_ref[base + stride * r], 1)], dst_ref.at[pl.ds(r, 1)], sem)
        if wait:
            copy.wait()
        else:
            copy.start()
        return carry

    lax.fori_loop(r0, r0 + n, body, 0, unroll=8)


def _gswiglu_kernel(te_ref, nv_ref, src_ref, x_hbm, g_ref, sh_ref, sc_ref, wg_ref, wu_ref, o_ref,
                    xbuf_ref, a_ref, sems):
    i, j = pl.program_id(0), pl.program_id(1)
    tm = a_ref.shape[0]
    n_valid = nv_ref[0]
    slot = i % 2

    def gather(tile, to_slot, wait):
        @pl.when(tile < n_valid)
        def _():
            _gather_rows(src_ref, tile * tm, 1, x_hbm, xbuf_ref.at[to_slot], sems.at[to_slot], 0, tm, wait=wait)

    @pl.when(j == 0)
    def _():
        @pl.when(i == 0)
        def _():
            gather(i, slot, False)

        gather(i, slot, True)
        gather(i + 1, 1 - slot, False)

        @pl.when(i < n_valid)
        def _():
            gain = g_ref[...] * (1.0 + sc_ref[...])
            rc = 2 * BF16_SUBLANES

            def norm_rows(c, carry):
                rows = pl.ds(pl.multiple_of(c * rc, rc), rc)
                x = xbuf_ref[slot, rows, :]
                y = x * lax.rsqrt(jnp.mean(x * x, axis=-1, keepdims=True) + NORM_EPS)
                a_ref[rows, :] = (y * gain + sh_ref[...]).astype(a_ref.dtype)
                return carry

            lax.fori_loop(0, tm // rc, norm_rows, 0, unroll=2)

    @pl.when(i < n_valid)
    def _():
        a = a_ref[...]
        g = jnp.dot(a, wg_ref[0], preferred_element_type=F32)
        u = jnp.dot(a, wu_ref[0], preferred_element_type=F32)
        o_ref[...] = (g * _sigmoid(g) * u).astype(o_ref.dtype)

    @pl.when(i >= n_valid)
    def _():
        o_ref[...] = jnp.zeros_like(o_ref)


def grouped_swiglu_up(x, src_tok, g, shift, scale, w_gu, tile_expert, n_valid, tn=512):
    k = x.shape[1]
    rows = src_tok.shape[0]
    f = w_gu.shape[2] // 2
    nj = f // tn
    tm = MOE_TILE
    row = pl.BlockSpec((1, k), lambda i, j, te, nv, src: (0, 0))
    vmem = 2 * tm * k * 4 + tm * k * 2 + 3 * tm * k * 4 + 4 * k * tn * 2 + 2 * tm * tn * 2 + 4 * tm * tn * 4 + (2 << 20)
    return pl.pallas_call(
        _gswiglu_kernel,
        grid_spec=pltpu.PrefetchScalarGridSpec(
            num_scalar_prefetch=3,
            grid=(rows // tm, nj),
            in_specs=[
                pl.BlockSpec(memory_space=pl.ANY), row, row, row,
                pl.BlockSpec((1, k, tn), lambda i, j, te, nv, src: (te[i], 0, j)),
                pl.BlockSpec((1, k, tn), lambda i, j, te, nv, src: (te[i], 0, j + nj)),
            ],
            out_specs=pl.BlockSpec((tm, tn), lambda i, j, te, nv, src: (i, j)),
            scratch_shapes=[pltpu.VMEM((2, tm, k), F32), pltpu.VMEM((tm, k), BF16), pltpu.SemaphoreType.DMA((2,))],
        ),
        out_shape=jax.ShapeDtypeStruct((rows, f), BF16),
        compiler_params=_params(("arbitrary", "arbitrary"), vmem),
        name="grouped_swiglu_up",
    )(tile_expert, n_valid, src_tok, x, g, shift, scale, w_gu, w_gu)


def _gdown_kernel(te_ref, nv_ref, a_ref, w_ref, o_ref):
    valid = pl.program_id(0) < nv_ref[0]

    @pl.when(valid)
    def _():
        o_ref[...] = jnp.dot(a_ref[...], w_ref[0], preferred_element_type=F32)

    @pl.when(jnp.logical_not(valid))
    def _():
        o_ref[...] = jnp.zeros_like(o_ref)


def grouped_down(act, w_down, tile_expert, n_valid, tn=1024):
    rows, f = act.shape
    n = w_down.shape[2]
    tm = MOE_TILE
    vmem = 2 * tm * f * 2 + 2 * f * tn * 2 + 2 * tm * tn * 4 + 2 * tm * tn * 4 + (2 << 20)
    return pl.pallas_call(
        _gdown_kernel,
        grid_spec=pltpu.PrefetchScalarGridSpec(
            num_scalar_prefetch=2,
            grid=(rows // tm, n // tn),
            in_specs=[
                pl.BlockSpec((tm, f), lambda i, j, te, nv: (i, 0)),
                pl.BlockSpec((1, f, tn), lambda i, j, te, nv: (te[i], 0, j)),
            ],
            out_specs=pl.BlockSpec((tm, tn), lambda i, j, te, nv: (i, j)),
        ),
        out_shape=jax.ShapeDtypeStruct((rows, n), F32),
        compiler_params=_params(("parallel", "arbitrary"), vmem),
        name="grouped_down",
    )(tile_expert, n_valid, act, w_down)


def _moe_combine_kernel(pos_ref, x_ref, w_ref, ys_hbm, g_ref, gt_ref, o_ref, buf_ref, sems):
    i = pl.program_id(0)
    tm = x_ref.shape[0]
    slot = i % 2

    def gather(step, to_slot, wait):
        for e in range(2):
            _gather_rows(pos_ref, 2 * step * tm + e, 2, ys_hbm, buf_ref.at[to_slot, e], sems.at[to_slot], 0, tm,
                         wait=wait)

    @pl.when(i == 0)
    def _():
        gather(i, slot, False)

    @pl.when(i + 1 < pl.num_programs(0))
    def _():
        gather(i + 1, 1 - slot, False)

    gather(i, slot, True)
    w = w_ref[...]
    y = w[:, 0:1] * buf_ref[slot, 0] + w[:, 1:2] * buf_ref[slot, 1]
    yn = y * lax.rsqrt(jnp.mean(y * y, axis=-1, keepdims=True) + NORM_EPS)
    o_ref[...] = x_ref[...] + gt_ref[...] * (yn * g_ref[...])


def moe_combine_residual(x, wts, ys, pos, g, gate, tm=128):
    t, d = x.shape
    row = pl.BlockSpec((1, d), lambda i, p: (0, 0))
    return pl.pallas_call(
        _moe_combine_kernel,
        grid_spec=pltpu.PrefetchScalarGridSpec(
            num_scalar_prefetch=1,
            grid=(t // tm,),
            in_specs=[pl.BlockSpec((tm, d), lambda i, p: (i, 0)), pl.BlockSpec((tm, LANES), lambda i, p: (i, 0)),
                      pl.BlockSpec(memory_space=pl.ANY), row, row],
            out_specs=pl.BlockSpec((tm, d), lambda i, p: (i, 0)),
            scratch_shapes=[pltpu.VMEM((2, 2, tm, d), F32), pltpu.SemaphoreType.DMA((2,))],
        ),
        out_shape=jax.ShapeDtypeStruct((t, d), F32),
        compiler_params=_params(("arbitrary",), 12 * tm * d * 4 + (2 << 20)),
        name="moe_combine_residual",
    )(pos, x, wts, ys, g, gate)


def _rope(x, cs, sn):
    lane = lax.broadcasted_iota(jnp.int32, x.shape, 1)
    first = (lane % (2 * ROPE_PAIRS)) < ROPE_PAIRS
    partner = jnp.where(first, pltpu.roll(x, HEAD_DIM - ROPE_PAIRS, 1), pltpu.roll(x, ROPE_PAIRS, 1))
    return x * cs + partner * sn


def _shift_rows(u, halo_prev, halo_next):
    n = u.shape[0]
    row = lax.broadcasted_iota(jnp.int32, u.shape, 0)
    prev = jnp.where(row == 0, halo_prev, pltpu.roll(u, 1, 0))
    nxt = jnp.where(row == n - 1, halo_next, pltpu.roll(u, n - 1, 0))
    return prev, nxt


def _mixer_kernel(sink_ref, p_ref, kvp_ref, kvn_ref, cp_ref, hp_ref, cn_ref, hn_ref, ckv_ref,
                  cs_ref, sn_ref, csp_ref, snp_ref, csn_ref, snn_ref, cw_ref, o_ref):
    i = pl.program_id(0)
    nb = pl.num_programs(0)
    cs, sn = cs_ref[...], sn_ref[...]

    last = BF16_SUBLANES - 1
    for c0 in range(0, CONV_WIDTH, LANES):
        cl = slice(c0, c0 + LANES)
        cc = slice(CONV_WIDTH + c0, CONV_WIDTH + c0 + LANES)
        ch = slice(2 * CONV_WIDTH + c0, 2 * CONV_WIDTH + c0 + LANES)
        cu = p_ref[:, cc].astype(F32) * p_ref[:, ch].astype(F32)
        halo_p = cp_ref[last:last + 1, cl].astype(F32) * hp_ref[last:last + 1, cl].astype(F32)
        halo_n = cn_ref[0:1, cl].astype(F32) * hn_ref[0:1, cl].astype(F32)
        halo_p = jnp.where(i > 0, halo_p, 0.0)
        halo_n = jnp.where(i < nb - 1, halo_n, 0.0)
        cu_prev, cu_next = _shift_rows(cu, halo_p, halo_n)
        y_conv = p_ref[:, cl].astype(F32) * (cu_prev * cw_ref[0:1, cl] + cu * cw_ref[1:2, cl] + cu_next * cw_ref[2:3, cl])
        o_ref[:, cl] = y_conv.astype(o_ref.dtype)

    n_ctx = ckv_ref.shape[0]
    n_keys = 3 * BLOCK + n_ctx
    qi = lax.broadcasted_iota(jnp.int32, (BLOCK, n_keys), 0)
    kj = lax.broadcasted_iota(jnp.int32, (BLOCK, n_keys), 1)
    key_pos = i * BLOCK + kj - BLOCK
    visible = (kj >= 3 * BLOCK) | ((jnp.abs(qi + BLOCK - kj) <= BLOCK) & (key_pos >= 0) & (key_pos < nb * BLOCK))
    nt = (((1,), (1,)), ((), ()))

    for kh in range(N_KV_HEADS):
        ko = KV_START + kh * HEAD_DIM
        vo = KV_START + KV_WIDTH + kh * HEAD_DIM
        kb = kh * HEAD_DIM
        vb = KV_WIDTH + kh * HEAD_DIM
        keys = jnp.concatenate([
            _rope(kvp_ref[:, kb:kb + HEAD_DIM].astype(F32), csp_ref[...], snp_ref[...]).astype(BF16),
            _rope(p_ref[:, ko:ko + HEAD_DIM].astype(F32), cs, sn).astype(BF16),
            _rope(kvn_ref[:, kb:kb + HEAD_DIM].astype(F32), csn_ref[...], snn_ref[...]).astype(BF16),
            ckv_ref[:, kb:kb + HEAD_DIM],
        ], axis=0)
        vals = jnp.concatenate([kvp_ref[:, vb:vb + HEAD_DIM], p_ref[:, vo:vo + HEAD_DIM],
                                kvn_ref[:, vb:vb + HEAD_DIM], ckv_ref[:, vb:vb + HEAD_DIM]], axis=0)
        for g in range(GQA_GROUP):
            h = kh * GQA_GROUP + g
            qo = Q_START + h * HEAD_DIM
            q = (_rope(p_ref[:, qo:qo + HEAD_DIM].astype(F32), cs, sn) * ATTN_SCALE).astype(BF16)
            sc = jnp.where(visible, lax.dot_general(q, keys, nt, preferred_element_type=F32), -jnp.inf)
            sink = sink_ref[h]
            m = jnp.maximum(jnp.max(sc, axis=1, keepdims=True), sink)
            e = jnp.exp(sc - m)
            denom = jnp.exp(sink - m) + jnp.sum(e, axis=1, keepdims=True)
            o = jnp.dot(e.astype(BF16), vals, preferred_element_type=F32) / denom
            oo = CONV_WIDTH + h * HEAD_DIM
            o_ref[:, oo:oo + HEAD_DIM] = o.astype(o_ref.dtype)


def even_mixer_core(p, ckv, cs, sn, conv_w, sink):
    s = p.shape[0]
    nb = s // BLOCK
    hb = BLOCK // BF16_SUBLANES
    kvc = KV_START // (2 * KV_WIDTH)
    prev = lambda i: jnp.maximum(i - 1, 0)
    nxt = lambda i: jnp.minimum(i + 1, nb - 1)
    tab = lambda f: pl.BlockSpec((BLOCK, HEAD_DIM), lambda i: (f(i), 0))
    same = lambda i: i
    in_specs = [
        pl.BlockSpec(memory_space=pltpu.SMEM),
        pl.BlockSpec((BLOCK, IN_WIDTH), lambda i: (i, 0)),
        pl.BlockSpec((BLOCK, 2 * KV_WIDTH), lambda i: (prev(i), kvc)),
        pl.BlockSpec((BLOCK, 2 * KV_WIDTH), lambda i: (nxt(i), kvc)),
        pl.BlockSpec((BF16_SUBLANES, CONV_WIDTH), lambda i: (jnp.maximum(i * hb - 1, 0), 1)),
        pl.BlockSpec((BF16_SUBLANES, CONV_WIDTH), lambda i: (jnp.maximum(i * hb - 1, 0), 2)),
        pl.BlockSpec((BF16_SUBLANES, CONV_WIDTH), lambda i: (jnp.minimum((i + 1) * hb, nb * hb - 1), 1)),
        pl.BlockSpec((BF16_SUBLANES, CONV_WIDTH), lambda i: (jnp.minimum((i + 1) * hb, nb * hb - 1), 2)),
        pl.BlockSpec(ckv.shape, lambda i: (0, 0)),
        tab(same), tab(same), tab(prev), tab(prev), tab(nxt), tab(nxt),
        pl.BlockSpec(conv_w.shape, lambda i: (0, 0)),
    ]
    return pl.pallas_call(
        _mixer_kernel,
        grid=(nb,),
        in_specs=in_specs,
        out_specs=pl.BlockSpec((BLOCK, CONV_WIDTH + Q_WIDTH), lambda i: (i, 0)),
        out_shape=jax.ShapeDtypeStruct((s, CONV_WIDTH + Q_WIDTH), BF16),
        compiler_params=_params(("parallel",), 32 << 20),
        name="even_mixer_core",
    )(sink, p, p, p, p, p, p, p, ckv, cs, sn, cs, sn, cs, sn, conv_w)


def rope_tables(s):
    t = np.arange(s)
    inv = ROPE_BASE ** (-np.arange(ROPE_PAIRS, dtype=np.float64) / ROPE_PAIRS)
    ang_r = (t // GRID_W)[:, None] * inv[None, :]
    ang_c = (t % GRID_W)[:, None] * inv[None, :]
    cs = np.concatenate([np.cos(ang_r)] * 2 + [np.cos(ang_c)] * 2, axis=1)
    sn = np.concatenate([-np.sin(ang_r), np.sin(ang_r), -np.sin(ang_c), np.sin(ang_c)], axis=1)
    return cs.astype(np.float32), sn.astype(np.float32)


def _hyena_in_kernel(*refs, side_chunks):
    a_ref, ap_ref, an_ref, w_ref, b_ref, cw_ref, cb_ref = refs[:7]
    o_ref, ax_ref = (refs[8], refs[10]) if side_chunks else (refs[7], refs[8])
    i, j = pl.program_id(0), pl.program_id(1)
    halo = ap_ref.shape[0]
    tm = a_ref.shape[0]
    rows = ax_ref.shape[0]

    @pl.when(j == 0)
    def _():
        ax_ref[0:halo, :] = ap_ref[...]
        ax_ref[halo:halo + tm, :] = a_ref[...]
        ax_ref[halo + tm:, :] = an_ref[...]

    for c0 in range(0, w_ref.shape[1], V7X_MXU_COLS):
        cl = slice(c0, c0 + V7X_MXU_COLS)
        u = jnp.dot(ax_ref[...], w_ref[:, cl], preferred_element_type=F32) + b_ref[:, cl]
        r = lax.broadcasted_iota(jnp.int32, u.shape, 0)
        prev = jnp.where((r == halo) & (i == 0), 0.0, pltpu.roll(u, 1, 0))
        nxt = jnp.where((r == halo + tm - 1) & (i == pl.num_programs(0) - 1), 0.0, pltpu.roll(u, rows - 1, 0))
        y = prev * cw_ref[0:1, cl] + u * cw_ref[1:2, cl] + nxt * cw_ref[2:3, cl] + cb_ref[:, cl]
        o_ref[:, cl] = y[halo:halo + tm].astype(o_ref.dtype)

    if side_chunks:
        _side_cast(refs[7], refs[9], i * pl.num_programs(1) + j, side_chunks)


def hyena_in_proj(a, w, bias, conv_w, conv_b, col0, width, out_dtype, side=None, side_chunks=0, tm=1024, tn=512):
    s, k = a.shape
    halo = BF16_SUBLANES
    nj, j0 = width // tn, col0 // tn
    hb = tm // halo
    col = lambda rows: pl.BlockSpec((rows, tn), lambda i, j: (0, j + j0))
    in_specs = [
        pl.BlockSpec((tm, k), lambda i, j: (i, 0)),
        pl.BlockSpec((halo, k), lambda i, j: (jnp.maximum(i * hb - 1, 0), 0)),
        pl.BlockSpec((halo, k), lambda i, j: (jnp.minimum((i + 1) * hb, s // halo - 1), 0)),
        pl.BlockSpec((k, tn), lambda i, j: (0, j + j0)),
        col(1), col(3), col(1),
    ]
    args = [a, a, a, w, bias, conv_w, conv_b]
    osz = jnp.dtype(out_dtype).itemsize
    vmem = 2 * tm * k * 2 + (tm + 2 * halo) * k * 2 + 2 * k * tn * 2 + 2 * tm * tn * osz + 8 * tm * tn * 4 + (2 << 20)
    out_specs = pl.BlockSpec((tm, tn), lambda i, j: (i, j))
    out_shape = jax.ShapeDtypeStruct((s, width), out_dtype)
    if side is not None:
        assert side_chunks <= (s // tm) * nj
        spec, shape, side_vmem = _side_cast_plumbing(side, side_chunks, lambda i, j: i * nj + j)
        in_specs.append(spec)
        args.append(side)
        out_specs, out_shape, vmem = [out_specs, spec], [out_shape, shape], vmem + side_vmem
    return pl.pallas_call(
        functools.partial(_hyena_in_kernel, side_chunks=side_chunks),
        grid=(s // tm, nj),
        in_specs=in_specs,
        out_specs=out_specs,
        out_shape=out_shape,
        scratch_shapes=[pltpu.VMEM((tm + 2 * halo, k), BF16)],
        compiler_params=_params(("arbitrary", "arbitrary"), vmem),
        name="hyena_in_proj",
    )(*args)


def _hid_kernel(z_ref, w1_ref, b1_ref, w2_ref, b2_ref, w3_ref, b3_ref, fr_ref, o_ref):
    hp = lax.Precision.HIGHEST
    fr = fr_ref[...]
    h = jnp.sin(fr * (jnp.dot(z_ref[...], w1_ref[...], preferred_element_type=F32, precision=hp) + b1_ref[...]))
    h = jnp.sin(fr * (jnp.dot(h, w2_ref[...], preferred_element_type=F32, precision=hp) + b2_ref[...]))
    o_ref[...] = jnp.sin(fr * (jnp.dot(h, w3_ref[...], preferred_element_type=F32, precision=hp) + b3_ref[...]))


def _pad2(a, r, c):
    return jnp.zeros((r, c), F32).at[:a.shape[0], :a.shape[1]].set(a.astype(F32))


def filter_hidden(length, w1, b1, freq, w2, b2, w3, b3, tl=2048):
    t = np.linspace(0.0, 1.0, length)[:, None]
    w = (2.0 * math.pi / length) * np.arange(length)[:, None]
    bands = np.linspace(1e-4, FILTER_BANDS - 1, FILTER_BANDS)[None]
    z = np.concatenate([t, np.cos(bands * w), -np.sin(bands * w)], axis=-1)
    zp = np.zeros((length, LANES), np.float32)
    zp[:, :FILTER_EMB] = z
    full = pl.BlockSpec((LANES, LANES), lambda i: (0, 0))
    row = pl.BlockSpec((1, LANES), lambda i: (0, 0))
    return pl.pallas_call(
        _hid_kernel,
        grid=(length // tl,),
        in_specs=[pl.BlockSpec((tl, LANES), lambda i: (i, 0)), full, row, full, row, full, row, row],
        out_specs=pl.BlockSpec((tl, LANES), lambda i: (i, 0)),
        out_shape=jax.ShapeDtypeStruct((length, LANES), F32),
        compiler_params=_params(("parallel",), 16 << 20),
        name="filter_hidden",
    )(zp, _pad2(w1, LANES, LANES), _pad2(b1[None], 1, LANES), _pad2(w2, LANES, LANES), _pad2(b2[None], 1, LANES),
      _pad2(w3, LANES, LANES), _pad2(b3[None], 1, LANES), _pad2(freq[None], 1, LANES))


FFT_N1 = 256
FFT_N2 = 128
FFT_K1 = FFT_N1 // 2
FFT_GROUP = BF16_SUBLANES
FFT_PITCH = 3 * F32_SUBLANES


@functools.lru_cache(maxsize=None)
def _phase_tables(length):
    n1h, n2 = FFT_K1, FFT_N2
    assert length == n1h * n2
    n = 2 * length
    ia = np.arange(n1h, dtype=np.int64)
    tha = ((ia[None, :] * (2 * ia[:, None] + 1)) % (2 * FFT_N1)) * (2.0 * np.pi / (2 * FFT_N1))
    fa = np.concatenate([np.cos(tha), -np.sin(tha)], axis=0)
    ca = np.concatenate([np.cos(tha).T, -np.sin(tha).T], axis=1) * (2.0 / n)
    k1 = ia[:, None, None]
    k2 = np.arange(n2, dtype=np.int64)[None, :, None]
    m2 = np.arange(n2, dtype=np.int64)[None, None, :]
    phb = ((m2 * (2 * (k1 + FFT_N1 * k2) + 1)) % (2 * n)) * (2.0 * np.pi / (2 * n))
    gr, gi = np.cos(phb), -np.sin(phb)
    gb = np.concatenate([np.concatenate([gr, -gi], axis=2), np.concatenate([gi, gr], axis=2)], axis=1)
    hr, hi = np.swapaxes(gr, 1, 2), -np.swapaxes(gi, 1, 2)
    gbi = np.concatenate([np.concatenate([hr, -hi], axis=2), np.concatenate([hi, hr], axis=2)], axis=1)
    return tuple(t.astype(np.float32).astype(BF16) for t in (fa, ca, gb, gbi))


def _regroup_rows(k):
    return pl.ds(pl.multiple_of(k * FFT_PITCH, F32_SUBLANES), FFT_GROUP)


def _fft_a_core(rows_of, fa_ref, o_ref, s_refs, ns):
    fa = fa_ref[...]
    for j in range(FFT_GROUP):
        res = jnp.dot(fa, rows_of(j).astype(BF16), preferred_element_type=F32)
        for p in range(2):
            for s in range(ns):
                s_refs[p * ns + s][pl.ds(j, FFT_K1, stride=FFT_PITCH), :] = (
                    res[p * FFT_K1:(p + 1) * FFT_K1, s * LANES:(s + 1) * LANES])

    def emit(k, carry):
        for p in range(2):
            for s in range(ns):
                o_ref[0, k, p, :, s * LANES:(s + 1) * LANES] = s_refs[p * ns + s][_regroup_rows(k), :].astype(o_ref.dtype)
        return carry

    lax.fori_loop(0, FFT_K1, emit, 0, unroll=8)


def _stage_rows(dst_ref, blk):
    pad = jnp.zeros((FFT_K1, FFT_PITCH - FFT_GROUP, LANES), F32)
    dst_ref[...] = jnp.concatenate([blk, pad], axis=1).reshape(FFT_K1 * FFT_PITCH, LANES)


def _staged_rows_of(refs):
    return lambda j: jnp.concatenate([r[pl.ds(j, FFT_K1, stride=FFT_PITCH), :] for r in refs], axis=1)


def _fft_a_kernel(*refs, ns):
    x_refs, fa_ref, o_ref = refs[:ns], refs[ns], refs[ns + 1]
    xs_refs = refs[ns + 2:2 * ns + 2]
    for s in range(ns):
        _stage_rows(xs_refs[s], x_refs[s][...])
    _fft_a_core(_staged_rows_of(xs_refs), fa_ref, o_ref, refs[2 * ns + 2:], ns)


def _fft_a_filter_kernel(hid_ref, w_ref, delta_ref, fa_ref, o_ref, n_ref, *scratch, ns, length):
    fs_refs, s_refs = scratch[:ns], scratch[ns:]
    g = pl.program_id(2)

    @pl.when(g == 0)
    def _():
        n_ref[...] = jnp.zeros_like(n_ref)

    rows = FFT_K1 * FFT_GROUP
    hid = hid_ref[...].reshape(rows, LANES).astype(BF16)
    r = lax.broadcasted_iota(jnp.int32, (rows, ns * LANES), 0)
    pos = (r // FFT_GROUP) * FFT_N2 + g * FFT_GROUP + r % FFT_GROUP
    decay = jnp.exp(-(pos.astype(F32) / float(length - 1)) * delta_ref[...])
    f = jnp.dot(hid, w_ref[0].astype(BF16), preferred_element_type=F32) * decay
    n_ref[0, 0:1, :] += jnp.sum(jnp.abs(f), axis=0, keepdims=True)
    for s in range(ns):
        _stage_rows(fs_refs[s], f[:, s * LANES:(s + 1) * LANES].reshape(FFT_K1, FFT_GROUP, LANES))
    _fft_a_core(_staged_rows_of(fs_refs), fa_ref, o_ref, s_refs, ns)


def _fft_a_scratch(ns):
    return [pltpu.VMEM((FFT_K1 * FFT_PITCH, LANES), F32)] * (3 * ns)


def _fft_a_vmem(ns):
    blk = FFT_K1 * FFT_GROUP * LANES
    return 2 * ns * blk * 4 + 2 * 2 * ns * blk * 2 + 3 * ns * FFT_K1 * FFT_PITCH * LANES * 4 + (8 << 20)


def fft_filter_pass_a(hid, w_o, length, dc=256):
    q, _, d = w_o.shape
    ns = dc // LANES
    delta = jnp.abs(jnp.linspace(MIN_DECAY, MAX_DECAY, d, dtype=F32))[None]
    hid3 = hid.reshape(FFT_K1, FFT_N2, LANES)
    fa = _phase_tables(length)[0]
    return pl.pallas_call(
        functools.partial(_fft_a_filter_kernel, ns=ns, length=length),
        grid=(q, d // dc, FFT_N2 // FFT_GROUP),
        in_specs=[pl.BlockSpec((FFT_K1, FFT_GROUP, LANES), lambda w, ci, g: (0, g, 0)),
                  pl.BlockSpec((1, LANES, dc), lambda w, ci, g: (w, 0, ci)),
                  pl.BlockSpec((1, dc), lambda w, ci, g: (0, ci)),
                  pl.BlockSpec(fa.shape, lambda w, ci, g: (0, 0))],
        out_specs=[pl.BlockSpec((1, FFT_K1, 2, FFT_GROUP, dc), lambda w, ci, g: (w, 0, 0, g, ci)),
                   pl.BlockSpec((1, F32_SUBLANES, dc), lambda w, ci, g: (w, 0, ci))],
        out_shape=[jax.ShapeDtypeStruct((q, FFT_K1, 2, FFT_N2, d), BF16),
                   jax.ShapeDtypeStruct((q, F32_SUBLANES, d), F32)],
        scratch_shapes=_fft_a_scratch(ns),
        compiler_params=_params(("parallel", "parallel", "arbitrary"), _fft_a_vmem(ns)),
        name="fft_filter_pass_a",
    )(hid3, w_o, delta, fa)


def fft_pass_a(x, col0, d, dc=256):
    q, length, c = x.shape
    ns = dc // LANES
    x4 = x.reshape(q, FFT_K1, FFT_N2, c)
    slab0 = col0 // LANES
    in_specs = [pl.BlockSpec((None, FFT_K1, FFT_GROUP, LANES),
                             functools.partial(lambda w, ci, g, s: (w, 0, g, slab0 + ci * ns + s), s=s))
                for s in range(ns)]
    fa = _phase_tables(length)[0]
    in_specs.append(pl.BlockSpec(fa.shape, lambda w, ci, g: (0, 0)))
    return pl.pallas_call(
        functools.partial(_fft_a_kernel, ns=ns),
        grid=(q, d // dc, FFT_N2 // FFT_GROUP),
        in_specs=in_specs,
        out_specs=pl.BlockSpec((1, FFT_K1, 2, FFT_GROUP, dc), lambda w, ci, g: (w, 0, 0, g, ci)),
        out_shape=jax.ShapeDtypeStruct((q, FFT_K1, 2, FFT_N2, d), BF16),
        scratch_shapes=_fft_a_scratch(ns),
        compiler_params=_params(("parallel", "parallel", "parallel"), _fft_a_vmem(ns)),
        name="fft_pass_a",
    )(*([x4] * ns), fa)


def _fft_b_kernel(a_ref, af_ref, ab_ref, gb_ref, gbi_ref, o_ref):
    n2 = FFT_N2
    dc = a_ref.shape[2]
    for i in range(a_ref.shape[0]):
        rhs = jnp.concatenate([a_ref[i], af_ref[0, i], ab_ref[0, i]], axis=1)
        res = jnp.dot(gb_ref[i], rhs, preferred_element_type=F32)
        u, uf, ub = res[:, 0:dc], res[:, dc:2 * dc], res[:, 2 * dc:3 * dc]
        ur, ui = u[0:n2], u[n2:2 * n2]
        kr, ki = uf[0:n2] + ub[0:n2], uf[n2:2 * n2] - ub[n2:2 * n2]
        v = jnp.concatenate([ur * kr - ui * ki, ur * ki + ui * kr], axis=0).astype(BF16)
        o_ref[i] = jnp.dot(gbi_ref[i], v, preferred_element_type=F32).astype(o_ref.dtype)


def fft_pass_b(a, filt_a, order, gb, gbi, kb=8, dc=512):
    k1, r, d = a.shape
    vmem = 2 * 4 * kb * r * dc * 2 + 2 * 2 * kb * r * r * 2 + 10 * r * dc * 4 + (4 << 20)
    return pl.pallas_call(
        _fft_b_kernel,
        grid=(k1 // kb, d // dc),
        in_specs=[pl.BlockSpec((kb, r, dc), lambda k, c: (k, 0, c)),
                  pl.BlockSpec((1, kb, r, dc), lambda k, c: (2 * order, k, 0, c)),
                  pl.BlockSpec((1, kb, r, dc), lambda k, c: (2 * order + 1, k, 0, c)),
                  pl.BlockSpec((kb, r, r), lambda k, c: (k, 0, 0)),
                  pl.BlockSpec((kb, r, r), lambda k, c: (k, 0, 0))],
        out_specs=pl.BlockSpec((kb, r, dc), lambda k, c: (k, 0, c)),
        out_shape=jax.ShapeDtypeStruct((k1, r, d), BF16),
        compiler_params=_params(("parallel", "parallel"), vmem),
        name="fft_pass_b",
    )(a, filt_a, filt_a, gb, gbi)


def _fft_c_kernel(*refs, ns):
    b_ref, ca_ref, gate_refs, z_refs = refs[0], refs[1], refs[2:2 + ns], refs[2 + ns:2 + 2 * ns]
    n_ref, bias_ref, o_ref = refs[2 + 2 * ns:5 + 2 * ns]
    s_refs = refs[5 + 2 * ns:5 + 4 * ns]
    t_refs = refs[5 + 4 * ns:]
    ca = ca_ref[...]

    def spread(k, carry):
        for p in range(2):
            for s in range(ns):
                s_refs[p * ns + s][_regroup_rows(k), :] = b_ref[k, p, :, s * LANES:(s + 1) * LANES].astype(F32)
        return carry

    lax.fori_loop(0, FFT_K1, spread, 0, unroll=8)
    for j in range(FFT_GROUP):
        b = jnp.concatenate([
            jnp.concatenate([s_refs[p * ns + s][pl.ds(j, FFT_K1, stride=FFT_PITCH), :] for s in range(ns)], axis=1)
            for p in range(2)], axis=0)
        y = jnp.dot(ca, b.astype(BF16), preferred_element_type=F32)
        for s in range(ns):
            t_refs[s][pl.ds(j, FFT_K1, stride=FFT_PITCH), :] = y[:, s * LANES:(s + 1) * LANES]

    def emit(k, carry):
        for s in range(ns):
            sl = slice(s * LANES, (s + 1) * LANES)
            y = t_refs[s][_regroup_rows(k), :] / (n_ref[0:1, sl] + n_ref[1:2, sl] + 1e-6)
            o_ref[k, :, sl] = (gate_refs[s][k].astype(F32) * (y + bias_ref[:, sl] * z_refs[s][k])).astype(o_ref.dtype)
        return carry

    lax.fori_loop(0, FFT_K1, emit, 0, unroll=8)


def fft_pass_c(bp, ca, gate_arr, gate_col0, z_arr, z_col0, nsum, bias, out_dtype, dc=256):
    k1, r, d = bp.shape
    ns = dc // LANES
    length = FFT_K1 * FFT_N2
    b4 = bp.reshape(k1, 2, FFT_N2, d)
    g3 = gate_arr.reshape(FFT_K1, FFT_N2, gate_arr.shape[1])
    z3 = z_arr.reshape(FFT_K1, FFT_N2, z_arr.shape[1])
    sig = lambda slab0: [pl.BlockSpec((FFT_K1, FFT_GROUP, LANES),
                                      functools.partial(lambda ci, g, s: (0, g, slab0 + ci * ns + s), s=s))
                         for s in range(ns)]
    row = pl.BlockSpec((1, dc), lambda ci, g: (0, ci))
    blk = FFT_K1 * FFT_GROUP * dc
    vmem = 2 * 2 * blk * 2 + 2 * 3 * blk * 4 + 3 * ns * FFT_K1 * FFT_PITCH * LANES * 4 + (8 << 20)
    out = pl.pallas_call(
        functools.partial(_fft_c_kernel, ns=ns),
        grid=(d // dc, FFT_N2 // FFT_GROUP),
        in_specs=[pl.BlockSpec((k1, 2, FFT_GROUP, dc), lambda ci, g: (0, 0, g, ci)),
                  pl.BlockSpec(ca.shape, lambda ci, g: (0, 0)),
                  *sig(gate_col0 // LANES), *sig(z_col0 // LANES),
                  pl.BlockSpec((2, dc), lambda ci, g: (0, ci)), row],
        out_specs=pl.BlockSpec((FFT_K1, FFT_GROUP, dc), lambda ci, g: (0, g, ci)),
        out_shape=jax.ShapeDtypeStruct((FFT_K1, FFT_N2, d), out_dtype),
        scratch_shapes=[pltpu.VMEM((FFT_K1 * FFT_PITCH, LANES), F32)] * (3 * ns),
        compiler_params=_params(("parallel", "parallel"), vmem),
        name="fft_pass_c",
    )(b4, ca, *([g3] * ns), *([z3] * ns), nsum, bias)
    return out.reshape(length, d)


def kernel(x, c, ctx, c_ctx, ada_w, ada_b, norm_g, mix_w_in, mix_conv_w, mix_sink, mix_w_out, ffn_w_gu, ffn_w_down, hy_w_in, hy_b_in, hy_conv_w, hy_conv_b, hf_w1, hf_b1, hf_freq, hf_w2, hf_b2, hf_w3, hf_b3, hf_w_out, hf_bias, hy_w_out, hy_b_out, router_w, router_b, moe_w_gu, moe_w_down):
    assert x.shape[0] == 1 and ada_w.shape[0] == 2
    s, d = x.shape[1], x.shape[2]
    xs = x[0]
    ctxs = ctx[0]

    acts = jnp.zeros((F32_SUBLANES, d), F32).at[0].set(c[0]).at[1].set(c_ctx)
    mods = adaln_all(acts, ada_w, ada_b)
    row = lambda layer, r, k: mods[layer, r:r + 1, k * d:(k + 1) * d]

    g = norm_g[0]
    h = norm_mod(xs, g[0:1], row(0, 0, 0), row(0, 0, 1))
    hc = norm_mod(ctxs, g[0:1], row(0, 1, 0), row(0, 1, 1))
    w_in = mix_w_in[0].astype(BF16)
    p, ffn_gu = matmul(h, w_in, out_dtype=BF16, side=ffn_w_gu[0], side_chunks=256)
    ckv = matmul(hc, w_in[:, KV_START:], out_dtype=BF16)
    cs, sn = rope_tables(s)
    y = even_mixer_core(p, ckv, cs, sn, mix_conv_w[0], mix_sink[0])
    out = matmul(y, mix_w_out[0].astype(BF16), out_dtype=BF16)
    xs, h = post_norm(xs, out, g[1:2], row(0, 0, 2), g[2:3], row(0, 0, 3), row(0, 0, 4))

    act, ffn_down = swiglu_up(h, ffn_gu, tm=2048, tn=256, side=ffn_w_down[0], side_chunks=344)
    out, hy_in = matmul(act, ffn_down, out_dtype=BF16, tm=1024, tn=256, a_buffers=1, side=hy_w_in[0], side_chunks=256)
    g1 = norm_g[1]
    xs, h = post_norm(xs, out, g[3:4], row(0, 0, 5), g1[0:1], row(1, 0, 0), row(1, 0, 1))

    g = g1
    moe_gu_f32 = moe_w_gu[0].reshape(-1, moe_w_gu.shape[-1])
    hy_args = (h, hy_in, hy_b_in[0][None], hy_conv_w[0], hy_conv_b[0][None])
    x12, moe_gu = hyena_in_proj(*hy_args, 0, 2 * d, BF16, side=moe_gu_f32, side_chunks=256)
    v = hyena_in_proj(*hy_args, 2 * d, d, F32)
    hid = filter_hidden(s, hf_w1[0], hf_b1[0], hf_freq[0], hf_w2[0], hf_b2[0], hf_w3[0], hf_b3[0])
    w_o = hf_w_out[0].astype(F32).reshape(FILTER_HIDDEN, HYENA_ORDER * 2, d).transpose(1, 0, 2)
    w_o = jnp.zeros((HYENA_ORDER * 2, LANES, d), F32).at[:, :FILTER_HIDDEN].set(w_o)
    filt_a, fsum = fft_filter_pass_a(hid, w_o, s)
    fsum = fsum[:, 0, :]
    _, ca, gb, gbi = _phase_tables(s)
    spec = lambda a: a.reshape(a.shape[0], FFT_K1, 2 * FFT_N2, d)
    filt_a = spec(filt_a)
    a = spec(fft_pass_a(v[None], 0, d))[0]
    z1 = fft_pass_c(fft_pass_b(a, filt_a, 0, gb, gbi), ca, x12, 0, v, 0, fsum[0:2], hf_bias[0, 0][None], F32)
    a = spec(fft_pass_a(z1[None], 0, d))[0]
    z2 = fft_pass_c(fft_pass_b(a, filt_a, 1, gb, gbi), ca, x12, d, z1, 0, fsum[2:4], hf_bias[0, 1][None], BF16)
    moe_down_f32 = moe_w_down[0].reshape(-1, moe_w_down.shape[-1])
    out, moe_down = matmul(z2, hy_w_out[0].astype(BF16), bias=hy_b_out[0][None], out_dtype=BF16,
                           side=moe_down_f32, side_chunks=128)
    xs, sel, wts = post_router_top2(xs, out, g[1:2], row(1, 0, 2), g[2:3], row(1, 0, 3), row(1, 0, 4),
                                    router_w[0], router_b[0])
    pos, src_tok, tile_expert, n_valid = moe_plan(sel)
    act = grouped_swiglu_up(xs, src_tok, g[2:3], row(1, 0, 3), row(1, 0, 4), moe_gu.reshape(moe_w_gu.shape[1:]),
                            tile_expert, n_valid)
    ys = grouped_down(act, moe_down.reshape(moe_w_down.shape[1:]), tile_expert, n_valid)
    xs = moe_combine_residual(xs, wts, ys, pos, g[3:4], row(1, 0, 5))
    return xs[None]
```
